```python
import math
import numpy as np
import jax
import jax.numpy as jnp
from jax import lax

D_MODEL = 1024
BATCH = 32
SEQ = 2048
DEPTH = 4

N_EVEN = (DEPTH + 1) // 2
N_ODD = DEPTH // 2

Q_BLOCK = 128
N_BUCKETS = 32
MAX_DISTANCE = 128
NEG_INF = -1e30

MLA_HEADS = 8
MLA_Q_RANK = 384
MLA_KV_RANK = 256
MLA_NOPE = 64
MLA_ROPE = 32
MLA_V = 64
ROPE_BASE = 10000.0
MLA_COLS = MLA_Q_RANK + MLA_KV_RANK + MLA_ROPE

NSA_HEADS = 8
NSA_GROUPS = 2
NSA_HPG = NSA_HEADS // NSA_GROUPS
NSA_DH = 64
CMP_LEN = 32
CMP_STRIDE = 16
CMP_HIDDEN = 256
SEL_LEN = 64
SEL_TOP = 8
SEL_CHUNK = 32
FORCE_SCORE = 1e4
WINDOW = 256
NSA_Q_COLS = NSA_HEADS * NSA_DH
NSA_KV_COLS = 3 * 2 * NSA_GROUPS * NSA_DH
NSA_GATE_COLS = 3 * NSA_HEADS
NSA_COLS = NSA_Q_COLS + NSA_KV_COLS + NSA_GATE_COLS
EVEN_IN = MLA_COLS + NSA_COLS
EVEN_OUT = MLA_HEADS * MLA_V + NSA_HEADS * NSA_DH

DIFF_HEADS = 8
DIFF_DH = 64
DIFF_IN = DIFF_HEADS * 6 * DIFF_DH
DIFF_OUT = DIFF_HEADS * 2 * DIFF_DH

BIAS_HEADS = 8

D_FF = 2816
CONV_W = 3

ALPHA = (2.0 * DEPTH) ** 0.25
BETA = (8.0 * DEPTH) ** -0.25
LN_EPS = 1e-5
RMS_EPS = 1e-6

kernel_name = 'hybrid_mla_nsa_diffattn_convffn_deepnorm_adaln'


def _f32(t):
    return t.astype(jnp.float32)


def _layer_norm(t, g, b):
    tf = _f32(t)
    mu = jnp.mean(tf, axis=-1, keepdims=True)
    var = jnp.mean(jnp.square(tf - mu), axis=-1, keepdims=True)
    return ((tf - mu) * lax.rsqrt(var + LN_EPS) * _f32(g) + _f32(b)).astype(t.dtype)


def _rms_norm(t, g):
    tf = _f32(t)
    return (tf * lax.rsqrt(jnp.mean(tf * tf, axis=-1, keepdims=True) + RMS_EPS) * _f32(g)).astype(t.dtype)


def _t5_bucket(dist):
    n = jnp.maximum(dist, 0)
    max_exact = N_BUCKETS // 2
    nf = jnp.maximum(n, 1).astype(jnp.float32)
    large = max_exact + (jnp.log(nf / max_exact) / math.log(MAX_DISTANCE / max_exact)
                         * (N_BUCKETS - max_exact)).astype(jnp.int32)
    large = jnp.minimum(large, N_BUCKETS - 1)
    return jnp.where(n < max_exact, n, large)


def _rope_tables(seq, dim):
    inv = 1.0 / (ROPE_BASE ** (jnp.arange(0, dim, 2, dtype=jnp.float32) / dim))
    ang = jnp.arange(seq, dtype=jnp.float32)[:, None] * inv[None, :]
    return jnp.cos(ang), jnp.sin(ang)


def _apply_rope(t, cos, sin):
    t1, t2 = jnp.split(t, 2, axis=-1)
    cs = cos[None, :, None, :].astype(t.dtype)
    sn = sin[None, :, None, :].astype(t.dtype)
    return jnp.concatenate([t1 * cs - t2 * sn, t1 * sn + t2 * cs], axis=-1)


def _mla(proj, q_norm, kv_norm, w_uq, w_ukv, cos, sin):
    B, S, _ = proj.shape
    c_q = _rms_norm(proj[..., :MLA_Q_RANK], q_norm)
    c_kv = _rms_norm(proj[..., MLA_Q_RANK:MLA_Q_RANK + MLA_KV_RANK], kv_norm)
    k_rope = _apply_rope(proj[..., MLA_Q_RANK + MLA_KV_RANK:][:, :, None, :], cos, sin)[:, :, 0]
    q = (c_q @ w_uq).reshape(B, S, MLA_HEADS, MLA_NOPE + MLA_ROPE)
    q_nope = q[..., :MLA_NOPE]
    q_rope = _apply_rope(q[..., MLA_NOPE:], cos, sin)
    kv = (c_kv @ w_ukv).reshape(B, S, MLA_HEADS, MLA_NOPE + MLA_V)
    k_nope, v = kv[..., :MLA_NOPE], kv[..., MLA_NOPE:]
    scale = (MLA_NOPE + MLA_ROPE) ** -0.5
    outs = []
    for qb in range(S // Q_BLOCK):
        q0, q1 = qb * Q_BLOCK, (qb + 1) * Q_BLOCK
        s = (jnp.einsum('bqhd,bkhd->bhqk', q_nope[:, q0:q1], k_nope[:, :q1])
             + jnp.einsum('bqhr,bkr->bhqk', q_rope[:, q0:q1], k_rope[:, :q1]))
        causal = np.arange(q0, q1)[:, None] >= np.arange(q1)[None, :]
        p = jax.nn.softmax(jnp.where(causal, _f32(s) * scale, NEG_INF), axis=-1)
        outs.append(jnp.einsum('bhqk,bkhd->bqhd', p.astype(v.dtype), v[:, :q1]))
    return jnp.concatenate(outs, axis=1).reshape(B, S, MLA_HEADS * MLA_V)


def _nsa(proj, cmp_pe, cmp_w1, cmp_w2, table):
    B, S, _ = proj.shape
    G, J, dh = NSA_GROUPS, NSA_HPG, NSA_DH
    q = proj[..., :NSA_Q_COLS].reshape(B, S, G, J, dh)
    kv = proj[..., NSA_Q_COLS:NSA_Q_COLS + NSA_KV_COLS].reshape(B, S, 3, 2, G, dh)
    gates = jax.nn.sigmoid(_f32(proj[..., NSA_Q_COLS + NSA_KV_COLS:]).reshape(B, S, G, J, 3)).astype(q.dtype)
    scale = dh ** -0.5
    tbl = table.reshape(N_BUCKETS, G, J)
    t_pos = np.arange(S)

    n_cmp = (S - CMP_LEN) // CMP_STRIDE + 1
    starts = np.arange(n_cmp) * CMP_STRIDE
    tok = starts[:, None] + np.arange(CMP_LEN)[None, :]

    def compress(t, pe, w1, w2):
        blocks = t[:, tok] + pe[:, None, :]
        flat = jnp.moveaxis(blocks, 3, 2).reshape(B, n_cmp, G, CMP_LEN * dh)
        return jax.nn.gelu(flat @ w1) @ w2

    k_cmp = compress(kv[:, :, 0, 0], cmp_pe[0], cmp_w1[0], cmp_w2[0])
    v_cmp = compress(kv[:, :, 0, 1], cmp_pe[1], cmp_w1[1], cmp_w2[1])
    d_cmp = t_pos[:, None] - (starts + CMP_LEN - 1)[None, :]
    b_cmp = jnp.moveaxis(tbl[_t5_bucket(jnp.asarray(d_cmp))], 1, -1)
    s_cmp = _f32(jnp.einsum('bsgjd,bngd->bsgjn', q, k_cmp)) * scale + _f32(b_cmp)
    p_cmp = jax.nn.softmax(jnp.where((d_cmp >= 0)[:, None, None, :], s_cmp, NEG_INF), axis=-1)
    p_cmp = jnp.where((t_pos >= CMP_LEN - 1)[:, None, None, None], p_cmp, 0.0)
    o_cmp = jnp.einsum('bsgjn,bngd->bsgjd', p_cmp.astype(q.dtype), v_cmp)

    n_slc = S // SEL_LEN
    top = min(SEL_TOP, n_slc)
    jb = np.arange(n_slc)
    overlap = ((starts[:, None] < (jb[None, :] + 1) * SEL_LEN)
               & (starts[:, None] + CMP_LEN > jb[None, :] * SEL_LEN)).astype(np.float32)
    imp = jnp.einsum('bsgn,nm->bsgm', jnp.sum(p_cmp, axis=3), jnp.asarray(overlap))
    cur = t_pos // SEL_LEN
    forced = (jb[None, :] == 0) | (jb[None, :] == cur[:, None]) | (jb[None, :] == cur[:, None] - 1)
    causal_blk = jb[None, :] * SEL_LEN <= t_pos[:, None]
    score = jnp.where(forced[:, None, :], FORCE_SCORE, imp)
    score = jnp.where(causal_blk[:, None, :], score, -1.0)
    _, sel = lax.top_k(score, top)

    k_s = jnp.moveaxis(kv[:, :, 1, 0].reshape(B, n_slc, SEL_LEN, G, dh), 3, 1)
    v_s = jnp.moveaxis(kv[:, :, 1, 1].reshape(B, n_slc, SEL_LEN, G, dh), 3, 1)
    n_chunk = S // SEL_CHUNK
    bi = jnp.arange(B)[:, None, None, None]
    gi = jnp.arange(G)[None, None, :, None]
    offs = jnp.arange(SEL_LEN)
    n_keys = top * SEL_LEN

    def sel_chunk(args):
        qc, ic, tc = args
        kg = k_s[bi, gi, ic].reshape(B, SEL_CHUNK, G, n_keys, dh)
        vg = v_s[bi, gi, ic].reshape(B, SEL_CHUNK, G, n_keys, dh)
        pos = (ic[..., None] * SEL_LEN + offs).reshape(B, SEL_CHUNK, G, n_keys)
        dist = tc[None, :, None, None] - pos
        bias = jnp.moveaxis(tbl[_t5_bucket(dist), gi], -1, 3)
        s = _f32(jnp.einsum('bcgjd,bcgnd->bcgjn', qc, kg)) * scale + _f32(bias)
        p = jax.nn.softmax(jnp.where((dist >= 0)[:, :, :, None, :], s, NEG_INF), axis=-1)
        return jnp.einsum('bcgjn,bcgnd->bcgjd', p.astype(vg.dtype), vg)

    def chunks(a):
        return jnp.moveaxis(a.reshape((B, n_chunk, SEL_CHUNK) + a.shape[2:]), 1, 0)

    o_slc = lax.map(sel_chunk, (chunks(q), chunks(sel), jnp.asarray(t_pos.reshape(n_chunk, SEL_CHUNK))))
    o_slc = jnp.moveaxis(o_slc, 0, 1).reshape(B, S, G, J, dh)

    nb = S // Q_BLOCK
    wb = WINDOW // Q_BLOCK
    kb_len = (wb + 1) * Q_BLOCK

    def band(t):
        tp = jnp.pad(t, ((0, 0), (WINDOW, 0), (0, 0), (0, 0))).reshape(B, nb + wb, Q_BLOCK, G, dh)
        return jnp.concatenate([tp[:, i:i + nb] for i in range(wb + 1)], axis=2)

    k_band, v_band = band(kv[:, :, 2, 0]), band(kv[:, :, 2, 1])
    d_win = np.arange(Q_BLOCK)[:, None] + WINDOW - np.arange(kb_len)[None, :]
    b_win = _f32(jnp.moveaxis(tbl[_t5_bucket(jnp.asarray(d_win))], (2, 3), (0, 1)))
    key_abs = np.arange(nb)[:, None] * Q_BLOCK - WINDOW + np.arange(kb_len)[None, :]
    m_win = ((d_win >= 0) & (d_win < WINDOW))[None] & (key_abs >= 0)[:, None, :]

    def win_block(args):
        qi, ki, vi, mi = args
        s = _f32(jnp.einsum('bqgjd,bkgd->bgjqk', qi, ki)) * scale + b_win
        p = jax.nn.softmax(jnp.where(mi, s, NEG_INF), axis=-1)
        return jnp.einsum('bgjqk,bkgd->bqgjd', p.astype(vi.dtype), vi)

    q_blocks = jnp.moveaxis(q.reshape(B, nb, Q_BLOCK, G, J, dh), 1, 0)
    o_win = lax.map(win_block, (q_blocks, jnp.moveaxis(k_band, 1, 0), jnp.moveaxis(v_band, 1, 0), jnp.asarray(m_win)))
    o_win = jnp.moveaxis(o_win, 0, 1).reshape(B, S, G, J, dh)

    o = gates[..., 0:1] * o_cmp + gates[..., 1:2] * o_slc + gates[..., 2:3] * o_win
    return o.reshape(B, S, NSA_HEADS * dh)


def _diff_attention(proj, lam_vec, subln, table, layer_idx):
    B, S, _ = proj.shape
    H, d = DIFF_HEADS, DIFF_DH
    q = proj[..., :H * 2 * d].reshape(B, S, H, 2, d)
    k = proj[..., H * 2 * d:H * 4 * d].reshape(B, S, H, 2, d)
    v = proj[..., H * 4 * d:].reshape(B, S, H, 2 * d)
    lam_init = 0.8 - 0.6 * math.exp(-0.3 * layer_idx)
    lv = _f32(lam_vec)
    lam = jnp.exp(jnp.sum(lv[0] * lv[1])) - jnp.exp(jnp.sum(lv[2] * lv[3])) + lam_init
    scale = d ** -0.5
    outs = []
    for qb in range(S // Q_BLOCK):
        q0, q1 = qb * Q_BLOCK, (qb + 1) * Q_BLOCK
        dist = np.arange(q0, q1)[:, None] - np.arange(q1)[None, :]
        bias = jnp.moveaxis(_f32(table[_t5_bucket(jnp.asarray(dist))]), -1, 0)
        causal = dist >= 0
        maps = []
        for i in range(2):
            s = _f32(jnp.einsum('bqhd,bkhd->bhqk', q[:, q0:q1, :, i], k[:, :q1, :, i])) * scale + bias
            maps.append(jax.nn.softmax(jnp.where(causal, s, NEG_INF), axis=-1))
        a = maps[0] - lam * maps[1]
        outs.append(jnp.einsum('bhqk,bkhd->bqhd', a.astype(v.dtype), v[:, :q1]))
    o = _rms_norm(jnp.concatenate(outs, axis=1), subln) * (1.0 - lam_init)
    return o.reshape(B, S, H * 2 * d)


def _conv_ffn(h, w_up, w_gate, conv_w, conv_b, w_down):
    u = h @ w_up
    a = lax.conv_general_dilated(h @ w_gate, conv_w[:, None, :], window_strides=(1,),
                                 padding=[(CONV_W - 1, 0)], dimension_numbers=('NWC', 'WIO', 'NWC'),
                                 feature_group_count=D_FF) + conv_b
    return (jax.nn.silu(a) * u) @ w_down


def setup_inputs(seed: int = 0) -> dict:
    key = jax.random.key(seed)
    ks = iter(jax.random.split(key, 32))

    def nrm(shape, std):
        return std * jax.random.normal(next(ks), shape, jnp.float32)

    D = D_MODEL
    return {
        'x': nrm((BATCH, SEQ, D), 1.0),
        'c': nrm((BATCH, D), 1.0),
        'rel_bias': nrm((N_BUCKETS, BIAS_HEADS), 0.3),
        'ev_w_in': nrm((N_EVEN, D, EVEN_IN), D ** -0.5),
        'mla_q_norm': 1.0 + nrm((N_EVEN, MLA_Q_RANK), 0.01),
        'mla_kv_norm': 1.0 + nrm((N_EVEN, MLA_KV_RANK), 0.01),
        'mla_w_uq': nrm((N_EVEN, MLA_Q_RANK, MLA_HEADS * (MLA_NOPE + MLA_ROPE)), MLA_Q_RANK ** -0.5),
        'mla_w_ukv': nrm((N_EVEN, MLA_KV_RANK, MLA_HEADS * (MLA_NOPE + MLA_V)), MLA_KV_RANK ** -0.5),
        'nsa_cmp_pe': nrm((N_EVEN, 2, CMP_LEN, NSA_DH), 0.02),
        'nsa_cmp_w1': nrm((N_EVEN, 2, CMP_LEN * NSA_DH, CMP_HIDDEN), (CMP_LEN * NSA_DH) ** -0.5),
        'nsa_cmp_w2': nrm((N_EVEN, 2, CMP_HIDDEN, NSA_DH), CMP_HIDDEN ** -0.5),
        'ev_w_o': nrm((N_EVEN, EVEN_OUT, D), BETA * EVEN_OUT ** -0.5),
        'od_w_in': nrm((N_ODD, D, DIFF_IN), D ** -0.5),
        'diff_lambda': nrm((N_ODD, 4, DIFF_DH), 0.1),
        'diff_subln': 1.0 + nrm((N_ODD, 2 * DIFF_DH), 0.01),
        'od_w_o': nrm((N_ODD, DIFF_OUT, D), BETA * DIFF_OUT ** -0.5),
        'ada_w': nrm((DEPTH, D, 6 * D), 0.2 * D ** -0.5),
        'ada_b': nrm((DEPTH, 6 * D), 0.01),
        'ln_g': 1.0 + nrm((DEPTH, 2, D), 0.01),
        'ln_b': nrm((DEPTH, 2, D), 0.01),
        'ffn_w_up': nrm((DEPTH, D, D_FF), D ** -0.5),
        'ffn_w_gate': nrm((DEPTH, D, D_FF), D ** -0.5),
        'ffn_conv_w': nrm((DEPTH, CONV_W, D_FF), CONV_W ** -0.5),
        'ffn_conv_b': nrm((DEPTH, D_FF), 0.01),
        'ffn_w_down': nrm((DEPTH, D_FF, D), BETA * D_FF ** -0.5),
    }


def reference(x, c, rel_bias, ev_w_in, mla_q_norm, mla_kv_norm, mla_w_uq, mla_w_ukv,
              nsa_cmp_pe, nsa_cmp_w1, nsa_cmp_w2, ev_w_o, od_w_in, diff_lambda, diff_subln, od_w_o,
              ada_w, ada_b, ln_g, ln_b, ffn_w_up, ffn_w_gate, ffn_conv_w, ffn_conv_b, ffn_w_down):
    S = x.shape[1]
    cos, sin = _rope_tables(S, MLA_ROPE)
    c_act = jax.nn.silu(c)
    for l in range(DEPTH):
        mod = c_act @ ada_w[l] + ada_b[l]
        sh1, sc1, g1, sh2, sc2, g2 = [m[:, None, :] for m in jnp.split(mod, 6, axis=-1)]
        i = l // 2
        h = x * (1.0 + sc1) + sh1
        if l % 2 == 0:
            proj = h @ ev_w_in[i]
            o_a = _mla(proj[..., :MLA_COLS], mla_q_norm[i], mla_kv_norm[i], mla_w_uq[i], mla_w_ukv[i], cos, sin)
            o_b = _nsa(proj[..., MLA_COLS:], nsa_cmp_pe[i], nsa_cmp_w1[i], nsa_cmp_w2[i], rel_bias)
            y = jnp.concatenate([o_a, o_b], axis=-1) @ ev_w_o[i]
        else:
            y = _diff_attention(h @ od_w_in[i], diff_lambda[i], diff_subln[i], rel_bias, l) @ od_w_o[i]
        x = _layer_norm(ALPHA * x + (1.0 + g1) * y, ln_g[l, 0], ln_b[l, 0])
        h = x * (1.0 + sc2) + sh2
        y = _conv_ffn(h, ffn_w_up[l], ffn_w_gate[l], ffn_conv_w[l], ffn_conv_b[l], ffn_w_down[l])
        x = _layer_norm(ALPHA * x + (1.0 + g2) * y, ln_g[l, 1], ln_b[l, 1])
    return x
```

```python
import functools
import math

import numpy as np
import jax
import jax.numpy as jnp
from jax import lax
from jax.experimental import pallas as pl
from jax.experimental.pallas import tpu as pltpu

F32 = jnp.float32
BF16 = jnp.bfloat16

D_MODEL = 1024
DEPTH = 4
N_BUCKETS = 32
MAX_DISTANCE = 128
NEG_INF = -1e30

MLA_HEADS = 8
MLA_Q_RANK = 384
MLA_KV_RANK = 256
MLA_NOPE = 64
MLA_ROPE = 32
MLA_V = 64
ROPE_BASE = 10000.0
MLA_COLS = MLA_Q_RANK + MLA_KV_RANK + MLA_ROPE

NSA_HEADS = 8
NSA_GROUPS = 2
NSA_HPG = NSA_HEADS // NSA_GROUPS
NSA_DH = 64
CMP_LEN = 32
CMP_STRIDE = 16
CMP_HIDDEN = 256
SEL_LEN = 64
SEL_TOP = 8
FORCE_SCORE = 1e4
WINDOW = 256
NSA_Q_COLS = NSA_HEADS * NSA_DH
NSA_KV_COLS = 3 * 2 * NSA_GROUPS * NSA_DH
NSA_GATE_COLS = 3 * NSA_HEADS

DIFF_HEADS = 8
DIFF_DH = 64
DIFF_IN = DIFF_HEADS * 6 * DIFF_DH

D_FF = 2816
ALPHA = (2.0 * DEPTH) ** 0.25
LN_EPS = 1e-5
RMS_EPS = 1e-6

LANES = 128
VMEM_LIMIT = 56 * 1024 * 1024

TA = 256
EVEN_W = 2048
MISC_COL = 640
NSA_Q_COL = 768
NSA_KV_COL = 1280
NSA_CMP_COL = 1792
N_CMP_PAD = 128
MASK_BUCKET = N_BUCKETS


def _cparams(sem):
    return pltpu.CompilerParams(dimension_semantics=sem, vmem_limit_bytes=VMEM_LIMIT)


def _nt(a, b):
    return lax.dot_general(a, b, (((1,), (1,)), ((), ())), preferred_element_type=F32)


def _mm(a, b):
    return jnp.dot(a, b, preferred_element_type=F32)


def _split3(v):
    hi = v.astype(BF16)
    r1 = v - hi.astype(F32)
    mid = r1.astype(BF16)
    lo = (r1 - mid.astype(F32)).astype(BF16)
    return hi, mid, lo


def _layer_norm_rows(z, g, b):
    mu = jnp.mean(z, axis=-1, keepdims=True)
    zc = z - mu
    var = jnp.mean(zc * zc, axis=-1, keepdims=True)
    return zc * lax.rsqrt(var + LN_EPS) * g + b


def _online_update(s, v, m_ref, l_ref, acc_ref):
    m_prev = m_ref[...]
    m_new = jnp.maximum(m_prev, jnp.max(s, axis=-1, keepdims=True))
    alpha = jnp.exp(m_prev - m_new)
    p = jnp.exp(s - m_new)
    l_ref[...] = alpha * l_ref[...] + jnp.sum(p, axis=-1, keepdims=True)
    acc_ref[...] = alpha * acc_ref[...] + _mm(p.astype(BF16), v)
    m_ref[...] = m_new


def _reset(m_ref, l_ref, acc_ref):
    m_ref[...] = jnp.full(m_ref.shape, NEG_INF, F32)
    l_ref[...] = jnp.zeros(l_ref.shape, F32)
    acc_ref[...] = jnp.zeros(acc_ref.shape, F32)


def _ada_kernel(c_ref, w_ref, b_ref, o_ref):
    c = c_ref[...]
    ca = c / (1.0 + jnp.exp(-c))
    w = w_ref[0]
    c_hi = ca.astype(BF16)
    c_lo = (ca - c_hi.astype(F32)).astype(BF16)
    w_hi = w.astype(BF16)
    w_lo = (w - w_hi.astype(F32)).astype(BF16)
    o_ref[0] = _mm(c_hi, w_hi) + _mm(c_hi, w_lo) + _mm(c_lo, w_hi) + b_ref[0]


def _ada_mod(c, ada_w, ada_b):
    B, D = c.shape
    n_out = ada_w.shape[-1]
    tn = 1024
    return pl.pallas_call(
        _ada_kernel,
        grid=(DEPTH, n_out // tn),
        in_specs=[
            pl.BlockSpec((B, D), lambda l, j: (0, 0)),
            pl.BlockSpec((1, D, tn), lambda l, j: (l, 0, j)),
            pl.BlockSpec((1, 1, tn), lambda l, j: (l, 0, j)),
        ],
        out_specs=pl.BlockSpec((1, B, tn), lambda l, j: (l, 0, j)),
        out_shape=jax.ShapeDtypeStruct((DEPTH, B, n_out), F32),
        compiler_params=_cparams(("parallel", "parallel")),
        name="ada_mod",
    )(c, ada_w, ada_b.reshape(DEPTH, 1, n_out))


def _expand_kernel(tbl_ref, idx_ref, o_ref):
    h = pl.program_id(0)
    idx = idx_ref[...]
    acc = jnp.full(idx.shape, NEG_INF, F32)
    for b in range(N_BUCKETS):
        acc = jnp.where(idx == b, tbl_ref[b, h], acc)
    o_ref[0] = acc


def _expand_bias(table, idx):
    R, C = idx.shape
    rt = 256
    return pl.pallas_call(
        _expand_kernel,
        grid=(table.shape[1], R // rt),
        in_specs=[
            pl.BlockSpec(memory_space=pltpu.SMEM),
            pl.BlockSpec((rt, C), lambda h, r: (r, 0)),
        ],
        out_specs=pl.BlockSpec((1, rt, C), lambda h, r: (h, r, 0)),
        out_shape=jax.ShapeDtypeStruct((table.shape[1], R, C), F32),
        compiler_params=_cparams(("parallel", "parallel")),
        name="expand_bias",
    )(table, idx)


def _t5_bucket(dist):
    n = jnp.maximum(dist, 0)
    max_exact = N_BUCKETS // 2
    nf = jnp.maximum(n, 1).astype(jnp.float32)
    large = max_exact + (jnp.log(nf / max_exact) / math.log(MAX_DISTANCE / max_exact)
                         * (N_BUCKETS - max_exact)).astype(jnp.int32)
    large = jnp.minimum(large, N_BUCKETS - 1)
    return jnp.where(n < max_exact, n, large)


FAR_BUCKET = N_BUCKETS - 1


def _bias_index_tiles(S):
    a = np.arange(TA)[:, None]
    b = np.arange(TA)[None, :]
    d0 = a - b
    d1 = TA + a - b
    t0 = jnp.where(jnp.asarray(d0 >= 0), _t5_bucket(jnp.asarray(d0)), MASK_BUCKET)
    t1 = _t5_bucket(jnp.asarray(d1))
    t1w = jnp.where(jnp.asarray(d1 < WINDOW), t1, MASK_BUCKET)
    toe = jnp.concatenate([t0, t1, t1w], axis=0).astype(jnp.int32)
    t_pos = np.arange(S)[:, None]
    n = np.arange(N_CMP_PAD)[None, :]
    d_cmp = t_pos - (n * CMP_STRIDE + CMP_LEN - 1)
    cmp_idx = jnp.where(jnp.asarray(d_cmp >= 0), _t5_bucket(jnp.asarray(d_cmp)), MASK_BUCKET).astype(jnp.int32)
    return toe, cmp_idx


def _modmm_kernel(x_ref, sc_ref, sh_ref, w_ref, o_ref, h_ref):
    @pl.when(pl.program_id(1) == 0)
    def _():
        h_ref[...] = (x_ref[...] * (1.0 + sc_ref[0]) + sh_ref[0]).astype(BF16)

    o_ref[...] = _mm(h_ref[...], w_ref[...]).astype(BF16)


def _mod_matmul(x, scale, shift, w, S):
    T, D = x.shape
    N = w.shape[1]
    tm, tn = 1024, 512
    per_seq = S // tm
    return pl.pallas_call(
        _modmm_kernel,
        grid=(T // tm, N // tn),
        in_specs=[
            pl.BlockSpec((tm, D), lambda i, j: (i, 0)),
            pl.BlockSpec((1, 1, D), lambda i, j: (i // per_seq, 0, 0)),
            pl.BlockSpec((1, 1, D), lambda i, j: (i // per_seq, 0, 0)),
            pl.BlockSpec((D, tn), lambda i, j: (0, j)),
        ],
        out_specs=pl.BlockSpec((tm, tn), lambda i, j: (i, j)),
        out_shape=jax.ShapeDtypeStruct((T, N), BF16),
        scratch_shapes=[pltpu.VMEM((tm, D), BF16)],
        compiler_params=_cparams(("parallel", "arbitrary")),
        name="mod_matmul",
    )(x, scale, shift, w)


def _mla_prep_kernel(p_ref, qn_ref, kvn_ref, wq_ref, wk_ref, wv_ref, er_ref, ct_ref, st_ref, mr_ref,
                     q_ref, k_ref, v_ref):
    p = p_ref[...]
    ql = p[:, :MLA_Q_RANK].astype(F32)
    kl = p[:, MLA_Q_RANK:MLA_Q_RANK + MLA_KV_RANK].astype(F32)
    misc = p[:, MISC_COL:MISC_COL + LANES].astype(F32)
    c_q = (ql * lax.rsqrt(jnp.mean(ql * ql, axis=-1, keepdims=True) + RMS_EPS) * qn_ref[...]).astype(BF16)
    c_kv = (kl * lax.rsqrt(jnp.mean(kl * kl, axis=-1, keepdims=True) + RMS_EPS) * kvn_ref[...]).astype(BF16)
    ab = _mm(c_q, wq_ref[...])
    ct = ct_ref[...]
    st = st_ref[...]
    half = MLA_HEADS * LANES
    for h in range(MLA_HEADS):
        lo = h * LANES
        q_ref[:, lo:lo + LANES] = (ab[:, lo:lo + LANES] * ct + ab[:, half + lo:half + lo + LANES] * st).astype(BF16)
    k_rope = _mm((misc * mr_ref[...]).astype(BF16), er_ref[...])
    k_ref[...] = (_mm(c_kv, wk_ref[...]) + k_rope).astype(BF16)
    v_ref[...] = _mm(c_kv, wv_ref[...]).astype(BF16)


def _mla_prep(proj, qn, kvn, wq, wk, wv, erope, ctab, stab, mrope, S):
    T = proj.shape[0]
    tm = 512
    per_seq = S // tm
    hw = MLA_HEADS * LANES
    const = lambda i: (0, 0)
    seq = lambda i: (i % per_seq, 0)
    return pl.pallas_call(
        _mla_prep_kernel,
        grid=(T // tm,),
        in_specs=[
            pl.BlockSpec((tm, NSA_Q_COL), lambda i: (i, 0)),
            pl.BlockSpec((1, MLA_Q_RANK), const),
            pl.BlockSpec((1, MLA_KV_RANK), const),
            pl.BlockSpec((MLA_Q_RANK, 2 * hw), const),
            pl.BlockSpec((MLA_KV_RANK, hw), const),
            pl.BlockSpec((MLA_KV_RANK, MLA_HEADS * MLA_V), const),
            pl.BlockSpec((LANES, hw), const),
            pl.BlockSpec((tm, LANES), seq),
            pl.BlockSpec((tm, LANES), seq),
            pl.BlockSpec((tm, LANES), seq),
        ],
        out_specs=[
            pl.BlockSpec((tm, hw), lambda i: (i, 0)),
            pl.BlockSpec((tm, hw), lambda i: (i, 0)),
            pl.BlockSpec((tm, MLA_HEADS * MLA_V), lambda i: (i, 0)),
        ],
        out_shape=[
            jax.ShapeDtypeStruct((T, hw), BF16),
            jax.ShapeDtypeStruct((T, hw), BF16),
            jax.ShapeDtypeStruct((T, MLA_HEADS * MLA_V), BF16),
        ],
        compiler_params=_cparams(("parallel",)),
        name="mla_prep",
    )(proj, qn, kvn, wq, wk, wv, erope, ctab, stab, mrope)


def _mla_attn_kernel(q_ref, k_ref, v_ref, cm_ref, o_ref, m_ref, l_ref, acc_ref, *, n_tiles):
    lane = lax.broadcasted_iota(jnp.int32, (TA, LANES), 1)

    def q_tile(qi, carry):
        r0 = pl.multiple_of(qi * TA, TA)
        outs = []
        for hh in range(2):
            cs = slice(hh * LANES, (hh + 1) * LANES)
            q = q_ref[pl.ds(r0, TA), cs]
            _reset(m_ref, l_ref, acc_ref)
            s = _nt(q, k_ref[pl.ds(r0, TA), cs]) + cm_ref[...]
            _online_update(s, v_ref[pl.ds(r0, TA), :], m_ref, l_ref, acc_ref)

            def kv_tile(kj, c):
                k0 = pl.multiple_of(kj * TA, TA)
                s = _nt(q, k_ref[pl.ds(k0, TA), cs])
                _online_update(s, v_ref[pl.ds(k0, TA), :], m_ref, l_ref, acc_ref)
                return c

            lax.fori_loop(0, qi, kv_tile, 0)
            outs.append(acc_ref[...] * (1.0 / l_ref[...]))
        o_ref[pl.ds(r0, TA), :] = jnp.where(lane < MLA_V, outs[0], outs[1]).astype(BF16)
        return carry

    lax.fori_loop(0, n_tiles, q_tile, 0)


def _mla_attention(q, k, v, cmask, B, S):
    pairs = MLA_HEADS // 2
    return pl.pallas_call(
        functools.partial(_mla_attn_kernel, n_tiles=S // TA),
        grid=(B, pairs),
        in_specs=[
            pl.BlockSpec((S, 2 * LANES), lambda b, hp: (b, hp)),
            pl.BlockSpec((S, 2 * LANES), lambda b, hp: (b, hp)),
            pl.BlockSpec((S, LANES), lambda b, hp: (b, hp)),
            pl.BlockSpec((TA, TA), lambda b, hp: (0, 0)),
        ],
        out_specs=pl.BlockSpec((S, LANES), lambda b, hp: (b, hp)),
        out_shape=jax.ShapeDtypeStruct((B * S, MLA_HEADS * MLA_V), BF16),
        scratch_shapes=[pltpu.VMEM((TA, 1), F32), pltpu.VMEM((TA, 1), F32), pltpu.VMEM((TA, LANES), F32)],
        compiler_params=_cparams(("parallel", "parallel")),
        name="mla_attention",
    )(q, k, v, cmask)


def _compress_kernel(r_ref, pe_ref, w1_ref, w2_ref, o_ref):
    half = CMP_STRIDE * NSA_DH
    r = r_ref[0, 0].astype(F32)
    pe = pe_ref[0]
    x_lo = (r + pe[:, :half]).astype(BF16)
    x_hi = (r + pe[:, half:]).astype(BF16)
    y_lo = _mm(x_lo, w1_ref[0, :half, :])
    y_hi = _mm(x_hi, w1_ref[0, half:, :])
    y = y_lo + pltpu.roll(y_hi, N_CMP_PAD - 1, 0)
    o_ref[0, 0] = _mm(jax.nn.gelu(y).astype(BF16), w2_ref[0]).astype(BF16)


def _compress(r, pe, w1, w2_rep, B):
    rep = NSA_HPG * NSA_DH
    return pl.pallas_call(
        _compress_kernel,
        grid=(4, B),
        in_specs=[
            pl.BlockSpec((1, 1, N_CMP_PAD, CMP_STRIDE * NSA_DH), lambda c, b: (c, b, 0, 0)),
            pl.BlockSpec((1, 1, CMP_LEN * NSA_DH), lambda c, b: (c // 2, 0, 0)),
            pl.BlockSpec((1, CMP_LEN * NSA_DH, CMP_HIDDEN), lambda c, b: (c // 2, 0, 0)),
            pl.BlockSpec((1, CMP_HIDDEN, rep), lambda c, b: (c // 2, 0, 0)),
        ],
        out_specs=pl.BlockSpec((1, 1, N_CMP_PAD, rep), lambda c, b: (c, b, 0, 0)),
        out_shape=jax.ShapeDtypeStruct((4, B, N_CMP_PAD, rep), BF16),
        compiler_params=_cparams(("parallel", "parallel")),
        name="nsa_compress",
    )(r, pe, w1, w2_rep)


def _nsa_kernel(tbl_ref, q_ref, kv_ref, misc_ref, kc_ref, vc_ref, bc_ref, toe_ref, erep_ref, ovl_ref,
                eexp_ref, egate_ref, o_ref,
                sk_ref, wk_ref, sv_ref, wv_ref, m_ref, l_ref, acc_ref, *, n_tiles):
    g = pl.program_id(1)
    J = NSA_HPG
    W = J * NSA_DH
    kv = kv_ref[...]
    sk_ref[...] = _mm(kv, erep_ref[0]).astype(BF16)
    wk_ref[...] = _mm(kv, erep_ref[1]).astype(BF16)
    sv_ref[...] = _mm(kv, erep_ref[2]).astype(BF16)
    wv_ref[...] = _mm(kv, erep_ref[3]).astype(BF16)

    lane_w = lax.broadcasted_iota(jnp.int32, (TA, W), 1)
    head_of_lane = lane_w >> 6
    row = lax.broadcasted_iota(jnp.int32, (TA, LANES), 0)
    lane = lax.broadcasted_iota(jnp.int32, (TA, LANES), 1)
    far = [tbl_ref[FAR_BUCKET, g * J + j] for j in range(J)]

    def stack(fn):
        return jnp.concatenate([fn(j) for j in range(J)], axis=0)

    def finish():
        return acc_ref[...] * (1.0 / l_ref[...])

    def q_tile(qi, carry):
        r0 = pl.multiple_of(qi * TA, TA)
        q = q_ref[pl.ds(r0, TA), :]
        q4 = stack(lambda j: jnp.where(head_of_lane == j, q, jnp.zeros_like(q)))
        t_abs = row + qi * TA

        s = _nt(q4, kc_ref[0, 0]) + stack(lambda j: bc_ref[j, pl.ds(r0, TA), :])
        m = jnp.max(s, axis=-1, keepdims=True)
        e = jnp.exp(s - m)
        p = e * (1.0 / jnp.sum(e, axis=-1, keepdims=True))
        p = jnp.where(jnp.concatenate([t_abs] * J, axis=0) >= CMP_LEN - 1, p, 0.0)
        o_cmp = _mm(p.astype(BF16), vc_ref[0, 0])
        p_sum = p[0:TA] + p[TA:2 * TA] + p[2 * TA:3 * TA] + p[3 * TA:4 * TA]

        hi, mid, lo = _split3(p_sum)
        imp = _mm(hi, ovl_ref[...]) + _mm(mid, ovl_ref[...]) + _mm(lo, ovl_ref[...])
        cur = t_abs >> 6
        forced = (lane == 0) | (lane == cur) | (lane == cur - 1)
        score = jnp.where(forced, FORCE_SCORE, imp)
        score = jnp.where(lane <= cur, score, -1.0)
        rank = jnp.zeros((TA, LANES), F32)
        for mp in range(n_tiles * TA // SEL_LEN):
            col = score[:, mp:mp + 1]
            beats = (col > score) | ((col == score) & (lane > mp))
            rank = rank + jnp.where(beats, 1.0, 0.0)
        sel = jnp.where(rank < SEL_TOP, 1.0, 0.0).astype(BF16)

        def key_pen(kj):
            k0 = pl.multiple_of(kj * TA, TA)
            on = _mm(sel, eexp_ref[:, pl.ds(k0, TA)])
            return (on - 1.0) * (-NEG_INF)

        _reset(m_ref, l_ref, acc_ref)
        pen = key_pen(qi)
        s = _nt(q4, sk_ref[pl.ds(r0, TA), :]) + stack(lambda j: toe_ref[j, 0:TA, :] + pen)
        _online_update(s, sv_ref[pl.ds(r0, TA), :], m_ref, l_ref, acc_ref)
        kprev = jnp.maximum(qi - 1, 0)
        p0 = pl.multiple_of(kprev * TA, TA)
        off = jnp.where(qi == 0, NEG_INF, 0.0)
        pen = key_pen(kprev) + off
        s = _nt(q4, sk_ref[pl.ds(p0, TA), :]) + stack(lambda j: toe_ref[j, TA:2 * TA, :] + pen)
        _online_update(s, sv_ref[pl.ds(p0, TA), :], m_ref, l_ref, acc_ref)

        def far_tile(kj, c):
            k0 = pl.multiple_of(kj * TA, TA)
            pen = key_pen(kj)
            s = _nt(q4, sk_ref[pl.ds(k0, TA), :]) + stack(lambda j: pen + far[j])
            _online_update(s, sv_ref[pl.ds(k0, TA), :], m_ref, l_ref, acc_ref)
            return c

        lax.fori_loop(0, qi - 1, far_tile, 0)
        o_slc = finish()

        _reset(m_ref, l_ref, acc_ref)
        s = _nt(q4, wk_ref[pl.ds(r0, TA), :]) + stack(lambda j: toe_ref[j, 0:TA, :])
        _online_update(s, wv_ref[pl.ds(r0, TA), :], m_ref, l_ref, acc_ref)
        s = _nt(q4, wk_ref[pl.ds(p0, TA), :]) + stack(lambda j: toe_ref[j, 2 * TA:3 * TA, :] + off)
        _online_update(s, wv_ref[pl.ds(p0, TA), :], m_ref, l_ref, acc_ref)
        o_win = finish()

        gl = misc_ref[pl.ds(r0, TA), :].astype(F32)
        gates = _mm((1.0 / (1.0 + jnp.exp(-gl))).astype(BF16), egate_ref[0])
        out = jnp.zeros((TA, W), F32)
        for j in range(J):
            rs = slice(j * TA, (j + 1) * TA)
            mix = (gates[:, 0:W] * o_cmp[rs] + gates[:, W:2 * W] * o_slc[rs] + gates[:, 2 * W:3 * W] * o_win[rs])
            out = jnp.where(head_of_lane == j, mix, out)
        o_ref[pl.ds(r0, TA), :] = out.astype(BF16)
        return carry

    lax.fori_loop(0, n_tiles, q_tile, 0)


def _nsa_attention(proj, kcmp, table, bias_cmp, toe, erep, ovl, eexp, egate, B, S):
    J = NSA_HPG
    W = J * NSA_DH
    q_blk = NSA_Q_COL // W
    kv_blk = NSA_KV_COL // W
    misc_blk = MISC_COL // LANES
    return pl.pallas_call(
        functools.partial(_nsa_kernel, n_tiles=S // TA),
        grid=(B, NSA_GROUPS),
        in_specs=[
            pl.BlockSpec(memory_space=pltpu.SMEM),
            pl.BlockSpec((S, W), lambda b, g: (b, q_blk + g)),
            pl.BlockSpec((S, W), lambda b, g: (b, kv_blk + g)),
            pl.BlockSpec((S, LANES), lambda b, g: (b, misc_blk)),
            pl.BlockSpec((1, 1, N_CMP_PAD, W), lambda b, g: (g, b, 0, 0)),
            pl.BlockSpec((1, 1, N_CMP_PAD, W), lambda b, g: (2 + g, b, 0, 0)),
            pl.BlockSpec((J, S, N_CMP_PAD), lambda b, g: (g, 0, 0)),
            pl.BlockSpec((J, 3 * TA, TA), lambda b, g: (g, 0, 0)),
            pl.BlockSpec((4, W, W), lambda b, g: (0, 0, 0)),
            pl.BlockSpec((N_CMP_PAD, LANES), lambda b, g: (0, 0)),
            pl.BlockSpec((LANES, S), lambda b, g: (0, 0)),
            pl.BlockSpec((1, LANES, 3 * W), lambda b, g: (g, 0, 0)),
        ],
        out_specs=pl.BlockSpec((S, W), lambda b, g: (b, g)),
        out_shape=jax.ShapeDtypeStruct((B * S, NSA_HEADS * NSA_DH), BF16),
        scratch_shapes=[pltpu.VMEM((S, W), BF16)] * 4 + [
            pltpu.VMEM((J * TA, 1), F32), pltpu.VMEM((J * TA, 1), F32), pltpu.VMEM((J * TA, W), F32)],
        compiler_params=_cparams(("parallel", "parallel")),
        name="nsa_attention",
    )(table, proj, proj, proj, kcmp, kcmp, bias_cmp, toe, erep, ovl, eexp, egate)


def _diff_kernel(tbl_ref, q_ref, k_ref, v_ref, toe_ref, lam_ref, sub_ref, o_ref, m_ref, l_ref, acc_ref,
                 *, n_tiles, lam_init):
    h = pl.program_id(1)
    lane = lax.broadcasted_iota(jnp.int32, (TA, 2 * DIFF_DH), 1)
    far = tbl_ref[FAR_BUCKET, h]
    lv = lam_ref[...]
    lam = (jnp.exp(jnp.sum(lv[0:1] * lv[1:2], axis=-1, keepdims=True))
           - jnp.exp(jnp.sum(lv[2:3] * lv[3:4], axis=-1, keepdims=True)) + lam_init)

    def two(t):
        return jnp.concatenate([t, t], axis=0)

    def q_tile(qi, carry):
        r0 = pl.multiple_of(qi * TA, TA)
        q = q_ref[pl.ds(r0, TA), :]
        zero = jnp.zeros_like(q)
        q2 = jnp.concatenate([jnp.where(lane < DIFF_DH, q, zero), jnp.where(lane >= DIFF_DH, q, zero)], axis=0)
        _reset(m_ref, l_ref, acc_ref)
        s = _nt(q2, k_ref[pl.ds(r0, TA), :]) + two(toe_ref[0, 0:TA, :])
        _online_update(s, v_ref[pl.ds(r0, TA), :], m_ref, l_ref, acc_ref)
        kprev = jnp.maximum(qi - 1, 0)
        p0 = pl.multiple_of(kprev * TA, TA)
        off = jnp.where(qi == 0, NEG_INF, 0.0)
        s = _nt(q2, k_ref[pl.ds(p0, TA), :]) + two(toe_ref[0, TA:2 * TA, :] + off)
        _online_update(s, v_ref[pl.ds(p0, TA), :], m_ref, l_ref, acc_ref)

        def far_tile(kj, c):
            k0 = pl.multiple_of(kj * TA, TA)
            s = _nt(q2, k_ref[pl.ds(k0, TA), :]) + far
            _online_update(s, v_ref[pl.ds(k0, TA), :], m_ref, l_ref, acc_ref)
            return c

        lax.fori_loop(0, qi - 1, far_tile, 0)
        o2 = acc_ref[...] * (1.0 / l_ref[...])
        o = o2[0:TA] - lam * o2[TA:2 * TA]
        o = o * lax.rsqrt(jnp.mean(o * o, axis=-1, keepdims=True) + RMS_EPS) * sub_ref[...]
        o_ref[pl.ds(r0, TA), :] = (o * (1.0 - lam_init)).astype(BF16)
        return carry

    lax.fori_loop(0, n_tiles, q_tile, 0)


def _diff_attention(proj, table, toe, lam_vec, subln, layer_idx, B, S):
    H = DIFF_HEADS
    hw = 2 * DIFF_DH
    lam_init = 0.8 - 0.6 * math.exp(-0.3 * layer_idx)
    return pl.pallas_call(
        functools.partial(_diff_kernel, n_tiles=S // TA, lam_init=lam_init),
        grid=(B, H),
        in_specs=[
            pl.BlockSpec(memory_space=pltpu.SMEM),
            pl.BlockSpec((S, hw), lambda b, h: (b, h)),
            pl.BlockSpec((S, hw), lambda b, h: (b, H + h)),
            pl.BlockSpec((S, hw), lambda b, h: (b, 2 * H + h)),
            pl.BlockSpec((1, 3 * TA, TA), lambda b, h: (h, 0, 0)),
            pl.BlockSpec((4, DIFF_DH), lambda b, h: (0, 0)),
            pl.BlockSpec((1, hw), lambda b, h: (0, 0)),
        ],
        out_specs=pl.BlockSpec((S, hw), lambda b, h: (b, h)),
        out_shape=jax.ShapeDtypeStruct((B * S, H * hw), BF16),
        scratch_shapes=[pltpu.VMEM((2 * TA, 1), F32), pltpu.VMEM((2 * TA, 1), F32), pltpu.VMEM((2 * TA, hw), F32)],
        compiler_params=_cparams(("parallel", "parallel")),
        name="diff_attention",
    )(table, proj, proj, proj, toe, lam_vec, subln.reshape(1, hw))


def _outproj_kernel(oa_ref, ob_ref, w_ref, x_ref, g_ref, lg_ref, lb_ref, o_ref):
    half = oa_ref.shape[1]
    y = _mm(oa_ref[...], w_ref[:half, :]) + _mm(ob_ref[...], w_ref[half:, :])
    z = ALPHA * x_ref[...] + (1.0 + g_ref[0]) * y
    o_ref[...] = _layer_norm_rows(z, lg_ref[...], lb_ref[...])


def _outproj_ln(o_a, o_b, blk_a, blk_b, w, x, gate, ln_g, ln_b, S):
    T, D = x.shape
    tm = 512
    half = w.shape[0] // 2
    per_seq = S // tm
    return pl.pallas_call(
        _outproj_kernel,
        grid=(T // tm,),
        in_specs=[
            pl.BlockSpec((tm, half), lambda i: (i, blk_a)),
            pl.BlockSpec((tm, half), lambda i: (i, blk_b)),
            pl.BlockSpec((2 * half, D), lambda i: (0, 0)),
            pl.BlockSpec((tm, D), lambda i: (i, 0)),
            pl.BlockSpec((1, 1, D), lambda i: (i // per_seq, 0, 0)),
            pl.BlockSpec((1, D), lambda i: (0, 0)),
            pl.BlockSpec((1, D), lambda i: (0, 0)),
        ],
        out_specs=pl.BlockSpec((tm, D), lambda i: (i, 0)),
        out_shape=jax.ShapeDtypeStruct((T, D), F32),
        compiler_params=_cparams(("parallel",)),
        name="outproj_ln",
    )(o_a, o_b, w, x, gate, ln_g, ln_b)


def _ffn_kernel(x_ref, xh_ref, sc_ref, sh_ref, g_ref, wu_ref, wg_ref, cw_ref, cb_ref, wd_ref, lg_ref, lb_ref,
                o_ref, h_ref, hh_ref, acc_ref, *, per_seq, n_ff):
    i = pl.program_id(0)
    f = pl.program_id(1)

    @pl.when(f == 0)
    def _():
        scale = 1.0 + sc_ref[0]
        shift = sh_ref[0]
        h_ref[...] = (x_ref[...] * scale + shift).astype(BF16)
        hh_ref[...] = (xh_ref[...] * scale + shift).astype(BF16)
        acc_ref[...] = jnp.zeros(acc_ref.shape, F32)

    h = h_ref[...]
    u = _mm(h, wu_ref[...])
    gt = _mm(h, wg_ref[...])
    gh = _mm(hh_ref[...], wg_ref[...]) * jnp.where(i % per_seq == 0, 0.0, 1.0)
    row = lax.broadcasted_iota(jnp.int32, gt.shape, 0)
    prev1 = jnp.where(row == 0, gh[7:8], pltpu.roll(gt, 1, 0))
    prev2 = jnp.where(row == 0, gh[6:7], jnp.where(row == 1, gh[7:8], pltpu.roll(gt, 2, 0)))
    cw = cw_ref[...]
    a = cw[2:3] * gt + cw[1:2] * prev1 + cw[0:1] * prev2 + cb_ref[...]
    act = (a / (1.0 + jnp.exp(-a)) * u).astype(BF16)
    acc_ref[...] += _mm(act, wd_ref[...])

    @pl.when(f == n_ff - 1)
    def _():
        z = ALPHA * x_ref[...] + (1.0 + g_ref[0]) * acc_ref[...]
        o_ref[...] = _layer_norm_rows(z, lg_ref[...], lb_ref[...])


def _conv_ffn_ln(x, scale, shift, gate, w_up, w_gate, conv_w, conv_b, w_down, ln_g, ln_b, S):
    T, D = x.shape
    tm, tf = 1024, 256
    per_seq = S // tm
    n_ff = D_FF // tf
    halo = 8
    mod_spec = pl.BlockSpec((1, 1, D), lambda i, f: (i // per_seq, 0, 0))
    return pl.pallas_call(
        functools.partial(_ffn_kernel, per_seq=per_seq, n_ff=n_ff),
        grid=(T // tm, n_ff),
        in_specs=[
            pl.BlockSpec((tm, D), lambda i, f: (i, 0)),
            pl.BlockSpec((halo, D), lambda i, f: (jnp.maximum(i * (tm // halo) - 1, 0), 0)),
            mod_spec, mod_spec, mod_spec,
            pl.BlockSpec((D, tf), lambda i, f: (0, f)),
            pl.BlockSpec((D, tf), lambda i, f: (0, f)),
            pl.BlockSpec((3, tf), lambda i, f: (0, f)),
            pl.BlockSpec((1, tf), lambda i, f: (0, f)),
            pl.BlockSpec((tf, D), lambda i, f: (f, 0)),
            pl.BlockSpec((1, D), lambda i, f: (0, 0)),
            pl.BlockSpec((1, D), lambda i, f: (0, 0)),
        ],
        out_specs=pl.BlockSpec((tm, D), lambda i, f: (i, 0)),
        out_shape=jax.ShapeDtypeStruct((T, D), F32),
        scratch_shapes=[pltpu.VMEM((tm, D), BF16), pltpu.VMEM((halo, D), BF16), pltpu.VMEM((tm, D), F32)],
        compiler_params=_cparams(("parallel", "arbitrary")),
        name="conv_ffn_ln",
    )(x, x, scale, shift, gate, w_up, w_gate, conv_w, conv_b, w_down, ln_g, ln_b)


def _even_in_columns():
    src = np.zeros(EVEN_W, np.int32)
    mul = np.zeros(EVEN_W, np.float32)

    def put(dst, cols, scale=1.0):
        cols = np.asarray(cols)
        src[dst:dst + len(cols)] = cols
        mul[dst:dst + len(cols)] = scale

    put(0, np.arange(MLA_Q_RANK + MLA_KV_RANK))
    rope0 = MLA_Q_RANK + MLA_KV_RANK
    half = MLA_ROPE // 2
    put(MISC_COL, rope0 + np.arange(MLA_ROPE))
    put(MISC_COL + MLA_ROPE, rope0 + half + np.arange(half), -1.0)
    put(MISC_COL + MLA_ROPE + half, rope0 + np.arange(half), 1.0)
    nsa0 = MLA_COLS
    put(MISC_COL + 2 * MLA_ROPE, nsa0 + NSA_Q_COLS + NSA_KV_COLS + np.arange(NSA_GATE_COLS))
    put(NSA_Q_COL, nsa0 + np.arange(NSA_Q_COLS), NSA_DH ** -0.5)
    kv0 = nsa0 + NSA_Q_COLS

    def chunk(branch, kv, g):
        return kv0 + ((branch * 2 + kv) * NSA_GROUPS + g) * NSA_DH + np.arange(NSA_DH)

    for g in range(NSA_GROUPS):
        base = NSA_KV_COL + g * 4 * NSA_DH
        for slot, (branch, kv) in enumerate([(1, 0), (2, 0), (1, 1), (2, 1)]):
            put(base + slot * NSA_DH, chunk(branch, kv, g))
    put(NSA_CMP_COL, kv0 + np.arange(2 * NSA_GROUPS * NSA_DH))
    return src, mul


def _mla_up_columns():
    hw = MLA_HEADS * LANES
    qd = MLA_NOPE + MLA_ROPE
    half = MLA_ROPE // 2
    src = np.zeros(2 * hw, np.int32)
    mul = np.zeros(2 * hw, np.float32)
    for h in range(MLA_HEADS):
        src[h * LANES:h * LANES + qd] = h * qd + np.arange(qd)
        mul[h * LANES:h * LANES + qd] = 1.0
        r = hw + h * LANES + MLA_NOPE
        src[r:r + half] = h * qd + MLA_NOPE + half + np.arange(half)
        mul[r:r + half] = -1.0
        src[r + half:r + MLA_ROPE] = h * qd + MLA_NOPE + np.arange(half)
        mul[r + half:r + MLA_ROPE] = 1.0
    ksrc = np.zeros(hw, np.int32)
    kmul = np.zeros(hw, np.float32)
    vsrc = np.zeros(MLA_HEADS * MLA_V, np.int32)
    for h in range(MLA_HEADS):
        ksrc[h * LANES:h * LANES + MLA_NOPE] = h * (MLA_NOPE + MLA_V) + np.arange(MLA_NOPE)
        kmul[h * LANES:h * LANES + MLA_NOPE] = 1.0
        vsrc[h * MLA_V:(h + 1) * MLA_V] = h * (MLA_NOPE + MLA_V) + MLA_NOPE + np.arange(MLA_V)
    return src, mul, ksrc, kmul, vsrc


def _routing_constants(S):
    hw = MLA_HEADS * LANES
    half = MLA_ROPE // 2
    erope = np.zeros((LANES, hw), np.float32)
    for h in range(MLA_HEADS):
        for i in range(MLA_ROPE):
            erope[i, h * LANES + MLA_NOPE + i] = 1.0
            erope[MLA_ROPE + i, h * LANES + MLA_NOPE + i] = 1.0
    W = NSA_HPG * NSA_DH
    erep = np.zeros((4, W, W), np.float32)
    for slot in range(4):
        for d in range(NSA_DH):
            for j in range(NSA_HPG):
                erep[slot, slot * NSA_DH + d, j * NSA_DH + d] = 1.0
    n_slc = S // SEL_LEN
    starts = np.arange(N_CMP_PAD) * CMP_STRIDE
    jb = np.arange(n_slc)
    ovl = np.zeros((N_CMP_PAD, LANES), np.float32)
    ovl[:, :n_slc] = ((starts[:, None] < (jb[None, :] + 1) * SEL_LEN)
                      & (starts[:, None] + CMP_LEN > jb[None, :] * SEL_LEN))
    ovl[(S - CMP_LEN) // CMP_STRIDE + 1:, :] = 0.0
    eexp = np.zeros((LANES, S), np.float32)
    eexp[np.arange(S) // SEL_LEN, np.arange(S)] = 1.0
    egate = np.zeros((NSA_GROUPS, LANES, 3 * W), np.float32)
    for g in range(NSA_GROUPS):
        for j in range(NSA_HPG):
            for br in range(3):
                src_lane = 2 * MLA_ROPE + (g * NSA_HPG + j) * 3 + br
                egate[g, src_lane, br * W + j * NSA_DH:br * W + (j + 1) * NSA_DH] = 1.0
    cmask = np.where(np.arange(TA)[:, None] >= np.arange(TA)[None, :], 0.0, NEG_INF).astype(np.float32)
    return erope, erep, ovl, eexp, egate, cmask


def _rope_tables(S):
    inv = 1.0 / (ROPE_BASE ** (jnp.arange(0, MLA_ROPE, 2, dtype=jnp.float32) / MLA_ROPE))
    ang = jnp.arange(S, dtype=jnp.float32)[:, None] * inv[None, :]
    cos, sin = jnp.cos(ang), jnp.sin(ang)
    cos2 = jnp.concatenate([cos, cos], axis=-1)
    sin2 = jnp.concatenate([sin, sin], axis=-1)
    scale = (MLA_NOPE + MLA_ROPE) ** -0.5
    z32 = jnp.zeros((S, LANES - MLA_NOPE - MLA_ROPE), F32)
    ctab = jnp.concatenate([jnp.full((S, MLA_NOPE), scale, F32), cos2 * scale, z32], axis=-1)
    stab = jnp.concatenate([jnp.zeros((S, MLA_NOPE), F32), sin2 * scale, z32], axis=-1)
    mrope = jnp.concatenate([cos2, sin2, jnp.zeros((S, LANES - 2 * MLA_ROPE), F32)], axis=-1)
    return ctab, stab, mrope


def kernel(x, c, rel_bias, ev_w_in, mla_q_norm, mla_kv_norm, mla_w_uq, mla_w_ukv, nsa_cmp_pe, nsa_cmp_w1,
           nsa_cmp_w2, ev_w_o, od_w_in, diff_lambda, diff_subln, od_w_o, ada_w, ada_b, ln_g, ln_b, ffn_w_up,
           ffn_w_gate, ffn_conv_w, ffn_conv_b, ffn_w_down):
    B, S, D = x.shape
    assert D == D_MODEL and S % TA == 0 and S // CMP_STRIDE == N_CMP_PAD and S // SEL_LEN <= LANES
    far_np = np.arange(TA + 1, max(S, TA + 2))
    far_bucket = 16 + (np.log(far_np.astype(np.float32) / 16) / math.log(MAX_DISTANCE / 16) * 16).astype(np.int32)
    assert far_bucket.min() >= FAR_BUCKET
    T = B * S
    xf = x.reshape(T, D)

    mod = _ada_mod(c, ada_w, ada_b)
    toe_idx, cmp_idx = _bias_index_tiles(S)
    toe = _expand_bias(rel_bias, toe_idx)
    bias_cmp = _expand_bias(rel_bias, cmp_idx)
    erope, erep, ovl, eexp, egate, cmask = _routing_constants(S)
    ctab, stab, mrope = _rope_tables(S)
    ev_src, ev_mul = _even_in_columns()
    q_src, q_mul, k_src, k_mul, v_src = _mla_up_columns()

    def bf(a):
        return jnp.asarray(a).astype(BF16)

    for l in range(DEPTH):
        sh1, sc1, g1, sh2, sc2, g2 = [mod[l, :, k * D:(k + 1) * D].reshape(B, 1, D) for k in range(6)]
        i = l // 2
        if l % 2 == 0:
            w_in = bf(ev_w_in[i][:, ev_src] * ev_mul)
            proj = _mod_matmul(xf, sc1, sh1, w_in, S)
            wq = bf(mla_w_uq[i][:, q_src] * q_mul)
            wk = bf(mla_w_ukv[i][:, k_src] * k_mul)
            wv = bf(mla_w_ukv[i][:, v_src])
            q_m, k_m, v_m = _mla_prep(proj, mla_q_norm[i].reshape(1, -1), mla_kv_norm[i].reshape(1, -1),
                                      wq, wk, wv, bf(erope), ctab, stab, mrope, S)
            o_a = _mla_attention(q_m, k_m, v_m, jnp.asarray(cmask), B, S)
            r = proj[:, NSA_CMP_COL:].reshape(B, S, 4, NSA_DH).transpose(2, 0, 1, 3)
            r = r.reshape(4, B, N_CMP_PAD, CMP_STRIDE * NSA_DH)
            w2_rep = jnp.tile(nsa_cmp_w2[i], (1, 1, NSA_HPG))
            kcmp = _compress(r, nsa_cmp_pe[i].reshape(2, 1, CMP_LEN * NSA_DH), bf(nsa_cmp_w1[i]), bf(w2_rep), B)
            o_b = _nsa_attention(proj, kcmp, rel_bias, bias_cmp, toe, bf(erep), bf(ovl), bf(eexp), bf(egate), B, S)
            xf = _outproj_ln(o_a, o_b, 0, 0, bf(ev_w_o[i]), xf, g1, ln_g[l, 0:1], ln_b[l, 0:1], S)
        else:
            w_in = od_w_in[i].at[:, :DIFF_HEADS * 2 * DIFF_DH].multiply(DIFF_DH ** -0.5)
            proj = _mod_matmul(xf, sc1, sh1, bf(w_in), S)
            o_d = _diff_attention(proj, rel_bias, toe, diff_lambda[i], diff_subln[i], l, B, S)
            xf = _outproj_ln(o_d, o_d, 0, 1, bf(od_w_o[i]), xf, g1, ln_g[l, 0:1], ln_b[l, 0:1], S)
        xf = _conv_ffn_ln(xf, sc2, sh2, g2, bf(ffn_w_up[l]), bf(ffn_w_gate[l]), ffn_conv_w[l],
                          ffn_conv_b[l].reshape(1, D_FF), bf(ffn_w_down[l]), ln_g[l, 1:2], ln_b[l, 1:2], S)
    return xf.reshape(B, S, D)
```

```python
import functools
import math

import numpy as np
import jax
import jax.numpy as jnp
from jax import lax
from jax.experimental import pallas as pl
from jax.experimental.pallas import tpu as pltpu

F32 = jnp.float32
BF16 = jnp.bfloat16

D_MODEL = 1024
DEPTH = 4
N_BUCKETS = 32
MAX_DISTANCE = 128
NEG_INF = -1e30

MLA_HEADS = 8
MLA_Q_RANK = 384
MLA_KV_RANK = 256
MLA_NOPE = 64
MLA_ROPE = 32
MLA_V = 64
ROPE_BASE = 10000.0
MLA_COLS = MLA_Q_RANK + MLA_KV_RANK + MLA_ROPE

NSA_HEADS = 8
NSA_GROUPS = 2
NSA_HPG = NSA_HEADS // NSA_GROUPS
NSA_DH = 64
CMP_LEN = 32
CMP_STRIDE = 16
CMP_HIDDEN = 256
SEL_LEN = 64
SEL_TOP = 8
FORCE_SCORE = 1e4
WINDOW = 256
NSA_Q_COLS = NSA_HEADS * NSA_DH
NSA_KV_COLS = 3 * 2 * NSA_GROUPS * NSA_DH
NSA_GATE_COLS = 3 * NSA_HEADS

DIFF_HEADS = 8
DIFF_DH = 64
DIFF_IN = DIFF_HEADS * 6 * DIFF_DH

D_FF = 2816
ALPHA = (2.0 * DEPTH) ** 0.25
LN_EPS = 1e-5
RMS_EPS = 1e-6

LANES = 128
VMEM_LIMIT = 56 * 1024 * 1024

TA = 256
EVEN_W = 2048
MISC_COL = 640
NSA_Q_COL = 768
NSA_KV_COL = 1280
NSA_CMP_COL = 1792
N_CMP_PAD = 128
MASK_BUCKET = N_BUCKETS


def _cparams(sem):
    return pltpu.CompilerParams(dimension_semantics=sem, vmem_limit_bytes=VMEM_LIMIT)


def _nt(a, b):
    return lax.dot_general(a, b, (((1,), (1,)), ((), ())), preferred_element_type=F32)


def _mm(a, b):
    return jnp.dot(a, b, preferred_element_type=F32)


def _split3(v):
    hi = v.astype(BF16)
    r1 = v - hi.astype(F32)
    mid = r1.astype(BF16)
    lo = (r1 - mid.astype(F32)).astype(BF16)
    return hi, mid, lo


def _layer_norm_rows(z, g, b):
    mu = jnp.mean(z, axis=-1, keepdims=True)
    zc = z - mu
    var = jnp.mean(zc * zc, axis=-1, keepdims=True)
    return zc * lax.rsqrt(var + LN_EPS) * g + b


def _online_update(s, v, m_ref, l_ref, acc_ref):
    m_prev = m_ref[...]
    m_new = jnp.maximum(m_prev, jnp.max(s, axis=-1, keepdims=True))
    alpha = jnp.exp(m_prev - m_new)
    p = jnp.exp(s - m_new)
    l_ref[...] = alpha * l_ref[...] + jnp.sum(p, axis=-1, keepdims=True)
    acc_ref[...] = alpha * acc_ref[...] + _mm(p.astype(BF16), v)
    m_ref[...] = m_new


def _softmax_pv_t(chunks, score_chunk, v_t_chunk, s_ref):
    w = s_ref.shape[1]
    m8 = None
    for i, c in enumerate(chunks):
        s = score_chunk(c)
        s_ref[i * TA:(i + 1) * TA, :] = s
        mc = jnp.max(s.reshape(TA // 8, 8, w), axis=0)
        m8 = mc if m8 is None else jnp.maximum(m8, mc)
    m = jnp.max(m8, axis=0, keepdims=True)
    l8 = None
    acc = None
    for i, c in enumerate(chunks):
        p = jnp.exp(s_ref[i * TA:(i + 1) * TA, :] - m)
        lc = jnp.sum(p.reshape(TA // 8, 8, w), axis=0)
        l8 = lc if l8 is None else l8 + lc
        pv = _mm(v_t_chunk(c), p.astype(BF16))
        acc = pv if acc is None else acc + pv
    return acc, jnp.sum(l8, axis=0, keepdims=True)


def _reset(m_ref, l_ref, acc_ref):
    m_ref[...] = jnp.full(m_ref.shape, NEG_INF, F32)
    l_ref[...] = jnp.zeros(l_ref.shape, F32)
    acc_ref[...] = jnp.zeros(acc_ref.shape, F32)


def _ada_kernel(c_ref, w_ref, b_ref, o_ref):
    c = c_ref[...]
    ca = c / (1.0 + jnp.exp(-c))
    w = w_ref[0]
    c_hi = ca.astype(BF16)
    c_lo = (ca - c_hi.astype(F32)).astype(BF16)
    w_hi = w.astype(BF16)
    w_lo = (w - w_hi.astype(F32)).astype(BF16)
    o_ref[0] = _mm(c_hi, w_hi) + _mm(c_hi, w_lo) + _mm(c_lo, w_hi) + b_ref[0]


def _ada_mod(c, ada_w, ada_b):
    B, D = c.shape
    n_out = ada_w.shape[-1]
    tn = 1024
    return pl.pallas_call(
        _ada_kernel,
        grid=(DEPTH, n_out // tn),
        in_specs=[
            pl.BlockSpec((B, D), lambda l, j: (0, 0)),
            pl.BlockSpec((1, D, tn), lambda l, j: (l, 0, j)),
            pl.BlockSpec((1, 1, tn), lambda l, j: (l, 0, j)),
        ],
        out_specs=pl.BlockSpec((1, B, tn), lambda l, j: (l, 0, j)),
        out_shape=jax.ShapeDtypeStruct((DEPTH, B, n_out), F32),
        compiler_params=_cparams(("parallel", "parallel")),
        name="ada_mod",
    )(c, ada_w, ada_b.reshape(DEPTH, 1, n_out))


def _expand_kernel(tbl_ref, idx_ref, o_ref, *, rel_far):
    h = pl.program_id(0)
    idx = idx_ref[...]
    acc = jnp.full(idx.shape, NEG_INF, F32)
    base = tbl_ref[FAR_BUCKET, h] if rel_far else 0.0
    for b in range(N_BUCKETS):
        acc = jnp.where(idx == b, tbl_ref[b, h] - base, acc)
    o_ref[0] = acc


def _expand_bias(table, idx, rel_far=False):
    R, C = idx.shape
    rt = min(R, 256)
    return pl.pallas_call(
        functools.partial(_expand_kernel, rel_far=rel_far),
        grid=(table.shape[1], R // rt),
        in_specs=[
            pl.BlockSpec(memory_space=pltpu.SMEM),
            pl.BlockSpec((rt, C), lambda h, r: (r, 0)),
        ],
        out_specs=pl.BlockSpec((1, rt, C), lambda h, r: (h, r, 0)),
        out_shape=jax.ShapeDtypeStruct((table.shape[1], R, C), F32),
        compiler_params=_cparams(("parallel", "parallel")),
        name="expand_bias",
    )(table, idx)


def _t5_bucket(dist):
    n = jnp.maximum(dist, 0)
    max_exact = N_BUCKETS // 2
    nf = jnp.maximum(n, 1).astype(jnp.float32)
    large = max_exact + (jnp.log(nf / max_exact) / math.log(MAX_DISTANCE / max_exact)
                         * (N_BUCKETS - max_exact)).astype(jnp.int32)
    large = jnp.minimum(large, N_BUCKETS - 1)
    return jnp.where(n < max_exact, n, large)


FAR_BUCKET = N_BUCKETS - 1


def _bias_index_tiles(S):
    a = np.arange(TA)[:, None]
    b = np.arange(TA)[None, :]
    d0 = a - b
    d1 = TA + a - b
    t0 = jnp.where(jnp.asarray(d0 >= 0), _t5_bucket(jnp.asarray(d0)), MASK_BUCKET)
    t1 = _t5_bucket(jnp.asarray(d1))
    t1w = jnp.where(jnp.asarray(d1 < WINDOW), t1, MASK_BUCKET)
    toe = jnp.concatenate([t0, t1, t1w], axis=0).astype(jnp.int32)
    t_pos = np.arange(S)[:, None]
    n = np.arange(N_CMP_PAD)[None, :]
    d_cmp = t_pos - (n * CMP_STRIDE + CMP_LEN - 1)
    cmp_idx = jnp.where(jnp.asarray(d_cmp >= 0), _t5_bucket(jnp.asarray(d_cmp)), MASK_BUCKET).astype(jnp.int32)
    toe_t = jnp.concatenate([t0.T, t1.T, t1w.T], axis=0).astype(jnp.int32)
    return toe, cmp_idx, toe_t, cmp_idx.T


def _modmm_kernel(x_ref, sc_ref, sh_ref, w_ref, o_ref, h_ref):
    @pl.when(pl.program_id(1) == 0)
    def _():
        h_ref[...] = (x_ref[...] * (1.0 + sc_ref[0]) + sh_ref[0]).astype(BF16)

    o_ref[...] = _mm(h_ref[...], w_ref[...]).astype(BF16)


def _mod_matmul(x, scale, shift, w, S):
    T, D = x.shape
    N = w.shape[1]
    tm, tn = 1024, 512
    per_seq = S // tm
    return pl.pallas_call(
        _modmm_kernel,
        grid=(T // tm, N // tn),
        in_specs=[
            pl.BlockSpec((tm, D), lambda i, j: (i, 0)),
            pl.BlockSpec((1, 1, D), lambda i, j: (i // per_seq, 0, 0)),
            pl.BlockSpec((1, 1, D), lambda i, j: (i // per_seq, 0, 0)),
            pl.BlockSpec((D, tn), lambda i, j: (0, j)),
        ],
        out_specs=pl.BlockSpec((tm, tn), lambda i, j: (i, j)),
        out_shape=jax.ShapeDtypeStruct((T, N), BF16),
        scratch_shapes=[pltpu.VMEM((tm, D), BF16)],
        compiler_params=_cparams(("parallel", "arbitrary")),
        name="mod_matmul",
    )(x, scale, shift, w)


def _mla_prep_kernel(p_ref, qn_ref, kvn_ref, wq_ref, wk_ref, wv_ref, er_ref, ct_ref, st_ref, mr_ref,
                     q_ref, k_ref, v_ref):
    p = p_ref[...]
    ql = p[:, :MLA_Q_RANK].astype(F32)
    kl = p[:, MLA_Q_RANK:MLA_Q_RANK + MLA_KV_RANK].astype(F32)
    misc = p[:, MISC_COL:MISC_COL + LANES].astype(F32)
    c_q = (ql * lax.rsqrt(jnp.mean(ql * ql, axis=-1, keepdims=True) + RMS_EPS) * qn_ref[...]).astype(BF16)
    c_kv = (kl * lax.rsqrt(jnp.mean(kl * kl, axis=-1, keepdims=True) + RMS_EPS) * kvn_ref[...]).astype(BF16)
    ab = _mm(c_q, wq_ref[...])
    ct = ct_ref[...]
    st = st_ref[...]
    half = MLA_HEADS * LANES
    for h in range(MLA_HEADS):
        lo = h * LANES
        q_ref[:, lo:lo + LANES] = (ab[:, lo:lo + LANES] * ct + ab[:, half + lo:half + lo + LANES] * st).astype(BF16)
    k_rope = _mm((misc * mr_ref[...]).astype(BF16), er_ref[...])
    k_ref[...] = (_mm(c_kv, wk_ref[...]) + k_rope).astype(BF16)
    v_ref[...] = _mm(c_kv, wv_ref[...]).astype(BF16)


def _mla_prep(proj, qn, kvn, wq, wk, wv, erope, ctab, stab, mrope, S):
    T = proj.shape[0]
    tm = 512
    per_seq = S // tm
    hw = MLA_HEADS * LANES
    const = lambda i: (0, 0)
    seq = lambda i: (i % per_seq, 0)
    return pl.pallas_call(
        _mla_prep_kernel,
        grid=(T // tm,),
        in_specs=[
            pl.BlockSpec((tm, NSA_Q_COL), lambda i: (i, 0)),
            pl.BlockSpec((1, MLA_Q_RANK), const),
            pl.BlockSpec((1, MLA_KV_RANK), const),
            pl.BlockSpec((MLA_Q_RANK, 2 * hw), const),
            pl.BlockSpec((MLA_KV_RANK, hw), const),
            pl.BlockSpec((MLA_KV_RANK, MLA_HEADS * MLA_V), const),
            pl.BlockSpec((LANES, hw), const),
            pl.BlockSpec((tm, LANES), seq),
            pl.BlockSpec((tm, LANES), seq),
            pl.BlockSpec((tm, LANES), seq),
        ],
        out_specs=[
            pl.BlockSpec((tm, hw), lambda i: (i, 0)),
            pl.BlockSpec((tm, hw), lambda i: (i, 0)),
            pl.BlockSpec((tm, MLA_HEADS * MLA_V), lambda i: (i, 0)),
        ],
        out_shape=[
            jax.ShapeDtypeStruct((T, hw), BF16),
            jax.ShapeDtypeStruct((T, hw), BF16),
            jax.ShapeDtypeStruct((T, MLA_HEADS * MLA_V), BF16),
        ],
        compiler_params=_cparams(("parallel",)),
        name="mla_prep",
    )(proj, qn, kvn, wq, wk, wv, erope, ctab, stab, mrope)


def _mla_attn_kernel(q_ref, k_ref, v_ref, cm_ref, id_ref, o_ref, vt_ref, s_ref, *, n_tiles):
    vt_ref[...] = _nt(id_ref[...], v_ref[...]).astype(BF16)
    for qi in range(n_tiles):
        rows = slice(qi * TA, (qi + 1) * TA)
        outs = []
        for hh in range(2):
            cs = slice(hh * LANES, (hh + 1) * LANES)
            q = q_ref[rows, cs]

            def score(c, q=q, cs=cs, qi=qi):
                s = _nt(k_ref[c * TA:(c + 1) * TA, cs], q)
                return s + cm_ref[...] if c == qi else s

            def v_t(c, hh=hh):
                return vt_ref[hh * MLA_V:(hh + 1) * MLA_V, c * TA:(c + 1) * TA]

            acc, l = _softmax_pv_t(range(qi + 1), score, v_t, s_ref)
            outs.append(acc * (1.0 / l))
        o_ref[rows, :] = jnp.concatenate(outs, axis=0).T.astype(BF16)


def _mla_attention(q, k, v, cmask_t, ident, B, S):
    pairs = MLA_HEADS // 2
    return pl.pallas_call(
        functools.partial(_mla_attn_kernel, n_tiles=S // TA),
        grid=(B, pairs),
        in_specs=[
            pl.BlockSpec((S, 2 * LANES), lambda b, hp: (b, hp)),
            pl.BlockSpec((S, 2 * LANES), lambda b, hp: (b, hp)),
            pl.BlockSpec((S, LANES), lambda b, hp: (b, hp)),
            pl.BlockSpec((TA, TA), lambda b, hp: (0, 0)),
            pl.BlockSpec((LANES, LANES), lambda b, hp: (0, 0)),
        ],
        out_specs=pl.BlockSpec((S, LANES), lambda b, hp: (b, hp)),
        out_shape=jax.ShapeDtypeStruct((B * S, MLA_HEADS * MLA_V), BF16),
        scratch_shapes=[pltpu.VMEM((LANES, S), BF16), pltpu.VMEM((S, TA), F32)],
        compiler_params=_cparams(("parallel", "parallel")),
        name="mla_attention",
    )(q, k, v, cmask_t, ident)


def _compress_kernel(r_ref, pe_ref, w1_ref, w2_ref, w2t_ref, o_ref, ot_ref):
    half = CMP_STRIDE * NSA_DH
    r = r_ref[0, 0].astype(F32)
    pe = pe_ref[0]
    x_lo = (r + pe[:, :half]).astype(BF16)
    x_hi = (r + pe[:, half:]).astype(BF16)
    y_lo = _mm(x_lo, w1_ref[0, :half, :])
    y_hi = _mm(x_hi, w1_ref[0, half:, :])
    y = y_lo + pltpu.roll(y_hi, N_CMP_PAD - 1, 0)
    hid = jax.nn.gelu(y).astype(BF16)
    o_ref[0, 0] = _mm(hid, w2_ref[0]).astype(BF16)
    ot_ref[0, 0] = _nt(w2t_ref[0], hid).astype(BF16)


def _compress(r, pe, w1, w2_rep, w2_t, B):
    rep = NSA_HPG * NSA_DH
    return pl.pallas_call(
        _compress_kernel,
        grid=(4, B),
        in_specs=[
            pl.BlockSpec((1, 1, N_CMP_PAD, CMP_STRIDE * NSA_DH), lambda c, b: (c, b, 0, 0)),
            pl.BlockSpec((1, 1, CMP_LEN * NSA_DH), lambda c, b: (c // 2, 0, 0)),
            pl.BlockSpec((1, CMP_LEN * NSA_DH, CMP_HIDDEN), lambda c, b: (c // 2, 0, 0)),
            pl.BlockSpec((1, CMP_HIDDEN, rep), lambda c, b: (c // 2, 0, 0)),
            pl.BlockSpec((1, NSA_DH, CMP_HIDDEN), lambda c, b: (c // 2, 0, 0)),
        ],
        out_specs=[
            pl.BlockSpec((1, 1, N_CMP_PAD, rep), lambda c, b: (c, b, 0, 0)),
            pl.BlockSpec((1, 1, NSA_DH, N_CMP_PAD), lambda c, b: (c, b, 0, 0)),
        ],
        out_shape=[
            jax.ShapeDtypeStruct((4, B, N_CMP_PAD, rep), BF16),
            jax.ShapeDtypeStruct((4, B, NSA_DH, N_CMP_PAD), BF16),
        ],
        compiler_params=_cparams(("parallel", "parallel")),
        name="nsa_compress",
    )(r, pe, w1, w2_rep, w2_t)


def _nsa_kernel(tbl_ref, q_ref, kv_ref, misc_ref, kc_ref, vc_ref, bc_ref, toe_ref, erep_ref, ovl_ref,
                eexp_ref, egate_ref, o_ref,
                sk_ref, wk_ref, sv_ref, wv_ref, m_ref, l_ref, acc_ref, *, n_tiles):
    g = pl.program_id(1)
    J = NSA_HPG
    W = J * NSA_DH
    kv = kv_ref[...]
    sk_ref[...] = _mm(kv, erep_ref[0]).astype(BF16)
    wk_ref[...] = _mm(kv, erep_ref[1]).astype(BF16)
    sv_ref[...] = _mm(kv, erep_ref[2]).astype(BF16)
    wv_ref[...] = _mm(kv, erep_ref[3]).astype(BF16)

    lane_w = lax.broadcasted_iota(jnp.int32, (TA, W), 1)
    head_of_lane = lane_w >> 6
    row = lax.broadcasted_iota(jnp.int32, (TA, LANES), 0)
    lane = lax.broadcasted_iota(jnp.int32, (TA, LANES), 1)
    far = [tbl_ref[FAR_BUCKET, g * J + j] for j in range(J)]

    def stack(fn):
        return jnp.concatenate([fn(j) for j in range(J)], axis=0)

    def finish():
        return acc_ref[...] * (1.0 / l_ref[...])

    def q_tile(qi, carry):
        r0 = pl.multiple_of(qi * TA, TA)
        q = q_ref[pl.ds(r0, TA), :]
        q4 = stack(lambda j: jnp.where(head_of_lane == j, q, jnp.zeros_like(q)))
        t_abs = row + qi * TA

        s = _nt(q4, kc_ref[0, 0]) + stack(lambda j: bc_ref[j, pl.ds(r0, TA), :])
        m = jnp.max(s, axis=-1, keepdims=True)
        e = jnp.exp(s - m)
        p = e * (1.0 / jnp.sum(e, axis=-1, keepdims=True))
        p = jnp.where(jnp.concatenate([t_abs] * J, axis=0) >= CMP_LEN - 1, p, 0.0)
        o_cmp = _mm(p.astype(BF16), vc_ref[0, 0])
        p_sum = p[0:TA] + p[TA:2 * TA] + p[2 * TA:3 * TA] + p[3 * TA:4 * TA]

        hi, mid, lo = _split3(p_sum)
        imp = _mm(hi, ovl_ref[...]) + _mm(mid, ovl_ref[...]) + _mm(lo, ovl_ref[...])
        cur = t_abs >> 6
        forced = (lane == 0) | (lane == cur) | (lane == cur - 1)
        score = jnp.where(forced, FORCE_SCORE, imp)
        score = jnp.where(lane <= cur, score, -1.0)
        rank = jnp.zeros((TA, LANES), F32)
        for mp in range(n_tiles * TA // SEL_LEN):
            col = score[:, mp:mp + 1]
            beats = (col > score) | ((col == score) & (lane > mp))
            rank = rank + jnp.where(beats, 1.0, 0.0)
        sel = jnp.where(rank < SEL_TOP, 1.0, 0.0).astype(BF16)

        def key_pen(kj):
            k0 = pl.multiple_of(kj * TA, TA)
            on = _mm(sel, eexp_ref[:, pl.ds(k0, TA)])
            return (on - 1.0) * (-NEG_INF)

        _reset(m_ref, l_ref, acc_ref)
        pen = key_pen(qi)
        s = _nt(q4, sk_ref[pl.ds(r0, TA), :]) + stack(lambda j: toe_ref[j, 0:TA, :] + pen)
        _online_update(s, sv_ref[pl.ds(r0, TA), :], m_ref, l_ref, acc_ref)
        kprev = jnp.maximum(qi - 1, 0)
        p0 = pl.multiple_of(kprev * TA, TA)
        off = jnp.where(qi == 0, NEG_INF, 0.0)
        pen = key_pen(kprev) + off
        s = _nt(q4, sk_ref[pl.ds(p0, TA), :]) + stack(lambda j: toe_ref[j, TA:2 * TA, :] + pen)
        _online_update(s, sv_ref[pl.ds(p0, TA), :], m_ref, l_ref, acc_ref)

        def far_tile(kj, c):
            k0 = pl.multiple_of(kj * TA, TA)
            pen = key_pen(kj)
            s = _nt(q4, sk_ref[pl.ds(k0, TA), :]) + stack(lambda j: pen + far[j])
            _online_update(s, sv_ref[pl.ds(k0, TA), :], m_ref, l_ref, acc_ref)
            return c

        lax.fori_loop(0, qi - 1, far_tile, 0)
        o_slc = finish()

        _reset(m_ref, l_ref, acc_ref)
        s = _nt(q4, wk_ref[pl.ds(r0, TA), :]) + stack(lambda j: toe_ref[j, 0:TA, :])
        _online_update(s, wv_ref[pl.ds(r0, TA), :], m_ref, l_ref, acc_ref)
        s = _nt(q4, wk_ref[pl.ds(p0, TA), :]) + stack(lambda j: toe_ref[j, 2 * TA:3 * TA, :] + off)
        _online_update(s, wv_ref[pl.ds(p0, TA), :], m_ref, l_ref, acc_ref)
        o_win = finish()

        gl = misc_ref[pl.ds(r0, TA), :].astype(F32)
        gates = _mm((1.0 / (1.0 + jnp.exp(-gl))).astype(BF16), egate_ref[0])
        out = jnp.zeros((TA, W), F32)
        for j in range(J):
            rs = slice(j * TA, (j + 1) * TA)
            mix = (gates[:, 0:W] * o_cmp[rs] + gates[:, W:2 * W] * o_slc[rs] + gates[:, 2 * W:3 * W] * o_win[rs])
            out = jnp.where(head_of_lane == j, mix, out)
        o_ref[pl.ds(r0, TA), :] = out.astype(BF16)
        return carry

    lax.fori_loop(0, n_tiles, q_tile, 0)


def _nsa_attention(proj, kcmp, table, bias_cmp, toe, erep, ovl, eexp, egate, B, S):
    J = NSA_HPG
    W = J * NSA_DH
    q_blk = NSA_Q_COL // W
    kv_blk = NSA_KV_COL // W
    misc_blk = MISC_COL // LANES
    return pl.pallas_call(
        functools.partial(_nsa_kernel, n_tiles=S // TA),
        grid=(B, NSA_GROUPS),
        in_specs=[
            pl.BlockSpec(memory_space=pltpu.SMEM),
            pl.BlockSpec((S, W), lambda b, g: (b, q_blk + g)),
            pl.BlockSpec((S, W), lambda b, g: (b, kv_blk + g)),
            pl.BlockSpec((S, LANES), lambda b, g: (b, misc_blk)),
            pl.BlockSpec((1, 1, N_CMP_PAD, W), lambda b, g: (g, b, 0, 0)),
            pl.BlockSpec((1, 1, N_CMP_PAD, W), lambda b, g: (2 + g, b, 0, 0)),
            pl.BlockSpec((J, S, N_CMP_PAD), lambda b, g: (g, 0, 0)),
            pl.BlockSpec((J, 3 * TA, TA), lambda b, g: (g, 0, 0)),
            pl.BlockSpec((4, W, W), lambda b, g: (0, 0, 0)),
            pl.BlockSpec((N_CMP_PAD, LANES), lambda b, g: (0, 0)),
            pl.BlockSpec((LANES, S), lambda b, g: (0, 0)),
            pl.BlockSpec((1, LANES, 3 * W), lambda b, g: (g, 0, 0)),
        ],
        out_specs=pl.BlockSpec((S, W), lambda b, g: (b, g)),
        out_shape=jax.ShapeDtypeStruct((B * S, NSA_HEADS * NSA_DH), BF16),
        scratch_shapes=[pltpu.VMEM((S, W), BF16)] * 4 + [
            pltpu.VMEM((J * TA, 1), F32), pltpu.VMEM((J * TA, 1), F32), pltpu.VMEM((J * TA, W), F32)],
        compiler_params=_cparams(("parallel", "parallel")),
        name="nsa_attention",
    )(table, proj, proj, proj, kcmp, kcmp, bias_cmp, toe, erep, ovl, eexp, egate)


def _nsa_t_kernel(q_ref, kv_ref, misc_ref, kc_ref, vct_ref, bc_ref, toe_ref, erep_ref, esel_ref, ovl_ref, o_ref,
                  sk_ref, wk_ref, svt_ref, wvt_ref, s_ref, *, n_tiles):
    g = pl.program_id(1)
    J = NSA_HPG
    W = J * NSA_DH
    QW = J * TA
    n_blk = n_tiles * TA // SEL_LEN
    blk_per_chunk = TA // SEL_LEN
    kv = kv_ref[...]
    sk_ref[...] = _mm(kv, erep_ref[0]).astype(BF16)
    wk_ref[...] = _mm(kv, erep_ref[1]).astype(BF16)
    svt_ref[...] = _nt(esel_ref[0], kv).astype(BF16)
    wvt_ref[...] = _nt(esel_ref[1], kv).astype(BF16)

    head_of_lane = lax.broadcasted_iota(jnp.int32, (TA, W), 1) >> 6
    blk = lax.broadcasted_iota(jnp.int32, (n_blk, TA), 0)
    qpos = lax.broadcasted_iota(jnp.int32, (n_blk, TA), 1)
    g_is0 = g == 0

    for qi in range(n_tiles):
        rows = slice(qi * TA, (qi + 1) * TA)
        q = q_ref[rows, :]
        zero = jnp.zeros_like(q)
        q4 = jnp.concatenate([jnp.where(head_of_lane == j, q, zero) for j in range(J)], axis=0)

        s = _nt(kc_ref[0, 0], q4) + bc_ref[0, qi]
        e = jnp.exp(s - jnp.max(s, axis=0, keepdims=True))
        p = e * (1.0 / jnp.sum(e, axis=0, keepdims=True))
        if qi * TA < CMP_LEN - 1:
            tq = (lax.broadcasted_iota(jnp.int32, p.shape, 1) & (TA - 1)) + qi * TA
            p = jnp.where(tq >= CMP_LEN - 1, p, 0.0)
        o_cmp = _mm(vct_ref[0, 0], p.astype(BF16))
        p_sum = p[:, 0:TA]
        for j in range(1, J):
            p_sum = p_sum + p[:, j * TA:(j + 1) * TA]

        hi, mid, lo = _split3(p_sum)
        imp = _mm(ovl_ref[...], hi) + _mm(ovl_ref[...], mid) + _mm(ovl_ref[...], lo)
        cur = (qpos + qi * TA) >> 6
        forced = (blk == 0) | (blk == cur) | (blk == cur - 1)
        score = jnp.where(forced, FORCE_SCORE, imp)
        score = jnp.where(blk <= cur, score, -1.0)
        rank = jnp.zeros((n_blk, TA), F32)
        for mp in range(n_blk):
            r = score[mp:mp + 1, :]
            beats = (r > score) | ((r == score) & (blk > mp))
            rank = rank + jnp.where(beats, 1.0, 0.0)
        pen = jnp.where(rank < SEL_TOP, 0.0, NEG_INF)
        pen4 = jnp.concatenate([pen] * J, axis=1)

        def pen_rows(c, pen4=pen4):
            parts = [jnp.broadcast_to(pen4[c * blk_per_chunk + i:c * blk_per_chunk + i + 1, :], (SEL_LEN, QW))
                     for i in range(blk_per_chunk)]
            return jnp.concatenate(parts, axis=0)

        def score_sel(c, q4=q4, qi=qi, pen_rows=pen_rows):
            s = _nt(sk_ref[c * TA:(c + 1) * TA, :], q4) + pen_rows(c)
            if c == qi:
                return s + toe_ref[0, 0]
            if c == qi - 1:
                return s + toe_ref[0, 1]
            return s

        acc, l = _softmax_pv_t(range(qi + 1), score_sel, lambda c: svt_ref[:, c * TA:(c + 1) * TA], s_ref)
        o_slc = acc * (1.0 / l)

        def score_win(c, q4=q4, qi=qi):
            s = _nt(wk_ref[c * TA:(c + 1) * TA, :], q4)
            return s + (toe_ref[0, 0] if c == qi else toe_ref[0, 2])

        acc, l = _softmax_pv_t(range(max(qi - 1, 0), qi + 1), score_win,
                               lambda c: wvt_ref[:, c * TA:(c + 1) * TA], s_ref)
        o_win = acc * (1.0 / l)

        gl = misc_ref[rows, :].astype(F32)
        sig_t = (1.0 / (1.0 + jnp.exp(-gl))).T
        mixes = []
        for j in range(J):
            def gate(br, j=j):
                r0 = 2 * MLA_ROPE + j * 3 + br
                r1 = r0 + J * 3
                return jnp.where(g_is0, sig_t[r0:r0 + 1, :], sig_t[r1:r1 + 1, :])

            cs = slice(j * TA, (j + 1) * TA)
            mixes.append(gate(0) * o_cmp[:, cs] + gate(1) * o_slc[:, cs] + gate(2) * o_win[:, cs])
        o_ref[rows, :] = jnp.concatenate(mixes, axis=0).T.astype(BF16)


def _nsa_attention_t(proj, kcmp, vcmp_t, bias_cmp_t, toe_g, erep, esel, ovl_t, B, S):
    J = NSA_HPG
    W = J * NSA_DH
    n_tiles = S // TA
    n_blk = S // SEL_LEN
    q_blk = NSA_Q_COL // W
    kv_blk = NSA_KV_COL // W
    misc_blk = MISC_COL // LANES
    return pl.pallas_call(
        functools.partial(_nsa_t_kernel, n_tiles=n_tiles),
        grid=(B, NSA_GROUPS),
        in_specs=[
            pl.BlockSpec((S, W), lambda b, g: (b, q_blk + g)),
            pl.BlockSpec((S, W), lambda b, g: (b, kv_blk + g)),
            pl.BlockSpec((S, LANES), lambda b, g: (b, misc_blk)),
            pl.BlockSpec((1, 1, N_CMP_PAD, W), lambda b, g: (g, b, 0, 0)),
            pl.BlockSpec((1, 1, NSA_DH, N_CMP_PAD), lambda b, g: (2 + g, b, 0, 0)),
            pl.BlockSpec((1, n_tiles, N_CMP_PAD, J * TA), lambda b, g: (g, 0, 0, 0)),
            pl.BlockSpec((1, 3, TA, J * TA), lambda b, g: (g, 0, 0, 0)),
            pl.BlockSpec((2, W, W), lambda b, g: (0, 0, 0)),
            pl.BlockSpec((2, NSA_DH, W), lambda b, g: (0, 0, 0)),
            pl.BlockSpec((n_blk, N_CMP_PAD), lambda b, g: (0, 0)),
        ],
        out_specs=pl.BlockSpec((S, W), lambda b, g: (b, g)),
        out_shape=jax.ShapeDtypeStruct((B * S, NSA_HEADS * NSA_DH), BF16),
        scratch_shapes=[pltpu.VMEM((S, W), BF16), pltpu.VMEM((S, W), BF16),
                        pltpu.VMEM((NSA_DH, S), BF16), pltpu.VMEM((NSA_DH, S), BF16),
                        pltpu.VMEM((S, J * TA), F32)],
        compiler_params=_cparams(("parallel", "parallel")),
        name="nsa_attention",
    )(proj, proj, proj, kcmp, vcmp_t, bias_cmp_t, toe_g, erep, esel, ovl_t)


def _diff_kernel(q_ref, k_ref, v_ref, toe_ref, lam_ref, sub_ref, id_ref, o_ref, vt_ref, s_ref, *, n_tiles, lam_init):
    hd = DIFF_DH
    lane = lax.broadcasted_iota(jnp.int32, (TA, 2 * hd), 1)
    lv = lam_ref[...]
    lam = (jnp.exp(jnp.sum(lv[0:1] * lv[1:2], axis=-1, keepdims=True))
           - jnp.exp(jnp.sum(lv[2:3] * lv[3:4], axis=-1, keepdims=True)) + lam_init)
    vt_ref[...] = _nt(id_ref[...], v_ref[...]).astype(BF16)

    def both(t):
        return jnp.concatenate([t, t], axis=1)

    for qi in range(n_tiles):
        rows = slice(qi * TA, (qi + 1) * TA)
        q = q_ref[rows, :]
        zero = jnp.zeros_like(q)
        q2 = jnp.concatenate([jnp.where(lane < hd, q, zero), jnp.where(lane >= hd, q, zero)], axis=0)

        def score(c, q2=q2, qi=qi):
            s = _nt(k_ref[c * TA:(c + 1) * TA, :], q2)
            if c == qi:
                return s + both(toe_ref[0, 0:TA, :])
            if c == qi - 1:
                return s + both(toe_ref[0, TA:2 * TA, :])
            return s

        acc, l = _softmax_pv_t(range(qi + 1), score, lambda c: vt_ref[:, c * TA:(c + 1) * TA], s_ref)
        o2 = acc * (1.0 / l)
        o = (o2[:, 0:TA] - lam * o2[:, TA:2 * TA]).T
        o = o * lax.rsqrt(jnp.mean(o * o, axis=-1, keepdims=True) + RMS_EPS) * sub_ref[...]
        o_ref[rows, :] = (o * (1.0 - lam_init)).astype(BF16)


def _diff_attention(proj, toe_t, lam_vec, subln, ident, layer_idx, B, S):
    H = DIFF_HEADS
    hw = 2 * DIFF_DH
    lam_init = 0.8 - 0.6 * math.exp(-0.3 * layer_idx)
    return pl.pallas_call(
        functools.partial(_diff_kernel, n_tiles=S // TA, lam_init=lam_init),
        grid=(B, H),
        in_specs=[
            pl.BlockSpec((S, hw), lambda b, h: (b, h)),
            pl.BlockSpec((S, hw), lambda b, h: (b, H + h)),
            pl.BlockSpec((S, hw), lambda b, h: (b, 2 * H + h)),
            pl.BlockSpec((1, 3 * TA, TA), lambda b, h: (h, 0, 0)),
            pl.BlockSpec((4, DIFF_DH), lambda b, h: (0, 0)),
            pl.BlockSpec((1, hw), lambda b, h: (0, 0)),
            pl.BlockSpec((hw, hw), lambda b, h: (0, 0)),
        ],
        out_specs=pl.BlockSpec((S, hw), lambda b, h: (b, h)),
        out_shape=jax.ShapeDtypeStruct((B * S, H * hw), BF16),
        scratch_shapes=[pltpu.VMEM((hw, S), BF16), pltpu.VMEM((S, 2 * TA), F32)],
        compiler_params=_cparams(("parallel", "parallel")),
        name="diff_attention",
    )(proj, proj, proj, toe_t, lam_vec, subln.reshape(1, hw), ident)


def _outproj_kernel(oa_ref, ob_ref, w_ref, x_ref, g_ref, lg_ref, lb_ref, o_ref):
    half = oa_ref.shape[1]
    y = _mm(oa_ref[...], w_ref[:half, :]) + _mm(ob_ref[...], w_ref[half:, :])
    z = ALPHA * x_ref[...] + (1.0 + g_ref[0]) * y
    o_ref[...] = _layer_norm_rows(z, lg_ref[...], lb_ref[...])


def _outproj_ln(o_a, o_b, blk_a, blk_b, w, x, gate, ln_g, ln_b, S):
    T, D = x.shape
    tm = 512
    half = w.shape[0] // 2
    per_seq = S // tm
    return pl.pallas_call(
        _outproj_kernel,
        grid=(T // tm,),
        in_specs=[
            pl.BlockSpec((tm, half), lambda i: (i, blk_a)),
            pl.BlockSpec((tm, half), lambda i: (i, blk_b)),
            pl.BlockSpec((2 * half, D), lambda i: (0, 0)),
            pl.BlockSpec((tm, D), lambda i: (i, 0)),
            pl.BlockSpec((1, 1, D), lambda i: (i // per_seq, 0, 0)),
            pl.BlockSpec((1, D), lambda i: (0, 0)),
            pl.BlockSpec((1, D), lambda i: (0, 0)),
        ],
        out_specs=pl.BlockSpec((tm, D), lambda i: (i, 0)),
        out_shape=jax.ShapeDtypeStruct((T, D), F32),
        compiler_params=_cparams(("parallel",)),
        name="outproj_ln",
    )(o_a, o_b, w, x, gate, ln_g, ln_b)


def _ffn_kernel(x_ref, xh_ref, sc_ref, sh_ref, g_ref, wu_ref, wg_ref, cw_ref, cb_ref, wd_ref, lg_ref, lb_ref,
                o_ref, h_ref, hh_ref, acc_ref, *, per_seq, n_ff):
    i = pl.program_id(0)
    f = pl.program_id(1)

    @pl.when(f == 0)
    def _():
        scale = 1.0 + sc_ref[0]
        shift = sh_ref[0]
        h_ref[...] = (x_ref[...] * scale + shift).astype(BF16)
        hh_ref[...] = (xh_ref[...] * scale + shift).astype(BF16)
        acc_ref[...] = jnp.zeros(acc_ref.shape, F32)

    h = h_ref[...]
    u = _mm(h, wu_ref[...])
    gt = _mm(h, wg_ref[...])
    gh = _mm(hh_ref[...], wg_ref[...]) * jnp.where(i % per_seq == 0, 0.0, 1.0)
    row = lax.broadcasted_iota(jnp.int32, gt.shape, 0)
    prev1 = jnp.where(row == 0, gh[7:8], pltpu.roll(gt, 1, 0))
    prev2 = jnp.where(row == 0, gh[6:7], jnp.where(row == 1, gh[7:8], pltpu.roll(gt, 2, 0)))
    cw = cw_ref[...]
    a = cw[2:3] * gt + cw[1:2] * prev1 + cw[0:1] * prev2 + cb_ref[...]
    act = (a / (1.0 + jnp.exp(-a)) * u).astype(BF16)
    acc_ref[...] += _mm(act, wd_ref[...])

    @pl.when(f == n_ff - 1)
    def _():
        z = ALPHA * x_ref[...] + (1.0 + g_ref[0]) * acc_ref[...]
        o_ref[...] = _layer_norm_rows(z, lg_ref[...], lb_ref[...])


def _conv_ffn_ln(x, scale, shift, gate, w_up, w_gate, conv_w, conv_b, w_down, ln_g, ln_b, S):
    T, D = x.shape
    tm, tf = 1024, 256
    per_seq = S // tm
    n_ff = D_FF // tf
    halo = 8
    mod_spec = pl.BlockSpec((1, 1, D), lambda i, f: (i // per_seq, 0, 0))
    return pl.pallas_call(
        functools.partial(_ffn_kernel, per_seq=per_seq, n_ff=n_ff),
        grid=(T // tm, n_ff),
        in_specs=[
            pl.BlockSpec((tm, D), lambda i, f: (i, 0)),
            pl.BlockSpec((halo, D), lambda i, f: (jnp.maximum(i * (tm // halo) - 1, 0), 0)),
            mod_spec, mod_spec, mod_spec,
            pl.BlockSpec((D, tf), lambda i, f: (0, f)),
            pl.BlockSpec((D, tf), lambda i, f: (0, f)),
            pl.BlockSpec((3, tf), lambda i, f: (0, f)),
            pl.BlockSpec((1, tf), lambda i, f: (0, f)),
            pl.BlockSpec((tf, D), lambda i, f: (f, 0)),
            pl.BlockSpec((1, D), lambda i, f: (0, 0)),
            pl.BlockSpec((1, D), lambda i, f: (0, 0)),
        ],
        out_specs=pl.BlockSpec((tm, D), lambda i, f: (i, 0)),
        out_shape=jax.ShapeDtypeStruct((T, D), F32),
        scratch_shapes=[pltpu.VMEM((tm, D), BF16), pltpu.VMEM((halo, D), BF16), pltpu.VMEM((tm, D), F32)],
        compiler_params=_cparams(("parallel", "arbitrary")),
        name="conv_ffn_ln",
    )(x, x, scale, shift, gate, w_up, w_gate, conv_w, conv_b, w_down, ln_g, ln_b)


def _even_in_columns():
    src = np.zeros(EVEN_W, np.int32)
    mul = np.zeros(EVEN_W, np.float32)

    def put(dst, cols, scale=1.0):
        cols = np.asarray(cols)
        src[dst:dst + len(cols)] = cols
        mul[dst:dst + len(cols)] = scale

    put(0, np.arange(MLA_Q_RANK + MLA_KV_RANK))
    rope0 = MLA_Q_RANK + MLA_KV_RANK
    half = MLA_ROPE // 2
    put(MISC_COL, rope0 + np.arange(MLA_ROPE))
    put(MISC_COL + MLA_ROPE, rope0 + half + np.arange(half), -1.0)
    put(MISC_COL + MLA_ROPE + half, rope0 + np.arange(half), 1.0)
    nsa0 = MLA_COLS
    put(MISC_COL + 2 * MLA_ROPE, nsa0 + NSA_Q_COLS + NSA_KV_COLS + np.arange(NSA_GATE_COLS))
    put(NSA_Q_COL, nsa0 + np.arange(NSA_Q_COLS), NSA_DH ** -0.5)
    kv0 = nsa0 + NSA_Q_COLS

    def chunk(branch, kv, g):
        return kv0 + ((branch * 2 + kv) * NSA_GROUPS + g) * NSA_DH + np.arange(NSA_DH)

    for g in range(NSA_GROUPS):
        base = NSA_KV_COL + g * 4 * NSA_DH
        for slot, (branch, kv) in enumerate([(1, 0), (2, 0), (1, 1), (2, 1)]):
            put(base + slot * NSA_DH, chunk(branch, kv, g))
    put(NSA_CMP_COL, kv0 + np.arange(2 * NSA_GROUPS * NSA_DH))
    return src, mul


def _mla_up_columns():
    hw = MLA_HEADS * LANES
    qd = MLA_NOPE + MLA_ROPE
    half = MLA_ROPE // 2
    src = np.zeros(2 * hw, np.int32)
    mul = np.zeros(2 * hw, np.float32)
    for h in range(MLA_HEADS):
        src[h * LANES:h * LANES + qd] = h * qd + np.arange(qd)
        mul[h * LANES:h * LANES + qd] = 1.0
        r = hw + h * LANES + MLA_NOPE
        src[r:r + half] = h * qd + MLA_NOPE + half + np.arange(half)
        mul[r:r + half] = -1.0
        src[r + half:r + MLA_ROPE] = h * qd + MLA_NOPE + np.arange(half)
        mul[r + half:r + MLA_ROPE] = 1.0
    ksrc = np.zeros(hw, np.int32)
    kmul = np.zeros(hw, np.float32)
    vsrc = np.zeros(MLA_HEADS * MLA_V, np.int32)
    for h in range(MLA_HEADS):
        ksrc[h * LANES:h * LANES + MLA_NOPE] = h * (MLA_NOPE + MLA_V) + np.arange(MLA_NOPE)
        kmul[h * LANES:h * LANES + MLA_NOPE] = 1.0
        vsrc[h * MLA_V:(h + 1) * MLA_V] = h * (MLA_NOPE + MLA_V) + MLA_NOPE + np.arange(MLA_V)
    return src, mul, ksrc, kmul, vsrc


def _routing_constants(S):
    hw = MLA_HEADS * LANES
    half = MLA_ROPE // 2
    erope = np.zeros((LANES, hw), np.float32)
    for h in range(MLA_HEADS):
        for i in range(MLA_ROPE):
            erope[i, h * LANES + MLA_NOPE + i] = 1.0
            erope[MLA_ROPE + i, h * LANES + MLA_NOPE + i] = 1.0
    W = NSA_HPG * NSA_DH
    erep = np.zeros((4, W, W), np.float32)
    for slot in range(4):
        for d in range(NSA_DH):
            for j in range(NSA_HPG):
                erep[slot, slot * NSA_DH + d, j * NSA_DH + d] = 1.0
    n_slc = S // SEL_LEN
    starts = np.arange(N_CMP_PAD) * CMP_STRIDE
    jb = np.arange(n_slc)
    ovl = np.zeros((N_CMP_PAD, LANES), np.float32)
    ovl[:, :n_slc] = ((starts[:, None] < (jb[None, :] + 1) * SEL_LEN)
                      & (starts[:, None] + CMP_LEN > jb[None, :] * SEL_LEN))
    ovl[(S - CMP_LEN) // CMP_STRIDE + 1:, :] = 0.0
    eexp = np.zeros((LANES, S), np.float32)
    eexp[np.arange(S) // SEL_LEN, np.arange(S)] = 1.0
    egate = np.zeros((NSA_GROUPS, LANES, 3 * W), np.float32)
    for g in range(NSA_GROUPS):
        for j in range(NSA_HPG):
            for br in range(3):
                src_lane = 2 * MLA_ROPE + (g * NSA_HPG + j) * 3 + br
                egate[g, src_lane, br * W + j * NSA_DH:br * W + (j + 1) * NSA_DH] = 1.0
    cmask = np.where(np.arange(TA)[:, None] >= np.arange(TA)[None, :], 0.0, NEG_INF).astype(np.float32)
    esel = np.zeros((2, NSA_DH, W), np.float32)
    for slot in range(2):
        esel[slot, np.arange(NSA_DH), (2 + slot) * NSA_DH + np.arange(NSA_DH)] = 1.0
    return erope, erep, ovl, eexp, egate, cmask, esel


def _rope_tables(S):
    inv = 1.0 / (ROPE_BASE ** (jnp.arange(0, MLA_ROPE, 2, dtype=jnp.float32) / MLA_ROPE))
    ang = jnp.arange(S, dtype=jnp.float32)[:, None] * inv[None, :]
    cos, sin = jnp.cos(ang), jnp.sin(ang)
    cos2 = jnp.concatenate([cos, cos], axis=-1)
    sin2 = jnp.concatenate([sin, sin], axis=-1)
    scale = (MLA_NOPE + MLA_ROPE) ** -0.5
    z32 = jnp.zeros((S, LANES - MLA_NOPE - MLA_ROPE), F32)
    ctab = jnp.concatenate([jnp.full((S, MLA_NOPE), scale, F32), cos2 * scale, z32], axis=-1)
    stab = jnp.concatenate([jnp.zeros((S, MLA_NOPE), F32), sin2 * scale, z32], axis=-1)
    mrope = jnp.concatenate([cos2, sin2, jnp.zeros((S, LANES - 2 * MLA_ROPE), F32)], axis=-1)
    return ctab, stab, mrope


def kernel(x, c, rel_bias, ev_w_in, mla_q_norm, mla_kv_norm, mla_w_uq, mla_w_ukv, nsa_cmp_pe, nsa_cmp_w1,
           nsa_cmp_w2, ev_w_o, od_w_in, diff_lambda, diff_subln, od_w_o, ada_w, ada_b, ln_g, ln_b, ffn_w_up,
           ffn_w_gate, ffn_conv_w, ffn_conv_b, ffn_w_down):
    B, S, D = x.shape
    assert D == D_MODEL and S % TA == 0 and S // CMP_STRIDE == N_CMP_PAD and S // SEL_LEN <= LANES
    far_np = np.arange(TA + 1, max(S, TA + 2))
    far_bucket = 16 + (np.log(far_np.astype(np.float32) / 16) / math.log(MAX_DISTANCE / 16) * 16).astype(np.int32)
    assert far_bucket.min() >= FAR_BUCKET
    T = B * S
    xf = x.reshape(T, D)

    mod = _ada_mod(c, ada_w, ada_b)
    toe_idx, cmp_idx, toe_t_idx, cmp_t_idx = _bias_index_tiles(S)
    toe_t = _expand_bias(rel_bias, toe_t_idx, rel_far=True)
    bias_cmp_t = _expand_bias(rel_bias, cmp_t_idx)
    G, J, NQ = NSA_GROUPS, NSA_HPG, S // TA
    toe_g = toe_t.reshape(G, J, 3, TA, TA).transpose(0, 2, 3, 1, 4).reshape(G, 3, TA, J * TA)
    bias_cmp_t = bias_cmp_t.reshape(G, J, N_CMP_PAD, NQ, TA).transpose(0, 3, 2, 1, 4).reshape(G, NQ, N_CMP_PAD, J * TA)
    erope, erep, ovl, eexp, egate, cmask, esel = _routing_constants(S)
    ovl_t = np.ascontiguousarray(ovl[:, :S // SEL_LEN].T)
    ident = jnp.eye(LANES, dtype=BF16)
    ctab, stab, mrope = _rope_tables(S)
    ev_src, ev_mul = _even_in_columns()
    q_src, q_mul, k_src, k_mul, v_src = _mla_up_columns()

    def bf(a):
        return jnp.asarray(a).astype(BF16)

    for l in range(DEPTH):
        sh1, sc1, g1, sh2, sc2, g2 = [mod[l, :, k * D:(k + 1) * D].reshape(B, 1, D) for k in range(6)]
        i = l // 2
        if l % 2 == 0:
            w_in = bf(ev_w_in[i][:, ev_src] * ev_mul)
            proj = _mod_matmul(xf, sc1, sh1, w_in, S)
            wq = bf(mla_w_uq[i][:, q_src] * q_mul)
            wk = bf(mla_w_ukv[i][:, k_src] * k_mul)
            wv = bf(mla_w_ukv[i][:, v_src])
            q_m, k_m, v_m = _mla_prep(proj, mla_q_norm[i].reshape(1, -1), mla_kv_norm[i].reshape(1, -1),
                                      wq, wk, wv, bf(erope), ctab, stab, mrope, S)
            o_a = _mla_attention(q_m, k_m, v_m, jnp.asarray(cmask.T), ident, B, S)
            r = proj[:, NSA_CMP_COL:].reshape(B, S, 4, NSA_DH).transpose(2, 0, 1, 3)
            r = r.reshape(4, B, N_CMP_PAD, CMP_STRIDE * NSA_DH)
            w2_rep = jnp.tile(nsa_cmp_w2[i], (1, 1, NSA_HPG))
            kcmp, vcmp_t = _compress(r, nsa_cmp_pe[i].reshape(2, 1, CMP_LEN * NSA_DH), bf(nsa_cmp_w1[i]), bf(w2_rep),
                                     bf(nsa_cmp_w2[i].transpose(0, 2, 1)), B)
            o_b = _nsa_attention_t(proj, kcmp, vcmp_t, bias_cmp_t, toe_g, bf(erep[:2]), bf(esel), bf(ovl_t), B, S)
            xf = _outproj_ln(o_a, o_b, 0, 0, bf(ev_w_o[i]), xf, g1, ln_g[l, 0:1], ln_b[l, 0:1], S)
        else:
            w_in = od_w_in[i].at[:, :DIFF_HEADS * 2 * DIFF_DH].multiply(DIFF_DH ** -0.5)
            proj = _mod_matmul(xf, sc1, sh1, bf(w_in), S)
            o_d = _diff_attention(proj, toe_t, diff_lambda[i], diff_subln[i], ident, l, B, S)
            xf = _outproj_ln(o_d, o_d, 0, 1, bf(od_w_o[i]), xf, g1, ln_g[l, 0:1], ln_b[l, 0:1], S)
        xf = _conv_ffn_ln(xf, sc2, sh2, g2, bf(ffn_w_up[l]), bf(ffn_w_gate[l]), ffn_conv_w[l],
                          ffn_conv_b[l].reshape(1, D_FF), bf(ffn_w_down[l]), ln_g[l, 1:2], ln_b[l, 1:2], S)
    return xf.reshape(B, S, D)
```

```python
import functools
import math

import numpy as np
import jax
import jax.numpy as jnp
from jax import lax
from jax.experimental import pallas as pl
from jax.experimental.pallas import tpu as pltpu

F32 = jnp.float32
BF16 = jnp.bfloat16

D_MODEL = 1024
DEPTH = 4
N_BUCKETS = 32
MAX_DISTANCE = 128
NEG_INF = -1e30

MLA_HEADS = 8
MLA_Q_RANK = 384
MLA_KV_RANK = 256
MLA_NOPE = 64
MLA_ROPE = 32
MLA_V = 64
ROPE_BASE = 10000.0
MLA_COLS = MLA_Q_RANK + MLA_KV_RANK + MLA_ROPE

NSA_HEADS = 8
NSA_GROUPS = 2
NSA_HPG = NSA_HEADS // NSA_GROUPS
NSA_DH = 64
CMP_LEN = 32
CMP_STRIDE = 16
CMP_HIDDEN = 256
SEL_LEN = 64
SEL_TOP = 8
FORCE_SCORE = 1e4
WINDOW = 256
NSA_Q_COLS = NSA_HEADS * NSA_DH
NSA_KV_COLS = 3 * 2 * NSA_GROUPS * NSA_DH
NSA_GATE_COLS = 3 * NSA_HEADS

DIFF_HEADS = 8
DIFF_DH = 64
DIFF_IN = DIFF_HEADS * 6 * DIFF_DH

D_FF = 2816
ALPHA = (2.0 * DEPTH) ** 0.25
LN_EPS = 1e-5
RMS_EPS = 1e-6

LANES = 128
VMEM_LIMIT = 56 * 1024 * 1024

TA = 256
EVEN_W = 2048
MISC_COL = 640
NSA_Q_COL = 768
NSA_KV_COL = 1280
NSA_CMP_COL = 1792
N_CMP_PAD = 128
MASK_BUCKET = N_BUCKETS


def _cparams(sem):
    return pltpu.CompilerParams(dimension_semantics=sem, vmem_limit_bytes=VMEM_LIMIT)


def _nt(a, b):
    return lax.dot_general(a, b, (((1,), (1,)), ((), ())), preferred_element_type=F32)


def _mm(a, b):
    return jnp.dot(a, b, preferred_element_type=F32)


def _split3(v):
    hi = v.astype(BF16)
    r1 = v - hi.astype(F32)
    mid = r1.astype(BF16)
    lo = (r1 - mid.astype(F32)).astype(BF16)
    return hi, mid, lo


def _layer_norm_rows(z, g, b):
    mu = jnp.mean(z, axis=-1, keepdims=True)
    zc = z - mu
    var = jnp.mean(zc * zc, axis=-1, keepdims=True)
    return zc * lax.rsqrt(var + LN_EPS) * g + b


def _online_update(s, v, m_ref, l_ref, acc_ref):
    m_prev = m_ref[...]
    m_new = jnp.maximum(m_prev, jnp.max(s, axis=-1, keepdims=True))
    alpha = jnp.exp(m_prev - m_new)
    p = jnp.exp(s - m_new)
    l_ref[...] = alpha * l_ref[...] + jnp.sum(p, axis=-1, keepdims=True)
    acc_ref[...] = alpha * acc_ref[...] + _mm(p.astype(BF16), v)
    m_ref[...] = m_new


def _softmax_pv_t(chunks, score_chunk, v_t_chunk, s_ref):
    w = s_ref.shape[1]
    m8 = None
    for i, c in enumerate(chunks):
        s = score_chunk(c)
        s_ref[i * TA:(i + 1) * TA, :] = s
        mc = jnp.max(s.reshape(TA // 8, 8, w), axis=0)
        m8 = mc if m8 is None else jnp.maximum(m8, mc)
    m = jnp.max(m8, axis=0, keepdims=True)
    l8 = None
    acc = None
    for i, c in enumerate(chunks):
        p = jnp.exp(s_ref[i * TA:(i + 1) * TA, :] - m)
        lc = jnp.sum(p.reshape(TA // 8, 8, w), axis=0)
        l8 = lc if l8 is None else l8 + lc
        pv = _mm(v_t_chunk(c), p.astype(BF16))
        acc = pv if acc is None else acc + pv
    return acc, jnp.sum(l8, axis=0, keepdims=True)


def _reset(m_ref, l_ref, acc_ref):
    m_ref[...] = jnp.full(m_ref.shape, NEG_INF, F32)
    l_ref[...] = jnp.zeros(l_ref.shape, F32)
    acc_ref[...] = jnp.zeros(acc_ref.shape, F32)


def _ada_kernel(c_ref, w_ref, b_ref, o_ref):
    c = c_ref[...]
    ca = c / (1.0 + jnp.exp(-c))
    w = w_ref[0]
    c_hi = ca.astype(BF16)
    c_lo = (ca - c_hi.astype(F32)).astype(BF16)
    w_hi = w.astype(BF16)
    w_lo = (w - w_hi.astype(F32)).astype(BF16)
    o_ref[0] = _mm(c_hi, w_hi) + _mm(c_hi, w_lo) + _mm(c_lo, w_hi) + b_ref[0]


def _ada_mod(c, ada_w, ada_b):
    B, D = c.shape
    n_out = ada_w.shape[-1]
    tn = 1024
    return pl.pallas_call(
        _ada_kernel,
        grid=(DEPTH, n_out // tn),
        in_specs=[
            pl.BlockSpec((B, D), lambda l, j: (0, 0)),
            pl.BlockSpec((1, D, tn), lambda l, j: (l, 0, j)),
            pl.BlockSpec((1, 1, tn), lambda l, j: (l, 0, j)),
        ],
        out_specs=pl.BlockSpec((1, B, tn), lambda l, j: (l, 0, j)),
        out_shape=jax.ShapeDtypeStruct((DEPTH, B, n_out), F32),
        compiler_params=_cparams(("parallel", "parallel")),
        name="ada_mod",
    )(c, ada_w, ada_b.reshape(DEPTH, 1, n_out))


def _expand_kernel(tbl_ref, idx_ref, o_ref, *, rel_far):
    h = pl.program_id(0)
    idx = idx_ref[...]
    acc = jnp.full(idx.shape, NEG_INF, F32)
    base = tbl_ref[FAR_BUCKET, h] if rel_far else 0.0
    for b in range(N_BUCKETS):
        acc = jnp.where(idx == b, tbl_ref[b, h] - base, acc)
    o_ref[0] = acc


def _expand_bias(table, idx, rel_far=False):
    R, C = idx.shape
    rt = min(R, 256)
    return pl.pallas_call(
        functools.partial(_expand_kernel, rel_far=rel_far),
        grid=(table.shape[1], R // rt),
        in_specs=[
            pl.BlockSpec(memory_space=pltpu.SMEM),
            pl.BlockSpec((rt, C), lambda h, r: (r, 0)),
        ],
        out_specs=pl.BlockSpec((1, rt, C), lambda h, r: (h, r, 0)),
        out_shape=jax.ShapeDtypeStruct((table.shape[1], R, C), F32),
        compiler_params=_cparams(("parallel", "parallel")),
        name="expand_bias",
    )(table, idx)


def _t5_bucket(dist):
    n = jnp.maximum(dist, 0)
    max_exact = N_BUCKETS // 2
    nf = jnp.maximum(n, 1).astype(jnp.float32)
    large = max_exact + (jnp.log(nf / max_exact) / math.log(MAX_DISTANCE / max_exact)
                         * (N_BUCKETS - max_exact)).astype(jnp.int32)
    large = jnp.minimum(large, N_BUCKETS - 1)
    return jnp.where(n < max_exact, n, large)


FAR_BUCKET = N_BUCKETS - 1


def _bias_index_tiles(S):
    a = np.arange(TA)[:, None]
    b = np.arange(TA)[None, :]
    d0 = a - b
    d1 = TA + a - b
    t0 = jnp.where(jnp.asarray(d0 >= 0), _t5_bucket(jnp.asarray(d0)), MASK_BUCKET)
    t1 = _t5_bucket(jnp.asarray(d1))
    t1w = jnp.where(jnp.asarray(d1 < WINDOW), t1, MASK_BUCKET)
    toe = jnp.concatenate([t0, t1, t1w], axis=0).astype(jnp.int32)
    t_pos = np.arange(S)[:, None]
    n = np.arange(N_CMP_PAD)[None, :]
    d_cmp = t_pos - (n * CMP_STRIDE + CMP_LEN - 1)
    cmp_idx = jnp.where(jnp.asarray(d_cmp >= 0), _t5_bucket(jnp.asarray(d_cmp)), MASK_BUCKET).astype(jnp.int32)
    toe_t = jnp.concatenate([t0.T, t1.T, t1w.T], axis=0).astype(jnp.int32)
    return toe, cmp_idx, toe_t, cmp_idx.T


def _modmm_kernel(x_ref, sc_ref, sh_ref, w_ref, o_ref, *, tn):
    h = (x_ref[...] * (1.0 + sc_ref[0]) + sh_ref[0]).astype(BF16)
    for j in range(w_ref.shape[1] // tn):
        cs = slice(j * tn, (j + 1) * tn)
        o_ref[:, cs] = _mm(h, w_ref[:, cs]).astype(BF16)


def _mod_matmul(x, scale, shift, w, S):
    T, D = x.shape
    N = w.shape[1]
    tm, tn = 512, 512
    per_seq = S // tm
    return pl.pallas_call(
        functools.partial(_modmm_kernel, tn=tn),
        grid=(T // tm,),
        in_specs=[
            pl.BlockSpec((tm, D), lambda i: (i, 0)),
            pl.BlockSpec((1, 1, D), lambda i: (i // per_seq, 0, 0)),
            pl.BlockSpec((1, 1, D), lambda i: (i // per_seq, 0, 0)),
            pl.BlockSpec((D, N), lambda i: (0, 0), pipeline_mode=pl.Buffered(1)),
        ],
        out_specs=pl.BlockSpec((tm, N), lambda i: (i, 0)),
        out_shape=jax.ShapeDtypeStruct((T, N), BF16),
        compiler_params=_cparams(("parallel",)),
        name="mod_matmul",
    )(x, scale, shift, w)


def _mla_prep_kernel(p_ref, qn_ref, kvn_ref, wq_ref, wk_ref, wv_ref, er_ref, ct_ref, st_ref, mr_ref,
                     q_ref, k_ref, v_ref):
    p = p_ref[...]
    ql = p[:, :MLA_Q_RANK].astype(F32)
    kl = p[:, MLA_Q_RANK:MLA_Q_RANK + MLA_KV_RANK].astype(F32)
    misc = p[:, MISC_COL:MISC_COL + LANES].astype(F32)
    c_q = (ql * lax.rsqrt(jnp.mean(ql * ql, axis=-1, keepdims=True) + RMS_EPS) * qn_ref[...]).astype(BF16)
    c_kv = (kl * lax.rsqrt(jnp.mean(kl * kl, axis=-1, keepdims=True) + RMS_EPS) * kvn_ref[...]).astype(BF16)
    ab = _mm(c_q, wq_ref[...])
    ct = ct_ref[...]
    st = st_ref[...]
    half = MLA_HEADS * LANES
    for h in range(MLA_HEADS):
        lo = h * LANES
        q_ref[:, lo:lo + LANES] = (ab[:, lo:lo + LANES] * ct + ab[:, half + lo:half + lo + LANES] * st).astype(BF16)
    k_rope = _mm((misc * mr_ref[...]).astype(BF16), er_ref[...])
    k_ref[...] = (_mm(c_kv, wk_ref[...]) + k_rope).astype(BF16)
    v_ref[...] = _mm(c_kv, wv_ref[...]).astype(BF16)


def _mla_prep(proj, qn, kvn, wq, wk, wv, erope, ctab, stab, mrope, S):
    T = proj.shape[0]
    tm = 512
    per_seq = S // tm
    hw = MLA_HEADS * LANES
    const = lambda i: (0, 0)
    seq = lambda i: (i % per_seq, 0)
    return pl.pallas_call(
        _mla_prep_kernel,
        grid=(T // tm,),
        in_specs=[
            pl.BlockSpec((tm, NSA_Q_COL), lambda i: (i, 0)),
            pl.BlockSpec((1, MLA_Q_RANK), const),
            pl.BlockSpec((1, MLA_KV_RANK), const),
            pl.BlockSpec((MLA_Q_RANK, 2 * hw), const),
            pl.BlockSpec((MLA_KV_RANK, hw), const),
            pl.BlockSpec((MLA_KV_RANK, MLA_HEADS * MLA_V), const),
            pl.BlockSpec((LANES, hw), const),
            pl.BlockSpec((tm, LANES), seq),
            pl.BlockSpec((tm, LANES), seq),
            pl.BlockSpec((tm, LANES), seq),
        ],
        out_specs=[
            pl.BlockSpec((tm, hw), lambda i: (i, 0)),
            pl.BlockSpec((tm, hw), lambda i: (i, 0)),
            pl.BlockSpec((tm, MLA_HEADS * MLA_V), lambda i: (i, 0)),
        ],
        out_shape=[
            jax.ShapeDtypeStruct((T, hw), BF16),
            jax.ShapeDtypeStruct((T, hw), BF16),
            jax.ShapeDtypeStruct((T, MLA_HEADS * MLA_V), BF16),
        ],
        compiler_params=_cparams(("parallel",)),
        name="mla_prep",
    )(proj, qn, kvn, wq, wk, wv, erope, ctab, stab, mrope)


def _mla_attn_kernel(q_ref, k_ref, v_ref, cm_ref, id_ref, o_ref, vt_ref, s_ref, *, n_tiles):
    vt_ref[...] = _nt(id_ref[...], v_ref[...]).astype(BF16)
    for qi in range(n_tiles):
        rows = slice(qi * TA, (qi + 1) * TA)
        outs = []
        for hh in range(2):
            cs = slice(hh * LANES, (hh + 1) * LANES)
            q = q_ref[rows, cs]

            def score(c, q=q, cs=cs, qi=qi):
                s = _nt(k_ref[c * TA:(c + 1) * TA, cs], q)
                return s + cm_ref[...] if c == qi else s

            def v_t(c, hh=hh):
                return vt_ref[hh * MLA_V:(hh + 1) * MLA_V, c * TA:(c + 1) * TA]

            acc, l = _softmax_pv_t(range(qi + 1), score, v_t, s_ref)
            outs.append(acc * (1.0 / l))
        o_ref[rows, :] = jnp.concatenate(outs, axis=0).T.astype(BF16)


def _mla_attention(q, k, v, cmask_t, ident, B, S):
    pairs = MLA_HEADS // 2
    return pl.pallas_call(
        functools.partial(_mla_attn_kernel, n_tiles=S // TA),
        grid=(B, pairs),
        in_specs=[
            pl.BlockSpec((S, 2 * LANES), lambda b, hp: (b, hp)),
            pl.BlockSpec((S, 2 * LANES), lambda b, hp: (b, hp)),
            pl.BlockSpec((S, LANES), lambda b, hp: (b, hp)),
            pl.BlockSpec((TA, TA), lambda b, hp: (0, 0)),
            pl.BlockSpec((LANES, LANES), lambda b, hp: (0, 0)),
        ],
        out_specs=pl.BlockSpec((S, LANES), lambda b, hp: (b, hp)),
        out_shape=jax.ShapeDtypeStruct((B * S, MLA_HEADS * MLA_V), BF16),
        scratch_shapes=[pltpu.VMEM((LANES, S), BF16), pltpu.VMEM((S, TA), F32)],
        compiler_params=_cparams(("parallel", "parallel")),
        name="mla_attention",
    )(q, k, v, cmask_t, ident)


def _compress_kernel(r_ref, pe_ref, w1_ref, w2_ref, w2t_ref, o_ref, ot_ref):
    half = CMP_STRIDE * NSA_DH
    r = r_ref[0, 0].astype(F32)
    pe = pe_ref[0]
    x_lo = (r + pe[:, :half]).astype(BF16)
    x_hi = (r + pe[:, half:]).astype(BF16)
    y_lo = _mm(x_lo, w1_ref[0, :half, :])
    y_hi = _mm(x_hi, w1_ref[0, half:, :])
    y = y_lo + pltpu.roll(y_hi, N_CMP_PAD - 1, 0)
    hid = jax.nn.gelu(y).astype(BF16)
    o_ref[0, 0] = _mm(hid, w2_ref[0]).astype(BF16)
    ot_ref[0, 0] = _nt(w2t_ref[0], hid).astype(BF16)


def _compress(r, pe, w1, w2_rep, w2_t, B):
    rep = NSA_HPG * NSA_DH
    return pl.pallas_call(
        _compress_kernel,
        grid=(4, B),
        in_specs=[
            pl.BlockSpec((1, 1, N_CMP_PAD, CMP_STRIDE * NSA_DH), lambda c, b: (c, b, 0, 0)),
            pl.BlockSpec((1, 1, CMP_LEN * NSA_DH), lambda c, b: (c // 2, 0, 0)),
            pl.BlockSpec((1, CMP_LEN * NSA_DH, CMP_HIDDEN), lambda c, b: (c // 2, 0, 0)),
            pl.BlockSpec((1, CMP_HIDDEN, rep), lambda c, b: (c // 2, 0, 0)),
            pl.BlockSpec((1, NSA_DH, CMP_HIDDEN), lambda c, b: (c // 2, 0, 0)),
        ],
        out_specs=[
            pl.BlockSpec((1, 1, N_CMP_PAD, rep), lambda c, b: (c, b, 0, 0)),
            pl.BlockSpec((1, 1, NSA_DH, N_CMP_PAD), lambda c, b: (c, b, 0, 0)),
        ],
        out_shape=[
            jax.ShapeDtypeStruct((4, B, N_CMP_PAD, rep), BF16),
            jax.ShapeDtypeStruct((4, B, NSA_DH, N_CMP_PAD), BF16),
        ],
        compiler_params=_cparams(("parallel", "parallel")),
        name="nsa_compress",
    )(r, pe, w1, w2_rep, w2_t)


def _nsa_kernel(tbl_ref, q_ref, kv_ref, misc_ref, kc_ref, vc_ref, bc_ref, toe_ref, erep_ref, ovl_ref,
                eexp_ref, egate_ref, o_ref,
                sk_ref, wk_ref, sv_ref, wv_ref, m_ref, l_ref, acc_ref, *, n_tiles):
    g = pl.program_id(1)
    J = NSA_HPG
    W = J * NSA_DH
    kv = kv_ref[...]
    sk_ref[...] = _mm(kv, erep_ref[0]).astype(BF16)
    wk_ref[...] = _mm(kv, erep_ref[1]).astype(BF16)
    sv_ref[...] = _mm(kv, erep_ref[2]).astype(BF16)
    wv_ref[...] = _mm(kv, erep_ref[3]).astype(BF16)

    lane_w = lax.broadcasted_iota(jnp.int32, (TA, W), 1)
    head_of_lane = lane_w >> 6
    row = lax.broadcasted_iota(jnp.int32, (TA, LANES), 0)
    lane = lax.broadcasted_iota(jnp.int32, (TA, LANES), 1)
    far = [tbl_ref[FAR_BUCKET, g * J + j] for j in range(J)]

    def stack(fn):
        return jnp.concatenate([fn(j) for j in range(J)], axis=0)

    def finish():
        return acc_ref[...] * (1.0 / l_ref[...])

    def q_tile(qi, carry):
        r0 = pl.multiple_of(qi * TA, TA)
        q = q_ref[pl.ds(r0, TA), :]
        q4 = stack(lambda j: jnp.where(head_of_lane == j, q, jnp.zeros_like(q)))
        t_abs = row + qi * TA

        s = _nt(q4, kc_ref[0, 0]) + stack(lambda j: bc_ref[j, pl.ds(r0, TA), :])
        m = jnp.max(s, axis=-1, keepdims=True)
        e = jnp.exp(s - m)
        p = e * (1.0 / jnp.sum(e, axis=-1, keepdims=True))
        p = jnp.where(jnp.concatenate([t_abs] * J, axis=0) >= CMP_LEN - 1, p, 0.0)
        o_cmp = _mm(p.astype(BF16), vc_ref[0, 0])
        p_sum = p[0:TA] + p[TA:2 * TA] + p[2 * TA:3 * TA] + p[3 * TA:4 * TA]

        hi, mid, lo = _split3(p_sum)
        imp = _mm(hi, ovl_ref[...]) + _mm(mid, ovl_ref[...]) + _mm(lo, ovl_ref[...])
        cur = t_abs >> 6
        forced = (lane == 0) | (lane == cur) | (lane == cur - 1)
        score = jnp.where(forced, FORCE_SCORE, imp)
        score = jnp.where(lane <= cur, score, -1.0)
        rank = jnp.zeros((TA, LANES), F32)
        for mp in range(n_tiles * TA // SEL_LEN):
            col = score[:, mp:mp + 1]
            beats = (col > score) | ((col == score) & (lane > mp))
            rank = rank + jnp.where(beats, 1.0, 0.0)
        sel = jnp.where(rank < SEL_TOP, 1.0, 0.0).astype(BF16)

        def key_pen(kj):
            k0 = pl.multiple_of(kj * TA, TA)
            on = _mm(sel, eexp_ref[:, pl.ds(k0, TA)])
            return (on - 1.0) * (-NEG_INF)

        _reset(m_ref, l_ref, acc_ref)
        pen = key_pen(qi)
        s = _nt(q4, sk_ref[pl.ds(r0, TA), :]) + stack(lambda j: toe_ref[j, 0:TA, :] + pen)
        _online_update(s, sv_ref[pl.ds(r0, TA), :], m_ref, l_ref, acc_ref)
        kprev = jnp.maximum(qi - 1, 0)
        p0 = pl.multiple_of(kprev * TA, TA)
        off = jnp.where(qi == 0, NEG_INF, 0.0)
        pen = key_pen(kprev) + off
        s = _nt(q4, sk_ref[pl.ds(p0, TA), :]) + stack(lambda j: toe_ref[j, TA:2 * TA, :] + pen)
        _online_update(s, sv_ref[pl.ds(p0, TA), :], m_ref, l_ref, acc_ref)

        def far_tile(kj, c):
            k0 = pl.multiple_of(kj * TA, TA)
            pen = key_pen(kj)
            s = _nt(q4, sk_ref[pl.ds(k0, TA), :]) + stack(lambda j: pen + far[j])
            _online_update(s, sv_ref[pl.ds(k0, TA), :], m_ref, l_ref, acc_ref)
            return c

        lax.fori_loop(0, qi - 1, far_tile, 0)
        o_slc = finish()

        _reset(m_ref, l_ref, acc_ref)
        s = _nt(q4, wk_ref[pl.ds(r0, TA), :]) + stack(lambda j: toe_ref[j, 0:TA, :])
        _online_update(s, wv_ref[pl.ds(r0, TA), :], m_ref, l_ref, acc_ref)
        s = _nt(q4, wk_ref[pl.ds(p0, TA), :]) + stack(lambda j: toe_ref[j, 2 * TA:3 * TA, :] + off)
        _online_update(s, wv_ref[pl.ds(p0, TA), :], m_ref, l_ref, acc_ref)
        o_win = finish()

        gl = misc_ref[pl.ds(r0, TA), :].astype(F32)
        gates = _mm((1.0 / (1.0 + jnp.exp(-gl))).astype(BF16), egate_ref[0])
        out = jnp.zeros((TA, W), F32)
        for j in range(J):
            rs = slice(j * TA, (j + 1) * TA)
            mix = (gates[:, 0:W] * o_cmp[rs] + gates[:, W:2 * W] * o_slc[rs] + gates[:, 2 * W:3 * W] * o_win[rs])
            out = jnp.where(head_of_lane == j, mix, out)
        o_ref[pl.ds(r0, TA), :] = out.astype(BF16)
        return carry

    lax.fori_loop(0, n_tiles, q_tile, 0)


def _nsa_attention(proj, kcmp, table, bias_cmp, toe, erep, ovl, eexp, egate, B, S):
    J = NSA_HPG
    W = J * NSA_DH
    q_blk = NSA_Q_COL // W
    kv_blk = NSA_KV_COL // W
    misc_blk = MISC_COL // LANES
    return pl.pallas_call(
        functools.partial(_nsa_kernel, n_tiles=S // TA),
        grid=(B, NSA_GROUPS),
        in_specs=[
            pl.BlockSpec(memory_space=pltpu.SMEM),
            pl.BlockSpec((S, W), lambda b, g: (b, q_blk + g)),
            pl.BlockSpec((S, W), lambda b, g: (b, kv_blk + g)),
            pl.BlockSpec((S, LANES), lambda b, g: (b, misc_blk)),
            pl.BlockSpec((1, 1, N_CMP_PAD, W), lambda b, g: (g, b, 0, 0)),
            pl.BlockSpec((1, 1, N_CMP_PAD, W), lambda b, g: (2 + g, b, 0, 0)),
            pl.BlockSpec((J, S, N_CMP_PAD), lambda b, g: (g, 0, 0)),
            pl.BlockSpec((J, 3 * TA, TA), lambda b, g: (g, 0, 0)),
            pl.BlockSpec((4, W, W), lambda b, g: (0, 0, 0)),
            pl.BlockSpec((N_CMP_PAD, LANES), lambda b, g: (0, 0)),
            pl.BlockSpec((LANES, S), lambda b, g: (0, 0)),
            pl.BlockSpec((1, LANES, 3 * W), lambda b, g: (g, 0, 0)),
        ],
        out_specs=pl.BlockSpec((S, W), lambda b, g: (b, g)),
        out_shape=jax.ShapeDtypeStruct((B * S, NSA_HEADS * NSA_DH), BF16),
        scratch_shapes=[pltpu.VMEM((S, W), BF16)] * 4 + [
            pltpu.VMEM((J * TA, 1), F32), pltpu.VMEM((J * TA, 1), F32), pltpu.VMEM((J * TA, W), F32)],
        compiler_params=_cparams(("parallel", "parallel")),
        name="nsa_attention",
    )(table, proj, proj, proj, kcmp, kcmp, bias_cmp, toe, erep, ovl, eexp, egate)


def _nsa_t_kernel(q_ref, kv_ref, misc_ref, kc_ref, vct_ref, bc_ref, toe_ref, erep_ref, esel_ref, ovl_ref, o_ref,
                  sk_ref, wk_ref, svt_ref, wvt_ref, s_ref, *, n_tiles):
    g = pl.program_id(1)
    J = NSA_HPG
    W = J * NSA_DH
    QW = J * TA
    n_blk = n_tiles * TA // SEL_LEN
    blk_per_chunk = TA // SEL_LEN
    kv = kv_ref[...]
    sk_ref[...] = _mm(kv, erep_ref[0]).astype(BF16)
    wk_ref[...] = _mm(kv, erep_ref[1]).astype(BF16)
    svt_ref[...] = _nt(esel_ref[0], kv).astype(BF16)
    wvt_ref[...] = _nt(esel_ref[1], kv).astype(BF16)

    head_of_lane = lax.broadcasted_iota(jnp.int32, (TA, W), 1) >> 6
    blk = lax.broadcasted_iota(jnp.int32, (n_blk, TA), 0)
    qpos = lax.broadcasted_iota(jnp.int32, (n_blk, TA), 1)
    g_is0 = g == 0

    for qi in range(n_tiles):
        rows = slice(qi * TA, (qi + 1) * TA)
        q = q_ref[rows, :]
        zero = jnp.zeros_like(q)
        q4 = jnp.concatenate([jnp.where(head_of_lane == j, q, zero) for j in range(J)], axis=0)

        s = _nt(kc_ref[0, 0], q4) + bc_ref[0, qi]
        e = jnp.exp(s - jnp.max(s, axis=0, keepdims=True))
        p = e * (1.0 / jnp.sum(e, axis=0, keepdims=True))
        if qi * TA < CMP_LEN - 1:
            tq = (lax.broadcasted_iota(jnp.int32, p.shape, 1) & (TA - 1)) + qi * TA
            p = jnp.where(tq >= CMP_LEN - 1, p, 0.0)
        o_cmp = _mm(vct_ref[0, 0], p.astype(BF16))
        p_sum = p[:, 0:TA]
        for j in range(1, J):
            p_sum = p_sum + p[:, j * TA:(j + 1) * TA]

        hi, mid, lo = _split3(p_sum)
        imp = _mm(ovl_ref[...], hi) + _mm(ovl_ref[...], mid) + _mm(ovl_ref[...], lo)
        cur = (qpos + qi * TA) >> 6
        forced = (blk == 0) | (blk == cur) | (blk == cur - 1)
        score = jnp.where(forced, FORCE_SCORE, imp)
        score = jnp.where(blk <= cur, score, -1.0)
        rank = jnp.zeros((n_blk, TA), F32)
        for mp in range(n_blk):
            r = score[mp:mp + 1, :]
            beats = (r > score) | ((r == score) & (blk > mp))
            rank = rank + jnp.where(beats, 1.0, 0.0)
        pen = jnp.where(rank < SEL_TOP, 0.0, NEG_INF)
        pen4 = jnp.concatenate([pen] * J, axis=1)

        def pen_rows(c, pen4=pen4):
            parts = [jnp.broadcast_to(pen4[c * blk_per_chunk + i:c * blk_per_chunk + i + 1, :], (SEL_LEN, QW))
                     for i in range(blk_per_chunk)]
            return jnp.concatenate(parts, axis=0)

        def score_sel(c, q4=q4, qi=qi, pen_rows=pen_rows):
            s = _nt(sk_ref[c * TA:(c + 1) * TA, :], q4) + pen_rows(c)
            if c == qi:
                return s + toe_ref[0, 0]
            if c == qi - 1:
                return s + toe_ref[0, 1]
            return s

        acc, l = _softmax_pv_t(range(qi + 1), score_sel, lambda c: svt_ref[:, c * TA:(c + 1) * TA], s_ref)
        o_slc = acc * (1.0 / l)

        def score_win(c, q4=q4, qi=qi):
            s = _nt(wk_ref[c * TA:(c + 1) * TA, :], q4)
            return s + (toe_ref[0, 0] if c == qi else toe_ref[0, 2])

        acc, l = _softmax_pv_t(range(max(qi - 1, 0), qi + 1), score_win,
                               lambda c: wvt_ref[:, c * TA:(c + 1) * TA], s_ref)
        o_win = acc * (1.0 / l)

        gl = misc_ref[rows, :].astype(F32)
        sig_t = (1.0 / (1.0 + jnp.exp(-gl))).T
        mixes = []
        for j in range(J):
            def gate(br, j=j):
                r0 = 2 * MLA_ROPE + j * 3 + br
                r1 = r0 + J * 3
                return jnp.where(g_is0, sig_t[r0:r0 + 1, :], sig_t[r1:r1 + 1, :])

            cs = slice(j * TA, (j + 1) * TA)
            mixes.append(gate(0) * o_cmp[:, cs] + gate(1) * o_slc[:, cs] + gate(2) * o_win[:, cs])
        o_ref[rows, :] = jnp.concatenate(mixes, axis=0).T.astype(BF16)


def _nsa_attention_t(proj, kcmp, vcmp_t, bias_cmp_t, toe_g, erep, esel, ovl_t, B, S):
    J = NSA_HPG
    W = J * NSA_DH
    n_tiles = S // TA
    n_blk = S // SEL_LEN
    q_blk = NSA_Q_COL // W
    kv_blk = NSA_KV_COL // W
    misc_blk = MISC_COL // LANES
    return pl.pallas_call(
        functools.partial(_nsa_t_kernel, n_tiles=n_tiles),
        grid=(B, NSA_GROUPS),
        in_specs=[
            pl.BlockSpec((S, W), lambda b, g: (b, q_blk + g)),
            pl.BlockSpec((S, W), lambda b, g: (b, kv_blk + g)),
            pl.BlockSpec((S, LANES), lambda b, g: (b, misc_blk)),
            pl.BlockSpec((1, 1, N_CMP_PAD, W), lambda b, g: (g, b, 0, 0)),
            pl.BlockSpec((1, 1, NSA_DH, N_CMP_PAD), lambda b, g: (2 + g, b, 0, 0)),
            pl.BlockSpec((1, n_tiles, N_CMP_PAD, J * TA), lambda b, g: (g, 0, 0, 0)),
            pl.BlockSpec((1, 3, TA, J * TA), lambda b, g: (g, 0, 0, 0)),
            pl.BlockSpec((2, W, W), lambda b, g: (0, 0, 0)),
            pl.BlockSpec((2, NSA_DH, W), lambda b, g: (0, 0, 0)),
            pl.BlockSpec((n_blk, N_CMP_PAD), lambda b, g: (0, 0)),
        ],
        out_specs=pl.BlockSpec((S, W), lambda b, g: (b, g)),
        out_shape=jax.ShapeDtypeStruct((B * S, NSA_HEADS * NSA_DH), BF16),
        scratch_shapes=[pltpu.VMEM((S, W), BF16), pltpu.VMEM((S, W), BF16),
                        pltpu.VMEM((NSA_DH, S), BF16), pltpu.VMEM((NSA_DH, S), BF16),
                        pltpu.VMEM((S, J * TA), F32)],
        compiler_params=_cparams(("parallel", "parallel")),
        name="nsa_attention",
    )(proj, proj, proj, kcmp, vcmp_t, bias_cmp_t, toe_g, erep, esel, ovl_t)


def _diff_kernel(q_ref, k_ref, v_ref, toe_ref, lam_ref, sub_ref, id_ref, o_ref, vt_ref, s_ref, *, n_tiles, lam_init):
    hd = DIFF_DH
    lane = lax.broadcasted_iota(jnp.int32, (TA, 2 * hd), 1)
    lv = lam_ref[...]
    lam = (jnp.exp(jnp.sum(lv[0:1] * lv[1:2], axis=-1, keepdims=True))
           - jnp.exp(jnp.sum(lv[2:3] * lv[3:4], axis=-1, keepdims=True)) + lam_init)
    vt_ref[...] = _nt(id_ref[...], v_ref[...]).astype(BF16)

    def both(t):
        return jnp.concatenate([t, t], axis=1)

    for qi in range(n_tiles):
        rows = slice(qi * TA, (qi + 1) * TA)
        q = q_ref[rows, :]
        zero = jnp.zeros_like(q)
        q2 = jnp.concatenate([jnp.where(lane < hd, q, zero), jnp.where(lane >= hd, q, zero)], axis=0)

        def score(c, q2=q2, qi=qi):
            s = _nt(k_ref[c * TA:(c + 1) * TA, :], q2)
            if c == qi:
                return s + both(toe_ref[0, 0:TA, :])
            if c == qi - 1:
                return s + both(toe_ref[0, TA:2 * TA, :])
            return s

        acc, l = _softmax_pv_t(range(qi + 1), score, lambda c: vt_ref[:, c * TA:(c + 1) * TA], s_ref)
        o2 = acc * (1.0 / l)
        o = (o2[:, 0:TA] - lam * o2[:, TA:2 * TA]).T
        o = o * lax.rsqrt(jnp.mean(o * o, axis=-1, keepdims=True) + RMS_EPS) * sub_ref[...]
        o_ref[rows, :] = (o * (1.0 - lam_init)).astype(BF16)


def _diff_attention(proj, toe_t, lam_vec, subln, ident, layer_idx, B, S):
    H = DIFF_HEADS
    hw = 2 * DIFF_DH
    lam_init = 0.8 - 0.6 * math.exp(-0.3 * layer_idx)
    return pl.pallas_call(
        functools.partial(_diff_kernel, n_tiles=S // TA, lam_init=lam_init),
        grid=(B, H),
        in_specs=[
            pl.BlockSpec((S, hw), lambda b, h: (b, h)),
            pl.BlockSpec((S, hw), lambda b, h: (b, H + h)),
            pl.BlockSpec((S, hw), lambda b, h: (b, 2 * H + h)),
            pl.BlockSpec((1, 3 * TA, TA), lambda b, h: (h, 0, 0)),
            pl.BlockSpec((4, DIFF_DH), lambda b, h: (0, 0)),
            pl.BlockSpec((1, hw), lambda b, h: (0, 0)),
            pl.BlockSpec((hw, hw), lambda b, h: (0, 0)),
        ],
        out_specs=pl.BlockSpec((S, hw), lambda b, h: (b, h)),
        out_shape=jax.ShapeDtypeStruct((B * S, H * hw), BF16),
        scratch_shapes=[pltpu.VMEM((hw, S), BF16), pltpu.VMEM((S, 2 * TA), F32)],
        compiler_params=_cparams(("parallel", "parallel")),
        name="diff_attention",
    )(proj, proj, proj, toe_t, lam_vec, subln.reshape(1, hw), ident)


def _outproj_kernel(oa_ref, ob_ref, w_ref, x_ref, g_ref, lg_ref, lb_ref, o_ref):
    half = oa_ref.shape[1]
    y = _mm(oa_ref[...], w_ref[:half, :]) + _mm(ob_ref[...], w_ref[half:, :])
    z = ALPHA * x_ref[...] + (1.0 + g_ref[0]) * y
    o_ref[...] = _layer_norm_rows(z, lg_ref[...], lb_ref[...])


def _outproj_ln(o_a, o_b, blk_a, blk_b, w, x, gate, ln_g, ln_b, S):
    T, D = x.shape
    tm = 512
    half = w.shape[0] // 2
    per_seq = S // tm
    return pl.pallas_call(
        _outproj_kernel,
        grid=(T // tm,),
        in_specs=[
            pl.BlockSpec((tm, half), lambda i: (i, blk_a)),
            pl.BlockSpec((tm, half), lambda i: (i, blk_b)),
            pl.BlockSpec((2 * half, D), lambda i: (0, 0)),
            pl.BlockSpec((tm, D), lambda i: (i, 0)),
            pl.BlockSpec((1, 1, D), lambda i: (i // per_seq, 0, 0)),
            pl.BlockSpec((1, D), lambda i: (0, 0)),
            pl.BlockSpec((1, D), lambda i: (0, 0)),
        ],
        out_specs=pl.BlockSpec((tm, D), lambda i: (i, 0)),
        out_shape=jax.ShapeDtypeStruct((T, D), F32),
        compiler_params=_cparams(("parallel",)),
        name="outproj_ln",
    )(o_a, o_b, w, x, gate, ln_g, ln_b)


def _ffn_kernel(x_ref, xh_ref, sc_ref, sh_ref, g_ref, wu_ref, wg_ref, cw_ref, cb_ref, wd_ref, lg_ref, lb_ref,
                o_ref, h_ref, hh_ref, acc_ref, *, per_seq, n_ff):
    i = pl.program_id(0)
    f = pl.program_id(1)

    @pl.when(f == 0)
    def _():
        scale = 1.0 + sc_ref[0]
        shift = sh_ref[0]
        h_ref[...] = (x_ref[...] * scale + shift).astype(BF16)
        hh_ref[...] = (xh_ref[...] * scale + shift).astype(BF16)
        acc_ref[...] = jnp.zeros(acc_ref.shape, F32)

    h = h_ref[...]
    u = _mm(h, wu_ref[...])
    gt = _mm(h, wg_ref[...])
    gh = _mm(hh_ref[...], wg_ref[...]) * jnp.where(i % per_seq == 0, 0.0, 1.0)
    row = lax.broadcasted_iota(jnp.int32, gt.shape, 0)
    prev1 = jnp.where(row == 0, gh[7:8], pltpu.roll(gt, 1, 0))
    prev2 = jnp.where(row == 0, gh[6:7], jnp.where(row == 1, gh[7:8], pltpu.roll(gt, 2, 0)))
    cw = cw_ref[...]
    a = cw[2:3] * gt + cw[1:2] * prev1 + cw[0:1] * prev2 + cb_ref[...]
    act = (a / (1.0 + jnp.exp(-a)) * u).astype(BF16)
    acc_ref[...] += _mm(act, wd_ref[...])

    @pl.when(f == n_ff - 1)
    def _():
        z = ALPHA * x_ref[...] + (1.0 + g_ref[0]) * acc_ref[...]
        o_ref[...] = _layer_norm_rows(z, lg_ref[...], lb_ref[...])


def _conv_ffn_ln(x, scale, shift, gate, w_up, w_gate, conv_w, conv_b, w_down, ln_g, ln_b, S):
    T, D = x.shape
    tm, tf = 1024, 256
    per_seq = S // tm
    n_ff = D_FF // tf
    halo = 8
    mod_spec = pl.BlockSpec((1, 1, D), lambda i, f: (i // per_seq, 0, 0))
    return pl.pallas_call(
        functools.partial(_ffn_kernel, per_seq=per_seq, n_ff=n_ff),
        grid=(T // tm, n_ff),
        in_specs=[
            pl.BlockSpec((tm, D), lambda i, f: (i, 0)),
            pl.BlockSpec((halo, D), lambda i, f: (jnp.maximum(i * (tm // halo) - 1, 0), 0)),
            mod_spec, mod_spec, mod_spec,
            pl.BlockSpec((D, tf), lambda i, f: (0, f)),
            pl.BlockSpec((D, tf), lambda i, f: (0, f)),
            pl.BlockSpec((3, tf), lambda i, f: (0, f)),
            pl.BlockSpec((1, tf), lambda i, f: (0, f)),
            pl.BlockSpec((tf, D), lambda i, f: (f, 0)),
            pl.BlockSpec((1, D), lambda i, f: (0, 0)),
            pl.BlockSpec((1, D), lambda i, f: (0, 0)),
        ],
        out_specs=pl.BlockSpec((tm, D), lambda i, f: (i, 0)),
        out_shape=jax.ShapeDtypeStruct((T, D), F32),
        scratch_shapes=[pltpu.VMEM((tm, D), BF16), pltpu.VMEM((halo, D), BF16), pltpu.VMEM((tm, D), F32)],
        compiler_params=_cparams(("parallel", "arbitrary")),
        name="conv_ffn_ln",
    )(x, x, scale, shift, gate, w_up, w_gate, conv_w, conv_b, w_down, ln_g, ln_b)


CONV_HALO = 16


def _tail_kernel(oa_ref, ob_ref, wo_ref, x_ref, g1_ref, lg1_ref, lb1_ref, sc_ref, sh_ref, g2_ref,
                 wu_ref, wg_ref, cw_ref, cb_ref, wd_ref, lg2_ref, lb2_ref, o_ref, hprev_ref, act_ref,
                 *, per_seq, tf):
    i = pl.program_id(0)
    half = oa_ref.shape[1]
    tm = x_ref.shape[0]
    y = _mm(oa_ref[...], wo_ref[:half, :]) + _mm(ob_ref[...], wo_ref[half:, :])
    x1 = _layer_norm_rows(ALPHA * x_ref[...] + (1.0 + g1_ref[0]) * y, lg1_ref[...], lb1_ref[...])
    o_ref[...] = x1
    h = (x1 * (1.0 + sc_ref[0]) + sh_ref[0]).astype(BF16)

    @pl.when(i % per_seq == 0)
    def _():
        hprev_ref[...] = jnp.zeros(hprev_ref.shape, BF16)

    h_ext = jnp.concatenate([hprev_ref[...], h], axis=0)
    hprev_ref[...] = h[tm - CONV_HALO:, :]
    for f in range(D_FF // tf):
        cs = slice(f * tf, (f + 1) * tf)
        u = _mm(h, wu_ref[:, cs])
        ge = _mm(h_ext, wg_ref[:, cs])
        cw = cw_ref[:, cs]
        a = (cw[2:3] * ge[CONV_HALO:CONV_HALO + tm] + cw[1:2] * ge[CONV_HALO - 1:CONV_HALO - 1 + tm]
             + cw[0:1] * ge[CONV_HALO - 2:CONV_HALO - 2 + tm] + cb_ref[:, cs])
        act_ref[:, cs] = (a / (1.0 + jnp.exp(-a)) * u).astype(BF16)
    y2 = _mm(act_ref[...], wd_ref[...])
    z = ALPHA * o_ref[...] + (1.0 + g2_ref[0]) * y2
    o_ref[...] = _layer_norm_rows(z, lg2_ref[...], lb2_ref[...])


def _layer_tail(o_a, o_b, blk_a, blk_b, w_o, x, g1, ln1_g, ln1_b, scale, shift, g2, w_up, w_gate, conv_w, conv_b,
                w_down, ln2_g, ln2_b, S):
    T, D = x.shape
    tm, tf = 512, 256
    half = w_o.shape[0] // 2
    per_seq = S // tm
    once = pl.Buffered(1)

    def const(shape):
        return pl.BlockSpec(shape, lambda i: (0,) * len(shape), pipeline_mode=once)

    mod_spec = pl.BlockSpec((1, 1, D), lambda i: (i // per_seq, 0, 0))
    return pl.pallas_call(
        functools.partial(_tail_kernel, per_seq=per_seq, tf=tf),
        grid=(T // tm,),
        in_specs=[
            pl.BlockSpec((tm, half), lambda i: (i, blk_a)),
            pl.BlockSpec((tm, half), lambda i: (i, blk_b)),
            const((2 * half, D)),
            pl.BlockSpec((tm, D), lambda i: (i, 0)),
            mod_spec, const((1, D)), const((1, D)),
            mod_spec, mod_spec, mod_spec,
            const((D, D_FF)), const((D, D_FF)), const((3, D_FF)), const((1, D_FF)), const((D_FF, D)),
            const((1, D)), const((1, D)),
        ],
        out_specs=pl.BlockSpec((tm, D), lambda i: (i, 0)),
        out_shape=jax.ShapeDtypeStruct((T, D), F32),
        scratch_shapes=[pltpu.VMEM((CONV_HALO, D), BF16), pltpu.VMEM((tm, D_FF), BF16)],
        compiler_params=_cparams(("arbitrary",)),
        name="layer_tail",
    )(o_a, o_b, w_o, x, g1, ln1_g, ln1_b, scale, shift, g2, w_up, w_gate, conv_w, conv_b, w_down, ln2_g, ln2_b)


def _even_in_columns():
    src = np.zeros(EVEN_W, np.int32)
    mul = np.zeros(EVEN_W, np.float32)

    def put(dst, cols, scale=1.0):
        cols = np.asarray(cols)
        src[dst:dst + len(cols)] = cols
        mul[dst:dst + len(cols)] = scale

    put(0, np.arange(MLA_Q_RANK + MLA_KV_RANK))
    rope0 = MLA_Q_RANK + MLA_KV_RANK
    half = MLA_ROPE // 2
    put(MISC_COL, rope0 + np.arange(MLA_ROPE))
    put(MISC_COL + MLA_ROPE, rope0 + half + np.arange(half), -1.0)
    put(MISC_COL + MLA_ROPE + half, rope0 + np.arange(half), 1.0)
    nsa0 = MLA_COLS
    put(MISC_COL + 2 * MLA_ROPE, nsa0 + NSA_Q_COLS + NSA_KV_COLS + np.arange(NSA_GATE_COLS))
    put(NSA_Q_COL, nsa0 + np.arange(NSA_Q_COLS), NSA_DH ** -0.5)
    kv0 = nsa0 + NSA_Q_COLS

    def chunk(branch, kv, g):
        return kv0 + ((branch * 2 + kv) * NSA_GROUPS + g) * NSA_DH + np.arange(NSA_DH)

    for g in range(NSA_GROUPS):
        base = NSA_KV_COL + g * 4 * NSA_DH
        for slot, (branch, kv) in enumerate([(1, 0), (2, 0), (1, 1), (2, 1)]):
            put(base + slot * NSA_DH, chunk(branch, kv, g))
    put(NSA_CMP_COL, kv0 + np.arange(2 * NSA_GROUPS * NSA_DH))
    return src, mul


def _mla_up_columns():
    hw = MLA_HEADS * LANES
    qd = MLA_NOPE + MLA_ROPE
    half = MLA_ROPE // 2
    src = np.zeros(2 * hw, np.int32)
    mul = np.zeros(2 * hw, np.float32)
    for h in range(MLA_HEADS):
        src[h * LANES:h * LANES + qd] = h * qd + np.arange(qd)
        mul[h * LANES:h * LANES + qd] = 1.0
        r = hw + h * LANES + MLA_NOPE
        src[r:r + half] = h * qd + MLA_NOPE + half + np.arange(half)
        mul[r:r + half] = -1.0
        src[r + half:r + MLA_ROPE] = h * qd + MLA_NOPE + np.arange(half)
        mul[r + half:r + MLA_ROPE] = 1.0
    ksrc = np.zeros(hw, np.int32)
    kmul = np.zeros(hw, np.float32)
    vsrc = np.zeros(MLA_HEADS * MLA_V, np.int32)
    for h in range(MLA_HEADS):
        ksrc[h * LANES:h * LANES + MLA_NOPE] = h * (MLA_NOPE + MLA_V) + np.arange(MLA_NOPE)
        kmul[h * LANES:h * LANES + MLA_NOPE] = 1.0
        vsrc[h * MLA_V:(h + 1) * MLA_V] = h * (MLA_NOPE + MLA_V) + MLA_NOPE + np.arange(MLA_V)
    return src, mul, ksrc, kmul, vsrc


def _routing_constants(S):
    hw = MLA_HEADS * LANES
    half = MLA_ROPE // 2
    erope = np.zeros((LANES, hw), np.float32)
    for h in range(MLA_HEADS):
        for i in range(MLA_ROPE):
            erope[i, h * LANES + MLA_NOPE + i] = 1.0
            erope[MLA_ROPE + i, h * LANES + MLA_NOPE + i] = 1.0
    W = NSA_HPG * NSA_DH
    erep = np.zeros((4, W, W), np.float32)
    for slot in range(4):
        for d in range(NSA_DH):
            for j in range(NSA_HPG):
                erep[slot, slot * NSA_DH + d, j * NSA_DH + d] = 1.0
    n_slc = S // SEL_LEN
    starts = np.arange(N_CMP_PAD) * CMP_STRIDE
    jb = np.arange(n_slc)
    ovl = np.zeros((N_CMP_PAD, LANES), np.float32)
    ovl[:, :n_slc] = ((starts[:, None] < (jb[None, :] + 1) * SEL_LEN)
                      & (starts[:, None] + CMP_LEN > jb[None, :] * SEL_LEN))
    ovl[(S - CMP_LEN) // CMP_STRIDE + 1:, :] = 0.0
    eexp = np.zeros((LANES, S), np.float32)
    eexp[np.arange(S) // SEL_LEN, np.arange(S)] = 1.0
    egate = np.zeros((NSA_GROUPS, LANES, 3 * W), np.float32)
    for g in range(NSA_GROUPS):
        for j in range(NSA_HPG):
            for br in range(3):
                src_lane = 2 * MLA_ROPE + (g * NSA_HPG + j) * 3 + br
                egate[g, src_lane, br * W + j * NSA_DH:br * W + (j + 1) * NSA_DH] = 1.0
    cmask = np.where(np.arange(TA)[:, None] >= np.arange(TA)[None, :], 0.0, NEG_INF).astype(np.float32)
    esel = np.zeros((2, NSA_DH, W), np.float32)
    for slot in range(2):
        esel[slot, np.arange(NSA_DH), (2 + slot) * NSA_DH + np.arange(NSA_DH)] = 1.0
    return erope, erep, ovl, eexp, egate, cmask, esel


def _rope_tables(S):
    inv = 1.0 / (ROPE_BASE ** (jnp.arange(0, MLA_ROPE, 2, dtype=jnp.float32) / MLA_ROPE))
    ang = jnp.arange(S, dtype=jnp.float32)[:, None] * inv[None, :]
    cos, sin = jnp.cos(ang), jnp.sin(ang)
    cos2 = jnp.concatenate([cos, cos], axis=-1)
    sin2 = jnp.concatenate([sin, sin], axis=-1)
    scale = (MLA_NOPE + MLA_ROPE) ** -0.5
    z32 = jnp.zeros((S, LANES - MLA_NOPE - MLA_ROPE), F32)
    ctab = jnp.concatenate([jnp.full((S, MLA_NOPE), scale, F32), cos2 * scale, z32], axis=-1)
    stab = jnp.concatenate([jnp.zeros((S, MLA_NOPE), F32), sin2 * scale, z32], axis=-1)
    mrope = jnp.concatenate([cos2, sin2, jnp.zeros((S, LANES - 2 * MLA_ROPE), F32)], axis=-1)
    return ctab, stab, mrope


def kernel(x, c, rel_bias, ev_w_in, mla_q_norm, mla_kv_norm, mla_w_uq, mla_w_ukv, nsa_cmp_pe, nsa_cmp_w1,
           nsa_cmp_w2, ev_w_o, od_w_in, diff_lambda, diff_subln, od_w_o, ada_w, ada_b, ln_g, ln_b, ffn_w_up,
           ffn_w_gate, ffn_conv_w, ffn_conv_b, ffn_w_down):
    B, S, D = x.shape
    assert D == D_MODEL and S % TA == 0 and S // CMP_STRIDE == N_CMP_PAD and S // SEL_LEN <= LANES
    far_np = np.arange(TA + 1, max(S, TA + 2))
    far_bucket = 16 + (np.log(far_np.astype(np.float32) / 16) / math.log(MAX_DISTANCE / 16) * 16).astype(np.int32)
    assert far_bucket.min() >= FAR_BUCKET
    T = B * S
    xf = x.reshape(T, D)

    mod = _ada_mod(c, ada_w, ada_b)
    toe_idx, cmp_idx, toe_t_idx, cmp_t_idx = _bias_index_tiles(S)
    toe_t = _expand_bias(rel_bias, toe_t_idx, rel_far=True)
    bias_cmp_t = _expand_bias(rel_bias, cmp_t_idx)
    G, J, NQ = NSA_GROUPS, NSA_HPG, S // TA
    toe_g = toe_t.reshape(G, J, 3, TA, TA).transpose(0, 2, 3, 1, 4).reshape(G, 3, TA, J * TA)
    bias_cmp_t = bias_cmp_t.reshape(G, J, N_CMP_PAD, NQ, TA).transpose(0, 3, 2, 1, 4).reshape(G, NQ, N_CMP_PAD, J * TA)
    erope, erep, ovl, eexp, egate, cmask, esel = _routing_constants(S)
    ovl_t = np.ascontiguousarray(ovl[:, :S // SEL_LEN].T)
    ident = jnp.eye(LANES, dtype=BF16)
    ctab, stab, mrope = _rope_tables(S)
    ev_src, ev_mul = _even_in_columns()
    q_src, q_mul, k_src, k_mul, v_src = _mla_up_columns()

    def bf(a):
        return jnp.asarray(a).astype(BF16)

    for l in range(DEPTH):
        sh1, sc1, g1, sh2, sc2, g2 = [mod[l, :, k * D:(k + 1) * D].reshape(B, 1, D) for k in range(6)]
        i = l // 2
        if l % 2 == 0:
            w_in = bf(ev_w_in[i][:, ev_src] * ev_mul)
            proj = _mod_matmul(xf, sc1, sh1, w_in, S)
            wq = bf(mla_w_uq[i][:, q_src] * q_mul)
            wk = bf(mla_w_ukv[i][:, k_src] * k_mul)
            wv = bf(mla_w_ukv[i][:, v_src])
            q_m, k_m, v_m = _mla_prep(proj, mla_q_norm[i].reshape(1, -1), mla_kv_norm[i].reshape(1, -1),
                                      wq, wk, wv, bf(erope), ctab, stab, mrope, S)
            o_a = _mla_attention(q_m, k_m, v_m, jnp.asarray(cmask.T), ident, B, S)
            r = proj[:, NSA_CMP_COL:].reshape(B, S, 4, NSA_DH).transpose(2, 0, 1, 3)
            r = r.reshape(4, B, N_CMP_PAD, CMP_STRIDE * NSA_DH)
            w2_rep = jnp.tile(nsa_cmp_w2[i], (1, 1, NSA_HPG))
            kcmp, vcmp_t = _compress(r, nsa_cmp_pe[i].reshape(2, 1, CMP_LEN * NSA_DH), bf(nsa_cmp_w1[i]), bf(w2_rep),
                                     bf(nsa_cmp_w2[i].transpose(0, 2, 1)), B)
            o_b = _nsa_attention_t(proj, kcmp, vcmp_t, bias_cmp_t, toe_g, bf(erep[:2]), bf(esel), bf(ovl_t), B, S)
            o_1, o_2, blk_2, w_o = o_a, o_b, 0, ev_w_o[i]
        else:
            w_in = od_w_in[i].at[:, :DIFF_HEADS * 2 * DIFF_DH].multiply(DIFF_DH ** -0.5)
            proj = _mod_matmul(xf, sc1, sh1, bf(w_in), S)
            o_1 = _diff_attention(proj, toe_t, diff_lambda[i], diff_subln[i], ident, l, B, S)
            o_2, blk_2, w_o = o_1, 1, od_w_o[i]
        xf = _layer_tail(o_1, o_2, 0, blk_2, bf(w_o), xf, g1, ln_g[l, 0:1], ln_b[l, 0:1], sc2, sh2, g2,
                         bf(ffn_w_up[l]), bf(ffn_w_gate[l]), ffn_conv_w[l], ffn_conv_b[l].reshape(1, D_FF),
                         bf(ffn_w_down[l]), ln_g[l, 1:2], ln_b[l, 1:2], S)
    return xf.reshape(B, S, D)
```

```python
import functools
import math

import numpy as np
import jax
import jax.numpy as jnp
from jax import lax
from jax.experimental import pallas as pl
from jax.experimental.pallas import tpu as pltpu

F32 = jnp.float32
BF16 = jnp.bfloat16

D_MODEL = 1024
DEPTH = 4
N_BUCKETS = 32
MAX_DISTANCE = 128
NEG_INF = -1e30

MLA_HEADS = 8
MLA_Q_RANK = 384
MLA_KV_RANK = 256
MLA_NOPE = 64
MLA_ROPE = 32
MLA_V = 64
ROPE_BASE = 10000.0
MLA_COLS = MLA_Q_RANK + MLA_KV_RANK + MLA_ROPE

NSA_HEADS = 8
NSA_GROUPS = 2
NSA_HPG = NSA_HEADS // NSA_GROUPS
NSA_DH = 64
CMP_LEN = 32
CMP_STRIDE = 16
CMP_HIDDEN = 256
SEL_LEN = 64
SEL_TOP = 8
FORCE_SCORE = 1e4
WINDOW = 256
NSA_Q_COLS = NSA_HEADS * NSA_DH
NSA_KV_COLS = 3 * 2 * NSA_GROUPS * NSA_DH
NSA_GATE_COLS = 3 * NSA_HEADS

DIFF_HEADS = 8
DIFF_DH = 64

D_FF = 2816
ALPHA = (2.0 * DEPTH) ** 0.25
LN_EPS = 1e-5
RMS_EPS = 1e-6

LANES = 128
VMEM_LIMIT = 56 * 1024 * 1024

TA = 256
EVEN_W = 2048
MISC_COL = 640
NSA_Q_COL = 768
NSA_KV_COL = 1280
NSA_CMP_COL = 1792
N_CMP_PAD = 128
MASK_BUCKET = N_BUCKETS
FAR_BUCKET = N_BUCKETS - 1
LOG2E = math.log2(math.e)
ONES_ROWS = 16
CONV_HALO = 16


def _cparams(sem):
    return pltpu.CompilerParams(dimension_semantics=sem, vmem_limit_bytes=VMEM_LIMIT)


def _nt(a, b):
    return lax.dot_general(a, b, (((1,), (1,)), ((), ())), preferred_element_type=F32)


def _mm(a, b):
    return jnp.dot(a, b, preferred_element_type=F32)


def _split3(v):
    hi = v.astype(BF16)
    r1 = v - hi.astype(F32)
    mid = r1.astype(BF16)
    lo = (r1 - mid.astype(F32)).astype(BF16)
    return hi, mid, lo


def _layer_norm_rows(z, g, b):
    mu = jnp.mean(z, axis=-1, keepdims=True)
    zc = z - mu
    var = jnp.mean(zc * zc, axis=-1, keepdims=True)
    return zc * lax.rsqrt(var + LN_EPS) * g + b


def _bias_tail(s, near, diag):
    n = s.shape[0]
    parts = []
    if near is not None:
        if n > 2 * TA:
            parts.append(s[:n - 2 * TA])
        parts.append(s[n - 2 * TA:n - TA] + near)
    elif n > TA:
        parts.append(s[:n - TA])
    parts.append(s[n - TA:] + diag)
    return parts[0] if len(parts) == 1 else jnp.concatenate(parts, axis=0)


class _SoftmaxJob:
    def __init__(self, n_chunks, scores, v_t_chunk, s_buf, finish):
        self.n, self.scores, self.v_t_chunk, self.s_buf, self.finish = n_chunks, scores, v_t_chunk, s_buf, finish
        self.m = self.acc = None

    def score_pass(self):
        s = self.scores()
        self.s_buf[0:self.n * TA, :] = s
        m8 = jnp.max(s.reshape(self.n * TA // 8, 8, s.shape[1]), axis=0)
        self.m = jnp.max(m8, axis=0, keepdims=True)

    def value_step(self, i):
        p = jnp.exp2((self.s_buf[i * TA:(i + 1) * TA, :] - self.m).astype(BF16))
        pv = _mm(self.v_t_chunk(i), p)
        self.acc = pv if self.acc is None else self.acc + pv

    def done(self):
        dv = self.acc.shape[0] - ONES_ROWS
        self.finish(self.acc[:dv], self.acc[dv:dv + 1])


def _run_interleaved(jobs):
    jobs[0].score_pass()
    for k, job in enumerate(jobs):
        for i in range(job.n):
            job.value_step(i)
            if i == 0 and k + 1 < len(jobs):
                jobs[k + 1].score_pass()
        job.done()


def _ada_kernel(c_ref, w_ref, b_ref, o_ref):
    c = c_ref[...]
    ca = c / (1.0 + jnp.exp(-c))
    w = w_ref[0]
    c_hi = ca.astype(BF16)
    c_lo = (ca - c_hi.astype(F32)).astype(BF16)
    w_hi = w.astype(BF16)
    w_lo = (w - w_hi.astype(F32)).astype(BF16)
    o_ref[0] = _mm(c_hi, w_hi) + _mm(c_hi, w_lo) + _mm(c_lo, w_hi) + b_ref[0]


def _ada_mod(c, ada_w, ada_b):
    B, D = c.shape
    n_out = ada_w.shape[-1]
    tn = 1024
    return pl.pallas_call(
        _ada_kernel,
        grid=(DEPTH, n_out // tn),
        in_specs=[
            pl.BlockSpec((B, D), lambda l, j: (0, 0)),
            pl.BlockSpec((1, D, tn), lambda l, j: (l, 0, j)),
            pl.BlockSpec((1, 1, tn), lambda l, j: (l, 0, j)),
        ],
        out_specs=pl.BlockSpec((1, B, tn), lambda l, j: (l, 0, j)),
        out_shape=jax.ShapeDtypeStruct((DEPTH, B, n_out), F32),
        compiler_params=_cparams(("parallel", "parallel")),
        name="ada_mod",
    )(c, ada_w, ada_b.reshape(DEPTH, 1, n_out))


def _expand_kernel(tbl_ref, idx_ref, o_ref, *, rel_far):
    h = pl.program_id(0)
    idx = idx_ref[...]
    acc = jnp.full(idx.shape, NEG_INF, F32)
    base = tbl_ref[FAR_BUCKET, h] if rel_far else 0.0
    for b in range(N_BUCKETS):
        acc = jnp.where(idx == b, (tbl_ref[b, h] - base) * LOG2E, acc)
    o_ref[0] = acc


def _expand_bias(table, idx, rel_far=False):
    R, C = idx.shape
    rt = min(R, 256)
    return pl.pallas_call(
        functools.partial(_expand_kernel, rel_far=rel_far),
        grid=(table.shape[1], R // rt),
        in_specs=[
            pl.BlockSpec(memory_space=pltpu.SMEM),
            pl.BlockSpec((rt, C), lambda h, r: (r, 0)),
        ],
        out_specs=pl.BlockSpec((1, rt, C), lambda h, r: (h, r, 0)),
        out_shape=jax.ShapeDtypeStruct((table.shape[1], R, C), F32),
        compiler_params=_cparams(("parallel", "parallel")),
        name="expand_bias",
    )(table, idx)


def _t5_bucket(dist):
    n = jnp.maximum(dist, 0)
    max_exact = N_BUCKETS // 2
    nf = jnp.maximum(n, 1).astype(jnp.float32)
    large = max_exact + (jnp.log(nf / max_exact) / math.log(MAX_DISTANCE / max_exact)
                         * (N_BUCKETS - max_exact)).astype(jnp.int32)
    large = jnp.minimum(large, N_BUCKETS - 1)
    return jnp.where(n < max_exact, n, large)


def _bias_index_tiles(S):
    a = np.arange(TA)[None, :]
    b = np.arange(TA)[:, None]
    d0 = a - b
    d1 = TA + a - b
    t0 = jnp.where(jnp.asarray(d0 >= 0), _t5_bucket(jnp.asarray(d0)), MASK_BUCKET)
    t1 = _t5_bucket(jnp.asarray(d1))
    t1w = jnp.where(jnp.asarray(d1 < WINDOW), t1, MASK_BUCKET)
    toe_t = jnp.concatenate([t0, t1, t1w], axis=0).astype(jnp.int32)
    t_pos = np.arange(S)[None, :]
    n = np.arange(N_CMP_PAD)[:, None]
    d_cmp = t_pos - (n * CMP_STRIDE + CMP_LEN - 1)
    cmp_t = jnp.where(jnp.asarray(d_cmp >= 0), _t5_bucket(jnp.asarray(d_cmp)), MASK_BUCKET).astype(jnp.int32)
    return toe_t, cmp_t


def _modmm_kernel(x_ref, sc_ref, sh_ref, w_ref, o_ref, *, tn):
    h = (x_ref[...] * (1.0 + sc_ref[0]) + sh_ref[0]).astype(BF16)
    for j in range(w_ref.shape[1] // tn):
        cs = slice(j * tn, (j + 1) * tn)
        o_ref[:, cs] = _mm(h, w_ref[:, cs]).astype(BF16)


def _mod_matmul(x, scale, shift, w, S):
    T, D = x.shape
    N = w.shape[1]
    tm, tn = 512, 512
    per_seq = S // tm
    return pl.pallas_call(
        functools.partial(_modmm_kernel, tn=tn),
        grid=(T // tm,),
        in_specs=[
            pl.BlockSpec((tm, D), lambda i: (i, 0)),
            pl.BlockSpec((1, 1, D), lambda i: (i // per_seq, 0, 0)),
            pl.BlockSpec((1, 1, D), lambda i: (i // per_seq, 0, 0)),
            pl.BlockSpec((D, N), lambda i: (0, 0), pipeline_mode=pl.Buffered(1)),
        ],
        out_specs=pl.BlockSpec((tm, N), lambda i: (i, 0)),
        out_shape=jax.ShapeDtypeStruct((T, N), BF16),
        compiler_params=_cparams(("parallel",)),
        name="mod_matmul",
    )(x, scale, shift, w)


def _mla_prep_kernel(p_ref, qn_ref, kvn_ref, wq_ref, wk_ref, wv_ref, er_ref, ct_ref, st_ref, mr_ref,
                     q_ref, k_ref, v_ref):
    p = p_ref[...]
    ql = p[:, :MLA_Q_RANK].astype(F32)
    kl = p[:, MLA_Q_RANK:MLA_Q_RANK + MLA_KV_RANK].astype(F32)
    misc = p[:, MISC_COL:MISC_COL + LANES].astype(F32)
    c_q = (ql * lax.rsqrt(jnp.mean(ql * ql, axis=-1, keepdims=True) + RMS_EPS) * qn_ref[...]).astype(BF16)
    c_kv = (kl * lax.rsqrt(jnp.mean(kl * kl, axis=-1, keepdims=True) + RMS_EPS) * kvn_ref[...]).astype(BF16)
    ab = _mm(c_q, wq_ref[...])
    ct = ct_ref[...]
    st = st_ref[...]
    half = MLA_HEADS * LANES
    for h in range(MLA_HEADS):
        lo = h * LANES
        q_ref[:, lo:lo + LANES] = (ab[:, lo:lo + LANES] * ct + ab[:, half + lo:half + lo + LANES] * st).astype(BF16)
    k_rope = _mm((misc * mr_ref[...]).astype(BF16), er_ref[...])
    k_ref[...] = (_mm(c_kv, wk_ref[...]) + k_rope).astype(BF16)
    v_ref[...] = _mm(c_kv, wv_ref[...]).astype(BF16)


def _mla_prep(proj, qn, kvn, wq, wk, wv, erope, ctab, stab, mrope, S):
    T = proj.shape[0]
    tm = 512
    per_seq = S // tm
    hw = MLA_HEADS * LANES
    const = lambda i: (0, 0)
    seq = lambda i: (i % per_seq, 0)
    return pl.pallas_call(
        _mla_prep_kernel,
        grid=(T // tm,),
        in_specs=[
            pl.BlockSpec((tm, NSA_Q_COL), lambda i: (i, 0)),
            pl.BlockSpec((1, MLA_Q_RANK), const),
            pl.BlockSpec((1, MLA_KV_RANK), const),
            pl.BlockSpec((MLA_Q_RANK, 2 * hw), const),
            pl.BlockSpec((MLA_KV_RANK, hw), const),
            pl.BlockSpec((MLA_KV_RANK, MLA_HEADS * MLA_V), const),
            pl.BlockSpec((LANES, hw), const),
            pl.BlockSpec((tm, LANES), seq),
            pl.BlockSpec((tm, LANES), seq),
            pl.BlockSpec((tm, LANES), seq),
        ],
        out_specs=[
            pl.BlockSpec((tm, hw), lambda i: (i, 0)),
            pl.BlockSpec((tm, hw), lambda i: (i, 0)),
            pl.BlockSpec((tm, MLA_HEADS * MLA_V), lambda i: (i, 0)),
        ],
        out_shape=[
            jax.ShapeDtypeStruct((T, hw), BF16),
            jax.ShapeDtypeStruct((T, hw), BF16),
            jax.ShapeDtypeStruct((T, MLA_HEADS * MLA_V), BF16),
        ],
        compiler_params=_cparams(("parallel",)),
        name="mla_prep",
    )(proj, qn, kvn, wq, wk, wv, erope, ctab, stab, mrope)


def _mla_attn_kernel(q_ref, k_ref, v_ref, cm_ref, id_ref, o_ref, vt_ref, s_ref, *, n_tiles):
    hv = MLA_V + ONES_ROWS
    vt = _nt(id_ref[...], v_ref[...]).astype(BF16)
    ones = jnp.ones((ONES_ROWS, vt.shape[1]), BF16)
    for hh in range(2):
        vt_ref[hh * hv:hh * hv + MLA_V, :] = vt[hh * MLA_V:(hh + 1) * MLA_V]
        vt_ref[hh * hv + MLA_V:(hh + 1) * hv, :] = ones
    outs = {}

    def make_job(qi, hh):
        rows = slice(qi * TA, (qi + 1) * TA)
        cs = slice(hh * LANES, (hh + 1) * LANES)

        def score():
            s = _nt(k_ref[0:(qi + 1) * TA, cs], q_ref[rows, cs])
            return _bias_tail(s, None, cm_ref[...])

        def finish(acc, l):
            outs[hh] = acc * (1.0 / l)
            if hh == 1:
                o_ref[rows, :] = jnp.concatenate([outs[0], outs[1]], axis=0).T.astype(BF16)

        return _SoftmaxJob(qi + 1, score, lambda c: vt_ref[hh * hv:(hh + 1) * hv, c * TA:(c + 1) * TA],
                           s_ref.at[hh], finish)

    _run_interleaved([make_job(qi, hh) for qi in range(n_tiles) for hh in range(2)])


def _mla_attention(q, k, v, cmask_t, ident, B, S):
    pairs = MLA_HEADS // 2
    return pl.pallas_call(
        functools.partial(_mla_attn_kernel, n_tiles=S // TA),
        grid=(B, pairs),
        in_specs=[
            pl.BlockSpec((S, 2 * LANES), lambda b, hp: (b, hp)),
            pl.BlockSpec((S, 2 * LANES), lambda b, hp: (b, hp)),
            pl.BlockSpec((S, LANES), lambda b, hp: (b, hp)),
            pl.BlockSpec((TA, TA), lambda b, hp: (0, 0)),
            pl.BlockSpec((LANES, LANES), lambda b, hp: (0, 0)),
        ],
        out_specs=pl.BlockSpec((S, LANES), lambda b, hp: (b, hp)),
        out_shape=jax.ShapeDtypeStruct((B * S, MLA_HEADS * MLA_V), BF16),
        scratch_shapes=[pltpu.VMEM((2 * (MLA_V + ONES_ROWS), S), BF16), pltpu.VMEM((2, S, TA), F32)],
        compiler_params=_cparams(("parallel", "parallel")),
        name="mla_attention",
    )(q, k, v, cmask_t, ident)


def _compress_kernel(r_ref, pe_ref, w1_ref, w2_ref, w2t_ref, o_ref, ot_ref):
    half = CMP_STRIDE * NSA_DH
    r = r_ref[0, 0].astype(F32)
    pe = pe_ref[0]
    x_lo = (r + pe[:, :half]).astype(BF16)
    x_hi = (r + pe[:, half:]).astype(BF16)
    y_lo = _mm(x_lo, w1_ref[0, :half, :])
    y_hi = _mm(x_hi, w1_ref[0, half:, :])
    y = y_lo + pltpu.roll(y_hi, N_CMP_PAD - 1, 0)
    hid = jax.nn.gelu(y).astype(BF16)
    o_ref[0, 0] = _mm(hid, w2_ref[0]).astype(BF16)
    ot_ref[0, 0] = _nt(w2t_ref[0], hid).astype(BF16)


def _compress(r, pe, w1, w2_rep, w2_t, B):
    rep = NSA_HPG * NSA_DH
    return pl.pallas_call(
        _compress_kernel,
        grid=(4, B),
        in_specs=[
            pl.BlockSpec((1, 1, N_CMP_PAD, CMP_STRIDE * NSA_DH), lambda c, b: (c, b, 0, 0)),
            pl.BlockSpec((1, 1, CMP_LEN * NSA_DH), lambda c, b: (c // 2, 0, 0)),
            pl.BlockSpec((1, CMP_LEN * NSA_DH, CMP_HIDDEN), lambda c, b: (c // 2, 0, 0)),
            pl.BlockSpec((1, CMP_HIDDEN, rep), lambda c, b: (c // 2, 0, 0)),
            pl.BlockSpec((1, NSA_DH, CMP_HIDDEN), lambda c, b: (c // 2, 0, 0)),
        ],
        out_specs=[
            pl.BlockSpec((1, 1, N_CMP_PAD, rep), lambda c, b: (c, b, 0, 0)),
            pl.BlockSpec((1, 1, NSA_DH, N_CMP_PAD), lambda c, b: (c, b, 0, 0)),
        ],
        out_shape=[
            jax.ShapeDtypeStruct((4, B, N_CMP_PAD, rep), BF16),
            jax.ShapeDtypeStruct((4, B, NSA_DH, N_CMP_PAD), BF16),
        ],
        compiler_params=_cparams(("parallel", "parallel")),
        name="nsa_compress",
    )(r, pe, w1, w2_rep, w2_t)


def _nsa_kernel(q_ref, kv_ref, misc_ref, kc_ref, vct_ref, bc_ref, toe_ref, erep_ref, esel_ref, ovl_ref, o_ref,
                sk_ref, wk_ref, svt_ref, wvt_ref, ssel_ref, swin_ref, *, n_tiles):
    g = pl.program_id(1)
    J = NSA_HPG
    W = J * NSA_DH
    QW = J * TA
    n_blk = n_tiles * TA // SEL_LEN
    kv = kv_ref[...]
    sk_ref[...] = _mm(kv, erep_ref[0]).astype(BF16)
    wk_ref[...] = _mm(kv, erep_ref[1]).astype(BF16)
    ones = jnp.ones((ONES_ROWS, kv.shape[0]), BF16)
    for ref, slot in ((svt_ref, 0), (wvt_ref, 1)):
        ref[0:NSA_DH, :] = _nt(esel_ref[slot], kv).astype(BF16)
        ref[NSA_DH:, :] = ones

    head_of_lane = lax.broadcasted_iota(jnp.int32, (TA, W), 1) >> 6
    blk = lax.broadcasted_iota(jnp.int32, (n_blk, TA), 0)
    qpos = lax.broadcasted_iota(jnp.int32, (n_blk, TA), 1)
    g_is0 = g == 0

    def make_jobs(qi):
        rows = slice(qi * TA, (qi + 1) * TA)
        first = max(qi - 1, 0)
        st = {}

        def q4():
            if "q4" not in st:
                q = q_ref[rows, :]
                zero = jnp.zeros_like(q)
                st["q4"] = jnp.concatenate([jnp.where(head_of_lane == j, q, zero) for j in range(J)], axis=0)
            return st["q4"]

        def select():
            s = _nt(kc_ref[0, 0], q4()) + bc_ref[0, qi]
            e = jnp.exp2(s - jnp.max(s, axis=0, keepdims=True))
            p = e * (1.0 / jnp.sum(e, axis=0, keepdims=True))
            if qi * TA < CMP_LEN - 1:
                tq = (lax.broadcasted_iota(jnp.int32, p.shape, 1) & (TA - 1)) + qi * TA
                p = jnp.where(tq >= CMP_LEN - 1, p, 0.0)
            st["o_cmp"] = _mm(vct_ref[0, 0], p.astype(BF16))
            p_sum = p[:, 0:TA]
            for j in range(1, J):
                p_sum = p_sum + p[:, j * TA:(j + 1) * TA]
            hi, mid, lo = _split3(p_sum)
            imp = _mm(ovl_ref[...], hi) + _mm(ovl_ref[...], mid) + _mm(ovl_ref[...], lo)
            cur = (qpos + qi * TA) >> 6
            forced = (blk == 0) | (blk == cur) | (blk == cur - 1)
            score = jnp.where(forced, FORCE_SCORE, imp)
            score = jnp.where(blk <= cur, score, -1.0)
            rank = jnp.zeros((n_blk, TA), F32)
            for mp in range(n_blk):
                r = score[mp:mp + 1, :]
                beats = (r > score) | ((r == score) & (blk > mp))
                rank = rank + jnp.where(beats, 1.0, 0.0)
            pen = jnp.where(rank < SEL_TOP, 0.0, NEG_INF)
            return jnp.concatenate([pen] * J, axis=1)

        def score_sel():
            n_keys = (qi + 1) * TA
            pen4 = select()
            pen_rows = jnp.concatenate(
                [jnp.broadcast_to(pen4[m:m + 1, :], (SEL_LEN, QW)) for m in range(n_keys // SEL_LEN)], axis=0)
            s = _nt(sk_ref[0:n_keys, :], q4()) + pen_rows
            return _bias_tail(s, toe_ref[0, 1] if qi > 0 else None, toe_ref[0, 0])

        def score_win():
            s = _nt(wk_ref[first * TA:(qi + 1) * TA, :], q4())
            return _bias_tail(s, toe_ref[0, 2] if qi > 0 else None, toe_ref[0, 0])

        def finish_win(acc, l):
            st["o_win"] = acc * (1.0 / l)

        def finish_sel(acc, l):
            o_slc = acc * (1.0 / l)
            gl = misc_ref[rows, :].astype(F32)
            sig_t = (1.0 / (1.0 + jnp.exp(-gl))).T
            mixes = []
            for j in range(J):
                def gate(br, j=j):
                    r0 = 2 * MLA_ROPE + j * 3 + br
                    r1 = r0 + J * 3
                    return jnp.where(g_is0, sig_t[r0:r0 + 1, :], sig_t[r1:r1 + 1, :])

                cs = slice(j * TA, (j + 1) * TA)
                mixes.append(gate(0) * st["o_cmp"][:, cs] + gate(1) * o_slc[:, cs] + gate(2) * st["o_win"][:, cs])
            o_ref[rows, :] = jnp.concatenate(mixes, axis=0).T.astype(BF16)

        win = _SoftmaxJob(qi + 1 - first, score_win, lambda c: wvt_ref[:, (first + c) * TA:(first + c + 1) * TA],
                          swin_ref, finish_win)
        sel = _SoftmaxJob(qi + 1, score_sel, lambda c: svt_ref[:, c * TA:(c + 1) * TA], ssel_ref, finish_sel)
        return [win, sel]

    _run_interleaved([job for qi in range(n_tiles) for job in make_jobs(qi)])


def _nsa_attention(proj, kcmp, vcmp_t, bias_cmp_t, toe_g, erep, esel, ovl_t, B, S):
    J = NSA_HPG
    W = J * NSA_DH
    n_tiles = S // TA
    n_blk = S // SEL_LEN
    q_blk = NSA_Q_COL // W
    kv_blk = NSA_KV_COL // W
    misc_blk = MISC_COL // LANES
    return pl.pallas_call(
        functools.partial(_nsa_kernel, n_tiles=n_tiles),
        grid=(B, NSA_GROUPS),
        in_specs=[
            pl.BlockSpec((S, W), lambda b, g: (b, q_blk + g)),
            pl.BlockSpec((S, W), lambda b, g: (b, kv_blk + g)),
            pl.BlockSpec((S, LANES), lambda b, g: (b, misc_blk)),
            pl.BlockSpec((1, 1, N_CMP_PAD, W), lambda b, g: (g, b, 0, 0)),
            pl.BlockSpec((1, 1, NSA_DH, N_CMP_PAD), lambda b, g: (2 + g, b, 0, 0)),
            pl.BlockSpec((1, n_tiles, N_CMP_PAD, J * TA), lambda b, g: (g, 0, 0, 0)),
            pl.BlockSpec((1, 3, TA, J * TA), lambda b, g: (g, 0, 0, 0)),
            pl.BlockSpec((2, W, W), lambda b, g: (0, 0, 0)),
            pl.BlockSpec((2, NSA_DH, W), lambda b, g: (0, 0, 0)),
            pl.BlockSpec((n_blk, N_CMP_PAD), lambda b, g: (0, 0)),
        ],
        out_specs=pl.BlockSpec((S, W), lambda b, g: (b, g)),
        out_shape=jax.ShapeDtypeStruct((B * S, NSA_HEADS * NSA_DH), BF16),
        scratch_shapes=[pltpu.VMEM((S, W), BF16), pltpu.VMEM((S, W), BF16),
                        pltpu.VMEM((NSA_DH + ONES_ROWS, S), BF16), pltpu.VMEM((NSA_DH + ONES_ROWS, S), BF16),
                        pltpu.VMEM((S, J * TA), F32), pltpu.VMEM((2 * TA, J * TA), F32)],
        compiler_params=_cparams(("parallel", "parallel")),
        name="nsa_attention",
    )(proj, proj, proj, kcmp, vcmp_t, bias_cmp_t, toe_g, erep, esel, ovl_t)


def _diff_kernel(q_ref, k_ref, v_ref, toe_ref, lam_ref, sub_ref, id_ref, o_ref, vt_ref, s_ref, *, n_tiles, lam_init):
    hd = DIFF_DH
    lane = lax.broadcasted_iota(jnp.int32, (TA, 2 * hd), 1)
    lv = lam_ref[...]
    lam = (jnp.exp(jnp.sum(lv[0:1] * lv[1:2], axis=-1, keepdims=True))
           - jnp.exp(jnp.sum(lv[2:3] * lv[3:4], axis=-1, keepdims=True)) + lam_init)
    vt_ref[0:2 * hd, :] = _nt(id_ref[...], v_ref[...]).astype(BF16)
    vt_ref[2 * hd:, :] = jnp.ones((ONES_ROWS, vt_ref.shape[1]), BF16)

    def both(t):
        return jnp.concatenate([t, t], axis=1)

    def make_job(qi):
        rows = slice(qi * TA, (qi + 1) * TA)

        def score():
            q = q_ref[rows, :]
            zero = jnp.zeros_like(q)
            q2 = jnp.concatenate([jnp.where(lane < hd, q, zero), jnp.where(lane >= hd, q, zero)], axis=0)
            s = _nt(k_ref[0:(qi + 1) * TA, :], q2)
            near = both(toe_ref[0, TA:2 * TA, :]) if qi > 0 else None
            return _bias_tail(s, near, both(toe_ref[0, 0:TA, :]))

        def finish(acc, l):
            o2 = acc * (1.0 / l)
            o = (o2[:, 0:TA] - lam * o2[:, TA:2 * TA]).T
            o = o * lax.rsqrt(jnp.mean(o * o, axis=-1, keepdims=True) + RMS_EPS) * sub_ref[...]
            o_ref[rows, :] = (o * (1.0 - lam_init)).astype(BF16)

        return _SoftmaxJob(qi + 1, score, lambda c: vt_ref[:, c * TA:(c + 1) * TA], s_ref.at[qi % 2], finish)

    _run_interleaved([make_job(qi) for qi in range(n_tiles)])


def _diff_attention(proj, toe_t, lam_vec, subln, ident, layer_idx, B, S):
    H = DIFF_HEADS
    hw = 2 * DIFF_DH
    lam_init = 0.8 - 0.6 * math.exp(-0.3 * layer_idx)
    return pl.pallas_call(
        functools.partial(_diff_kernel, n_tiles=S // TA, lam_init=lam_init),
        grid=(B, H),
        in_specs=[
            pl.BlockSpec((S, hw), lambda b, h: (b, h)),
            pl.BlockSpec((S, hw), lambda b, h: (b, H + h)),
            pl.BlockSpec((S, hw), lambda b, h: (b, 2 * H + h)),
            pl.BlockSpec((1, 3 * TA, TA), lambda b, h: (h, 0, 0)),
            pl.BlockSpec((4, DIFF_DH), lambda b, h: (0, 0)),
            pl.BlockSpec((1, hw), lambda b, h: (0, 0)),
            pl.BlockSpec((hw, hw), lambda b, h: (0, 0)),
        ],
        out_specs=pl.BlockSpec((S, hw), lambda b, h: (b, h)),
        out_shape=jax.ShapeDtypeStruct((B * S, H * hw), BF16),
        scratch_shapes=[pltpu.VMEM((hw + ONES_ROWS, S), BF16), pltpu.VMEM((2, S, 2 * TA), F32)],
        compiler_params=_cparams(("parallel", "parallel")),
        name="diff_attention",
    )(proj, proj, proj, toe_t, lam_vec, subln.reshape(1, hw), ident)


def _tail_kernel(oa_ref, ob_ref, wo_ref, x_ref, g1_ref, lg1_ref, lb1_ref, sc_ref, sh_ref, g2_ref,
                 wu_ref, wg_ref, cw_ref, cb_ref, wd_ref, lg2_ref, lb2_ref, o_ref, hprev_ref, act_ref,
                 *, per_seq, tf):
    i = pl.program_id(0)
    half = oa_ref.shape[1]
    tm = x_ref.shape[0]
    y = _mm(oa_ref[...], wo_ref[:half, :]) + _mm(ob_ref[...], wo_ref[half:, :])
    x1 = _layer_norm_rows(ALPHA * x_ref[...] + (1.0 + g1_ref[0]) * y, lg1_ref[...], lb1_ref[...])
    o_ref[...] = x1
    h = (x1 * (1.0 + sc_ref[0]) + sh_ref[0]).astype(BF16)

    @pl.when(i % per_seq == 0)
    def _():
        hprev_ref[...] = jnp.zeros(hprev_ref.shape, BF16)

    h_ext = jnp.concatenate([hprev_ref[...], h], axis=0)
    hprev_ref[...] = h[tm - CONV_HALO:, :]
    for f in range(D_FF // tf):
        cs = slice(f * tf, (f + 1) * tf)
        u = _mm(h, wu_ref[:, cs])
        ge = _mm(h_ext, wg_ref[:, cs])
        cw = cw_ref[:, cs]
        a = (cw[2:3] * ge[CONV_HALO:CONV_HALO + tm] + cw[1:2] * ge[CONV_HALO - 1:CONV_HALO - 1 + tm]
             + cw[0:1] * ge[CONV_HALO - 2:CONV_HALO - 2 + tm] + cb_ref[:, cs])
        act_ref[:, cs] = (a / (1.0 + jnp.exp(-a)) * u).astype(BF16)
    y2 = _mm(act_ref[...], wd_ref[...])
    z = ALPHA * o_ref[...] + (1.0 + g2_ref[0]) * y2
    o_ref[...] = _layer_norm_rows(z, lg2_ref[...], lb2_ref[...])


def _layer_tail(o_a, o_b, blk_a, blk_b, w_o, x, g1, ln1_g, ln1_b, scale, shift, g2, w_up, w_gate, conv_w, conv_b,
                w_down, ln2_g, ln2_b, S):
    T, D = x.shape
    tm, tf = 512, 256
    half = w_o.shape[0] // 2
    per_seq = S // tm
    once = pl.Buffered(1)

    def const(shape):
        return pl.BlockSpec(shape, lambda i: (0,) * len(shape), pipeline_mode=once)

    mod_spec = pl.BlockSpec((1, 1, D), lambda i: (i // per_seq, 0, 0))
    return pl.pallas_call(
        functools.partial(_tail_kernel, per_seq=per_seq, tf=tf),
        grid=(T // tm,),
        in_specs=[
            pl.BlockSpec((tm, half), lambda i: (i, blk_a)),
            pl.BlockSpec((tm, half), lambda i: (i, blk_b)),
            const((2 * half, D)),
            pl.BlockSpec((tm, D), lambda i: (i, 0)),
            mod_spec, const((1, D)), const((1, D)),
            mod_spec, mod_spec, mod_spec,
            const((D, D_FF)), const((D, D_FF)), const((3, D_FF)), const((1, D_FF)), const((D_FF, D)),
            const((1, D)), const((1, D)),
        ],
        out_specs=pl.BlockSpec((tm, D), lambda i: (i, 0)),
        out_shape=jax.ShapeDtypeStruct((T, D), F32),
        scratch_shapes=[pltpu.VMEM((CONV_HALO, D), BF16), pltpu.VMEM((tm, D_FF), BF16)],
        compiler_params=_cparams(("arbitrary",)),
        name="layer_tail",
    )(o_a, o_b, w_o, x, g1, ln1_g, ln1_b, scale, shift, g2, w_up, w_gate, conv_w, conv_b, w_down, ln2_g, ln2_b)


def _even_in_columns():
    src = np.zeros(EVEN_W, np.int32)
    mul = np.zeros(EVEN_W, np.float32)

    def put(dst, cols, scale=1.0):
        cols = np.asarray(cols)
        src[dst:dst + len(cols)] = cols
        mul[dst:dst + len(cols)] = scale

    put(0, np.arange(MLA_Q_RANK + MLA_KV_RANK))
    rope0 = MLA_Q_RANK + MLA_KV_RANK
    half = MLA_ROPE // 2
    put(MISC_COL, rope0 + np.arange(MLA_ROPE))
    put(MISC_COL + MLA_ROPE, rope0 + half + np.arange(half), -1.0)
    put(MISC_COL + MLA_ROPE + half, rope0 + np.arange(half), 1.0)
    nsa0 = MLA_COLS
    put(MISC_COL + 2 * MLA_ROPE, nsa0 + NSA_Q_COLS + NSA_KV_COLS + np.arange(NSA_GATE_COLS))
    put(NSA_Q_COL, nsa0 + np.arange(NSA_Q_COLS), NSA_DH ** -0.5 * LOG2E)
    kv0 = nsa0 + NSA_Q_COLS

    def chunk(branch, kv, g):
        return kv0 + ((branch * 2 + kv) * NSA_GROUPS + g) * NSA_DH + np.arange(NSA_DH)

    for g in range(NSA_GROUPS):
        base = NSA_KV_COL + g * 4 * NSA_DH
        for slot, (branch, kv) in enumerate([(1, 0), (2, 0), (1, 1), (2, 1)]):
            put(base + slot * NSA_DH, chunk(branch, kv, g))
    put(NSA_CMP_COL, kv0 + np.arange(2 * NSA_GROUPS * NSA_DH))
    return src, mul


def _mla_up_columns():
    hw = MLA_HEADS * LANES
    qd = MLA_NOPE + MLA_ROPE
    half = MLA_ROPE // 2
    src = np.zeros(2 * hw, np.int32)
    mul = np.zeros(2 * hw, np.float32)
    for h in range(MLA_HEADS):
        src[h * LANES:h * LANES + qd] = h * qd + np.arange(qd)
        mul[h * LANES:h * LANES + qd] = 1.0
        r = hw + h * LANES + MLA_NOPE
        src[r:r + half] = h * qd + MLA_NOPE + half + np.arange(half)
        mul[r:r + half] = -1.0
        src[r + half:r + MLA_ROPE] = h * qd + MLA_NOPE + np.arange(half)
        mul[r + half:r + MLA_ROPE] = 1.0
    ksrc = np.zeros(hw, np.int32)
    kmul = np.zeros(hw, np.float32)
    vsrc = np.zeros(MLA_HEADS * MLA_V, np.int32)
    for h in range(MLA_HEADS):
        ksrc[h * LANES:h * LANES + MLA_NOPE] = h * (MLA_NOPE + MLA_V) + np.arange(MLA_NOPE)
        kmul[h * LANES:h * LANES + MLA_NOPE] = 1.0
        vsrc[h * MLA_V:(h + 1) * MLA_V] = h * (MLA_NOPE + MLA_V) + MLA_NOPE + np.arange(MLA_V)
    return src, mul, ksrc, kmul, vsrc


def _routing_constants(S):
    hw = MLA_HEADS * LANES
    erope = np.zeros((LANES, hw), np.float32)
    for h in range(MLA_HEADS):
        for i in range(MLA_ROPE):
            erope[i, h * LANES + MLA_NOPE + i] = 1.0
            erope[MLA_ROPE + i, h * LANES + MLA_NOPE + i] = 1.0
    W = NSA_HPG * NSA_DH
    erep = np.zeros((2, W, W), np.float32)
    for slot in range(2):
        for d in range(NSA_DH):
            for j in range(NSA_HPG):
                erep[slot, slot * NSA_DH + d, j * NSA_DH + d] = 1.0
    esel = np.zeros((2, NSA_DH, W), np.float32)
    for slot in range(2):
        esel[slot, np.arange(NSA_DH), (2 + slot) * NSA_DH + np.arange(NSA_DH)] = 1.0
    n_slc = S // SEL_LEN
    starts = np.arange(N_CMP_PAD) * CMP_STRIDE
    jb = np.arange(n_slc)
    ovl = ((starts[:, None] < (jb[None, :] + 1) * SEL_LEN)
           & (starts[:, None] + CMP_LEN > jb[None, :] * SEL_LEN)).astype(np.float32)
    ovl[(S - CMP_LEN) // CMP_STRIDE + 1:, :] = 0.0
    cmask_t = np.where(np.arange(TA)[None, :] >= np.arange(TA)[:, None], 0.0, NEG_INF).astype(np.float32)
    return erope, erep, esel, np.ascontiguousarray(ovl.T), cmask_t


def _rope_tables(S):
    inv = 1.0 / (ROPE_BASE ** (jnp.arange(0, MLA_ROPE, 2, dtype=jnp.float32) / MLA_ROPE))
    ang = jnp.arange(S, dtype=jnp.float32)[:, None] * inv[None, :]
    cos, sin = jnp.cos(ang), jnp.sin(ang)
    cos2 = jnp.concatenate([cos, cos], axis=-1)
    sin2 = jnp.concatenate([sin, sin], axis=-1)
    scale = (MLA_NOPE + MLA_ROPE) ** -0.5 * LOG2E
    z32 = jnp.zeros((S, LANES - MLA_NOPE - MLA_ROPE), F32)
    ctab = jnp.concatenate([jnp.full((S, MLA_NOPE), scale, F32), cos2 * scale, z32], axis=-1)
    stab = jnp.concatenate([jnp.zeros((S, MLA_NOPE), F32), sin2 * scale, z32], axis=-1)
    mrope = jnp.concatenate([cos2, sin2, jnp.zeros((S, LANES - 2 * MLA_ROPE), F32)], axis=-1)
    return ctab, stab, mrope


def kernel(x, c, rel_bias, ev_w_in, mla_q_norm, mla_kv_norm, mla_w_uq, mla_w_ukv, nsa_cmp_pe, nsa_cmp_w1,
           nsa_cmp_w2, ev_w_o, od_w_in, diff_lambda, diff_subln, od_w_o, ada_w, ada_b, ln_g, ln_b, ffn_w_up,
           ffn_w_gate, ffn_conv_w, ffn_conv_b, ffn_w_down):
    B, S, D = x.shape
    assert D == D_MODEL and S % TA == 0 and S // CMP_STRIDE == N_CMP_PAD and S // SEL_LEN <= LANES
    far_np = np.arange(TA + 1, max(S, TA + 2))
    far_bucket = 16 + (np.log(far_np.astype(np.float32) / 16) / math.log(MAX_DISTANCE / 16) * 16).astype(np.int32)
    assert far_bucket.min() >= FAR_BUCKET
    T = B * S
    xf = x.reshape(T, D)

    mod = _ada_mod(c, ada_w, ada_b)
    toe_t_idx, cmp_t_idx = _bias_index_tiles(S)
    toe_t = _expand_bias(rel_bias, toe_t_idx, rel_far=True)
    bias_cmp_t = _expand_bias(rel_bias, cmp_t_idx)
    G, J, NQ = NSA_GROUPS, NSA_HPG, S // TA
    toe_g = toe_t.reshape(G, J, 3, TA, TA).transpose(0, 2, 3, 1, 4).reshape(G, 3, TA, J * TA)
    bias_cmp_t = bias_cmp_t.reshape(G, J, N_CMP_PAD, NQ, TA).transpose(0, 3, 2, 1, 4).reshape(G, NQ, N_CMP_PAD, J * TA)
    erope, erep, esel, ovl_t, cmask_t = _routing_constants(S)
    ident = jnp.eye(LANES, dtype=BF16)
    ctab, stab, mrope = _rope_tables(S)
    ev_src, ev_mul = _even_in_columns()
    q_src, q_mul, k_src, k_mul, v_src = _mla_up_columns()

    def bf(a):
        return jnp.asarray(a).astype(BF16)

    for l in range(DEPTH):
        sh1, sc1, g1, sh2, sc2, g2 = [mod[l, :, k * D:(k + 1) * D].reshape(B, 1, D) for k in range(6)]
        i = l // 2
        if l % 2 == 0:
            w_in = bf(ev_w_in[i][:, ev_src] * ev_mul)
            proj = _mod_matmul(xf, sc1, sh1, w_in, S)
            wq = bf(mla_w_uq[i][:, q_src] * q_mul)
            wk = bf(mla_w_ukv[i][:, k_src] * k_mul)
            wv = bf(mla_w_ukv[i][:, v_src])
            q_m, k_m, v_m = _mla_prep(proj, mla_q_norm[i].reshape(1, -1), mla_kv_norm[i].reshape(1, -1),
                                      wq, wk, wv, bf(erope), ctab, stab, mrope, S)
            o_1 = _mla_attention(q_m, k_m, v_m, jnp.asarray(cmask_t), ident, B, S)
            r = proj[:, NSA_CMP_COL:].reshape(B, S, 4, NSA_DH).transpose(2, 0, 1, 3)
            r = r.reshape(4, B, N_CMP_PAD, CMP_STRIDE * NSA_DH)
            w2_rep = jnp.tile(nsa_cmp_w2[i], (1, 1, NSA_HPG))
            kcmp, vcmp_t = _compress(r, nsa_cmp_pe[i].reshape(2, 1, CMP_LEN * NSA_DH), bf(nsa_cmp_w1[i]), bf(w2_rep),
                                     bf(nsa_cmp_w2[i].transpose(0, 2, 1)), B)
            o_2 = _nsa_attention(proj, kcmp, vcmp_t, bias_cmp_t, toe_g, bf(erep), bf(esel), bf(ovl_t), B, S)
            blk_2, w_o = 0, ev_w_o[i]
        else:
            w_in = od_w_in[i].at[:, :DIFF_HEADS * 2 * DIFF_DH].multiply(DIFF_DH ** -0.5 * LOG2E)
            proj = _mod_matmul(xf, sc1, sh1, bf(w_in), S)
            o_1 = _diff_attention(proj, toe_t, diff_lambda[i], diff_subln[i], ident, l, B, S)
            o_2, blk_2, w_o = o_1, 1, od_w_o[i]
        xf = _layer_tail(o_1, o_2, 0, blk_2, bf(w_o), xf, g1, ln_g[l, 0:1], ln_b[l, 0:1], sc2, sh2, g2,
                         bf(ffn_w_up[l]), bf(ffn_w_gate[l]), ffn_conv_w[l], ffn_conv_b[l].reshape(1, D_FF),
                         bf(ffn_w_down[l]), ln_g[l, 1:2], ln_b[l, 1:2], S)
    return xf.reshape(B, S, D)
```

```python
import functools
import math

import numpy as np
import jax
import jax.numpy as jnp
from jax import lax
from jax.experimental import pallas as pl
from jax.experimental.pallas import tpu as pltpu

F32 = jnp.float32
BF16 = jnp.bfloat16

D_MODEL = 1024
DEPTH = 4
N_BUCKETS = 32
MAX_DISTANCE = 128
NEG_INF = -1e30

MLA_HEADS = 8
MLA_Q_RANK = 384
MLA_KV_RANK = 256
MLA_NOPE = 64
MLA_ROPE = 32
MLA_V = 64
ROPE_BASE = 10000.0
MLA_COLS = MLA_Q_RANK + MLA_KV_RANK + MLA_ROPE

NSA_HEADS = 8
NSA_GROUPS = 2
NSA_HPG = NSA_HEADS // NSA_GROUPS
NSA_DH = 64
CMP_LEN = 32
CMP_STRIDE = 16
CMP_HIDDEN = 256
SEL_LEN = 64
SEL_TOP = 8
FORCE_SCORE = 1e4
WINDOW = 256
NSA_Q_COLS = NSA_HEADS * NSA_DH
NSA_KV_COLS = 3 * 2 * NSA_GROUPS * NSA_DH
NSA_GATE_COLS = 3 * NSA_HEADS

DIFF_HEADS = 8
DIFF_DH = 64

D_FF = 2816
ALPHA = (2.0 * DEPTH) ** 0.25
LN_EPS = 1e-5
RMS_EPS = 1e-6

LANES = 128
VMEM_LIMIT = 56 * 1024 * 1024

TA = 256
EVEN_W = 2048
MISC_COL = 640
NSA_Q_COL = 768
NSA_KV_COL = 1280
NSA_CMP_COL = 1792
N_CMP_PAD = 128
MASK_BUCKET = N_BUCKETS
FAR_BUCKET = N_BUCKETS - 1
LOG2E = math.log2(math.e)
ONES_ROWS = 16
CONV_HALO = 16


def _cparams(sem):
    return pltpu.CompilerParams(dimension_semantics=sem, vmem_limit_bytes=VMEM_LIMIT)


def _nt(a, b):
    return lax.dot_general(a, b, (((1,), (1,)), ((), ())), preferred_element_type=F32)


def _mm(a, b):
    return jnp.dot(a, b, preferred_element_type=F32)


def _split3(v):
    hi = v.astype(BF16)
    r1 = v - hi.astype(F32)
    mid = r1.astype(BF16)
    lo = (r1 - mid.astype(F32)).astype(BF16)
    return hi, mid, lo


def _layer_norm_rows(z, g, b):
    mu = jnp.mean(z, axis=-1, keepdims=True)
    zc = z - mu
    var = jnp.mean(zc * zc, axis=-1, keepdims=True)
    return zc * lax.rsqrt(var + LN_EPS) * g + b


def _bias_tail(s, near, diag):
    n = s.shape[0]
    parts = []
    if near is not None:
        if n > 2 * TA:
            parts.append(s[:n - 2 * TA])
        parts.append(s[n - 2 * TA:n - TA] + near)
    elif n > TA:
        parts.append(s[:n - TA])
    parts.append(s[n - TA:] + diag)
    return parts[0] if len(parts) == 1 else jnp.concatenate(parts, axis=0)


class _SoftmaxJob:
    def __init__(self, n_chunks, scores, v_t_chunk, s_buf, finish):
        self.n, self.scores, self.v_t_chunk, self.s_buf, self.finish = n_chunks, scores, v_t_chunk, s_buf, finish
        self.m = self.acc = None

    def score_pass(self):
        s = self.scores()
        self.s_buf[0:self.n * TA, :] = s
        m8 = jnp.max(s.reshape(self.n * TA // 8, 8, s.shape[1]), axis=0)
        self.m = jnp.max(m8, axis=0, keepdims=True)

    def value_step(self, i):
        p = jnp.exp2((self.s_buf[i * TA:(i + 1) * TA, :] - self.m).astype(BF16))
        pv = _mm(self.v_t_chunk(i), p)
        self.acc = pv if self.acc is None else self.acc + pv

    def done(self):
        dv = self.acc.shape[0] - ONES_ROWS
        self.finish(self.acc[:dv], self.acc[dv:dv + 1])


def _run_interleaved(jobs):
    jobs[0].score_pass()
    for k, job in enumerate(jobs):
        if k + 1 < len(jobs):
            jobs[k + 1].score_pass()
        for i in range(job.n):
            job.value_step(i)
        job.done()


def _ada_kernel(c_ref, w_ref, b_ref, o_ref):
    c = c_ref[...]
    ca = c / (1.0 + jnp.exp(-c))
    w = w_ref[0]
    c_hi = ca.astype(BF16)
    c_lo = (ca - c_hi.astype(F32)).astype(BF16)
    w_hi = w.astype(BF16)
    w_lo = (w - w_hi.astype(F32)).astype(BF16)
    o_ref[0] = _mm(c_hi, w_hi) + _mm(c_hi, w_lo) + _mm(c_lo, w_hi) + b_ref[0]


def _ada_mod(c, ada_w, ada_b):
    B, D = c.shape
    n_out = ada_w.shape[-1]
    tn = 1024
    return pl.pallas_call(
        _ada_kernel,
        grid=(DEPTH, n_out // tn),
        in_specs=[
            pl.BlockSpec((B, D), lambda l, j: (0, 0)),
            pl.BlockSpec((1, D, tn), lambda l, j: (l, 0, j)),
            pl.BlockSpec((1, 1, tn), lambda l, j: (l, 0, j)),
        ],
        out_specs=pl.BlockSpec((1, B, tn), lambda l, j: (l, 0, j)),
        out_shape=jax.ShapeDtypeStruct((DEPTH, B, n_out), F32),
        compiler_params=_cparams(("parallel", "parallel")),
        name="ada_mod",
    )(c, ada_w, ada_b.reshape(DEPTH, 1, n_out))


def _expand_kernel(tbl_ref, idx_ref, o_ref, *, rel_far):
    h = pl.program_id(0)
    idx = idx_ref[...]
    acc = jnp.full(idx.shape, NEG_INF, F32)
    base = tbl_ref[FAR_BUCKET, h] if rel_far else 0.0
    for b in range(N_BUCKETS):
        acc = jnp.where(idx == b, (tbl_ref[b, h] - base) * LOG2E, acc)
    o_ref[0] = acc


def _expand_bias(table, idx, rel_far=False):
    R, C = idx.shape
    rt = min(R, 256)
    return pl.pallas_call(
        functools.partial(_expand_kernel, rel_far=rel_far),
        grid=(table.shape[1], R // rt),
        in_specs=[
            pl.BlockSpec(memory_space=pltpu.SMEM),
            pl.BlockSpec((rt, C), lambda h, r: (r, 0)),
        ],
        out_specs=pl.BlockSpec((1, rt, C), lambda h, r: (h, r, 0)),
        out_shape=jax.ShapeDtypeStruct((table.shape[1], R, C), F32),
        compiler_params=_cparams(("parallel", "parallel")),
        name="expand_bias",
    )(table, idx)


def _t5_bucket(dist):
    n = jnp.maximum(dist, 0)
    max_exact = N_BUCKETS // 2
    nf = jnp.maximum(n, 1).astype(jnp.float32)
    large = max_exact + (jnp.log(nf / max_exact) / math.log(MAX_DISTANCE / max_exact)
                         * (N_BUCKETS - max_exact)).astype(jnp.int32)
    large = jnp.minimum(large, N_BUCKETS - 1)
    return jnp.where(n < max_exact, n, large)


def _bias_index_tiles(S):
    a = np.arange(TA)[None, :]
    b = np.arange(TA)[:, None]
    d0 = a - b
    d1 = TA + a - b
    t0 = jnp.where(jnp.asarray(d0 >= 0), _t5_bucket(jnp.asarray(d0)), MASK_BUCKET)
    t1 = _t5_bucket(jnp.asarray(d1))
    t1w = jnp.where(jnp.asarray(d1 < WINDOW), t1, MASK_BUCKET)
    toe_t = jnp.concatenate([t0, t1, t1w], axis=0).astype(jnp.int32)
    t_pos = np.arange(S)[None, :]
    n = np.arange(N_CMP_PAD)[:, None]
    d_cmp = t_pos - (n * CMP_STRIDE + CMP_LEN - 1)
    cmp_t = jnp.where(jnp.asarray(d_cmp >= 0), _t5_bucket(jnp.asarray(d_cmp)), MASK_BUCKET).astype(jnp.int32)
    return toe_t, cmp_t


def _modmm_kernel(x_ref, sc_ref, sh_ref, w_ref, o_ref, *, tn):
    h = (x_ref[...] * (1.0 + sc_ref[0]) + sh_ref[0]).astype(BF16)
    for j in range(w_ref.shape[1] // tn):
        cs = slice(j * tn, (j + 1) * tn)
        o_ref[:, cs] = _mm(h, w_ref[:, cs]).astype(BF16)


def _mod_matmul(x, scale, shift, w, S):
    T, D = x.shape
    N = w.shape[1]
    tm, tn = 512, 512
    per_seq = S // tm
    return pl.pallas_call(
        functools.partial(_modmm_kernel, tn=tn),
        grid=(T // tm,),
        in_specs=[
            pl.BlockSpec((tm, D), lambda i: (i, 0)),
            pl.BlockSpec((1, 1, D), lambda i: (i // per_seq, 0, 0)),
            pl.BlockSpec((1, 1, D), lambda i: (i // per_seq, 0, 0)),
            pl.BlockSpec((D, N), lambda i: (0, 0), pipeline_mode=pl.Buffered(1)),
        ],
        out_specs=pl.BlockSpec((tm, N), lambda i: (i, 0)),
        out_shape=jax.ShapeDtypeStruct((T, N), BF16),
        compiler_params=_cparams(("parallel",)),
        name="mod_matmul",
    )(x, scale, shift, w)


def _mla_prep_kernel(p_ref, qn_ref, kvn_ref, wq_ref, wk_ref, wv_ref, er_ref, ct_ref, st_ref, mr_ref,
                     q_ref, k_ref, v_ref):
    p = p_ref[...]
    ql = p[:, :MLA_Q_RANK].astype(F32)
    kl = p[:, MLA_Q_RANK:MLA_Q_RANK + MLA_KV_RANK].astype(F32)
    misc = p[:, MISC_COL:MISC_COL + LANES].astype(F32)
    c_q = (ql * lax.rsqrt(jnp.mean(ql * ql, axis=-1, keepdims=True) + RMS_EPS) * qn_ref[...]).astype(BF16)
    c_kv = (kl * lax.rsqrt(jnp.mean(kl * kl, axis=-1, keepdims=True) + RMS_EPS) * kvn_ref[...]).astype(BF16)
    ab = _mm(c_q, wq_ref[...])
    ct = ct_ref[...]
    st = st_ref[...]
    half = MLA_HEADS * LANES
    for h in range(MLA_HEADS):
        lo = h * LANES
        q_ref[:, lo:lo + LANES] = (ab[:, lo:lo + LANES] * ct + ab[:, half + lo:half + lo + LANES] * st).astype(BF16)
    k_rope = _mm((misc * mr_ref[...]).astype(BF16), er_ref[...])
    k_ref[...] = (_mm(c_kv, wk_ref[...]) + k_rope).astype(BF16)
    v_ref[...] = _mm(c_kv, wv_ref[...]).astype(BF16)


def _mla_prep(proj, qn, kvn, wq, wk, wv, erope, ctab, stab, mrope, S):
    T = proj.shape[0]
    tm = 512
    per_seq = S // tm
    hw = MLA_HEADS * LANES
    const = lambda i: (0, 0)
    seq = lambda i: (i % per_seq, 0)
    return pl.pallas_call(
        _mla_prep_kernel,
        grid=(T // tm,),
        in_specs=[
            pl.BlockSpec((tm, NSA_Q_COL), lambda i: (i, 0)),
            pl.BlockSpec((1, MLA_Q_RANK), const),
            pl.BlockSpec((1, MLA_KV_RANK), const),
            pl.BlockSpec((MLA_Q_RANK, 2 * hw), const),
            pl.BlockSpec((MLA_KV_RANK, hw), const),
            pl.BlockSpec((MLA_KV_RANK, MLA_HEADS * MLA_V), const),
            pl.BlockSpec((LANES, hw), const),
            pl.BlockSpec((tm, LANES), seq),
            pl.BlockSpec((tm, LANES), seq),
            pl.BlockSpec((tm, LANES), seq),
        ],
        out_specs=[
            pl.BlockSpec((tm, hw), lambda i: (i, 0)),
            pl.BlockSpec((tm, hw), lambda i: (i, 0)),
            pl.BlockSpec((tm, MLA_HEADS * MLA_V), lambda i: (i, 0)),
        ],
        out_shape=[
            jax.ShapeDtypeStruct((T, hw), BF16),
            jax.ShapeDtypeStruct((T, hw), BF16),
            jax.ShapeDtypeStruct((T, MLA_HEADS * MLA_V), BF16),
        ],
        compiler_params=_cparams(("parallel",)),
        name="mla_prep",
    )(proj, qn, kvn, wq, wk, wv, erope, ctab, stab, mrope)


def _mla_attn_kernel(q_ref, k_ref, v_ref, cm_ref, id_ref, o_ref, vt_ref, s_ref, *, n_tiles):
    hv = MLA_V + ONES_ROWS
    vt = _nt(id_ref[...], v_ref[...]).astype(BF16)
    ones = jnp.ones((ONES_ROWS, vt.shape[1]), BF16)
    for hh in range(2):
        vt_ref[hh * hv:hh * hv + MLA_V, :] = vt[hh * MLA_V:(hh + 1) * MLA_V]
        vt_ref[hh * hv + MLA_V:(hh + 1) * hv, :] = ones
    outs = {}

    def make_job(qi, hh):
        rows = slice(qi * TA, (qi + 1) * TA)
        cs = slice(hh * LANES, (hh + 1) * LANES)

        def score():
            s = _nt(k_ref[0:(qi + 1) * TA, cs], q_ref[rows, cs])
            return _bias_tail(s, None, cm_ref[...])

        def finish(acc, l):
            outs[hh] = acc * (1.0 / l)
            if hh == 1:
                o_ref[rows, :] = jnp.concatenate([outs[0], outs[1]], axis=0).T.astype(BF16)

        return _SoftmaxJob(qi + 1, score, lambda c: vt_ref[hh * hv:(hh + 1) * hv, c * TA:(c + 1) * TA],
                           s_ref.at[hh], finish)

    _run_interleaved([make_job(qi, hh) for qi in range(n_tiles) for hh in range(2)])


def _mla_attention(q, k, v, cmask_t, ident, B, S):
    pairs = MLA_HEADS // 2
    return pl.pallas_call(
        functools.partial(_mla_attn_kernel, n_tiles=S // TA),
        grid=(B, pairs),
        in_specs=[
            pl.BlockSpec((S, 2 * LANES), lambda b, hp: (b, hp)),
            pl.BlockSpec((S, 2 * LANES), lambda b, hp: (b, hp)),
            pl.BlockSpec((S, LANES), lambda b, hp: (b, hp)),
            pl.BlockSpec((TA, TA), lambda b, hp: (0, 0)),
            pl.BlockSpec((LANES, LANES), lambda b, hp: (0, 0)),
        ],
        out_specs=pl.BlockSpec((S, LANES), lambda b, hp: (b, hp)),
        out_shape=jax.ShapeDtypeStruct((B * S, MLA_HEADS * MLA_V), BF16),
        scratch_shapes=[pltpu.VMEM((2 * (MLA_V + ONES_ROWS), S), BF16), pltpu.VMEM((2, S, TA), F32)],
        compiler_params=_cparams(("parallel", "parallel")),
        name="mla_attention",
    )(q, k, v, cmask_t, ident)


def _compress_kernel(r_ref, pe_ref, w1_ref, w2_ref, w2t_ref, o_ref, ot_ref):
    half = CMP_STRIDE * NSA_DH
    r = r_ref[0, 0].astype(F32)
    pe = pe_ref[0]
    x_lo = (r + pe[:, :half]).astype(BF16)
    x_hi = (r + pe[:, half:]).astype(BF16)
    y_lo = _mm(x_lo, w1_ref[0, :half, :])
    y_hi = _mm(x_hi, w1_ref[0, half:, :])
    y = y_lo + pltpu.roll(y_hi, N_CMP_PAD - 1, 0)
    hid = jax.nn.gelu(y).astype(BF16)
    o_ref[0, 0] = _mm(hid, w2_ref[0]).astype(BF16)
    ot_ref[0, 0] = _nt(w2t_ref[0], hid).astype(BF16)


def _compress(r, pe, w1, w2_rep, w2_t, B):
    rep = NSA_HPG * NSA_DH
    return pl.pallas_call(
        _compress_kernel,
        grid=(4, B),
        in_specs=[
            pl.BlockSpec((1, 1, N_CMP_PAD, CMP_STRIDE * NSA_DH), lambda c, b: (c, b, 0, 0)),
            pl.BlockSpec((1, 1, CMP_LEN * NSA_DH), lambda c, b: (c // 2, 0, 0)),
            pl.BlockSpec((1, CMP_LEN * NSA_DH, CMP_HIDDEN), lambda c, b: (c // 2, 0, 0)),
            pl.BlockSpec((1, CMP_HIDDEN, rep), lambda c, b: (c // 2, 0, 0)),
            pl.BlockSpec((1, NSA_DH, CMP_HIDDEN), lambda c, b: (c // 2, 0, 0)),
        ],
        out_specs=[
            pl.BlockSpec((1, 1, N_CMP_PAD, rep), lambda c, b: (c, b, 0, 0)),
            pl.BlockSpec((1, 1, NSA_DH, N_CMP_PAD), lambda c, b: (c, b, 0, 0)),
        ],
        out_shape=[
            jax.ShapeDtypeStruct((4, B, N_CMP_PAD, rep), BF16),
            jax.ShapeDtypeStruct((4, B, NSA_DH, N_CMP_PAD), BF16),
        ],
        compiler_params=_cparams(("parallel", "parallel")),
        name="nsa_compress",
    )(r, pe, w1, w2_rep, w2_t)


def _nsa_kernel(q_ref, kv_ref, misc_ref, kc_ref, vct_ref, bc_ref, toe_ref, erep_ref, esel_ref, ovl_ref, o_ref,
                sk_ref, wk_ref, svt_ref, wvt_ref, ssel_ref, swin_ref, *, n_tiles):
    g = pl.program_id(1)
    J = NSA_HPG
    W = J * NSA_DH
    QW = J * TA
    n_blk = n_tiles * TA // SEL_LEN
    kv = kv_ref[...]
    sk_ref[...] = _mm(kv, erep_ref[0]).astype(BF16)
    wk_ref[...] = _mm(kv, erep_ref[1]).astype(BF16)
    ones = jnp.ones((ONES_ROWS, kv.shape[0]), BF16)
    for ref, slot in ((svt_ref, 0), (wvt_ref, 1)):
        ref[0:NSA_DH, :] = _nt(esel_ref[slot], kv).astype(BF16)
        ref[NSA_DH:, :] = ones

    head_of_lane = lax.broadcasted_iota(jnp.int32, (TA, W), 1) >> 6
    blk = lax.broadcasted_iota(jnp.int32, (n_blk, TA), 0)
    qpos = lax.broadcasted_iota(jnp.int32, (n_blk, TA), 1)
    g_is0 = g == 0

    def make_jobs(qi):
        rows = slice(qi * TA, (qi + 1) * TA)
        first = max(qi - 1, 0)
        st = {}

        def q4():
            if "q4" not in st:
                q = q_ref[rows, :]
                zero = jnp.zeros_like(q)
                st["q4"] = jnp.concatenate([jnp.where(head_of_lane == j, q, zero) for j in range(J)], axis=0)
            return st["q4"]

        def select():
            s = _nt(kc_ref[0, 0], q4()) + bc_ref[0, qi]
            e = jnp.exp2(s - jnp.max(s, axis=0, keepdims=True))
            p = e * (1.0 / jnp.sum(e, axis=0, keepdims=True))
            if qi * TA < CMP_LEN - 1:
                tq = (lax.broadcasted_iota(jnp.int32, p.shape, 1) & (TA - 1)) + qi * TA
                p = jnp.where(tq >= CMP_LEN - 1, p, 0.0)
            st["o_cmp"] = _mm(vct_ref[0, 0], p.astype(BF16))
            p_sum = p[:, 0:TA]
            for j in range(1, J):
                p_sum = p_sum + p[:, j * TA:(j + 1) * TA]
            hi, mid, lo = _split3(p_sum)
            imp = _mm(ovl_ref[...], hi) + _mm(ovl_ref[...], mid) + _mm(ovl_ref[...], lo)
            cur = (qpos + qi * TA) >> 6
            forced = (blk == 0) | (blk == cur) | (blk == cur - 1)
            score = jnp.where(forced, FORCE_SCORE, imp)
            score = jnp.where(blk <= cur, score, -1.0)
            rank = jnp.zeros((n_blk, TA), F32)
            for mp in range(n_blk):
                r = score[mp:mp + 1, :]
                beats = (r > score) | ((r == score) & (blk > mp))
                rank = rank + jnp.where(beats, 1.0, 0.0)
            pen = jnp.where(rank < SEL_TOP, 0.0, NEG_INF)
            return jnp.concatenate([pen] * J, axis=1)

        def score_sel():
            n_keys = (qi + 1) * TA
            pen4 = select()
            pen_rows = jnp.concatenate(
                [jnp.broadcast_to(pen4[m:m + 1, :], (SEL_LEN, QW)) for m in range(n_keys // SEL_LEN)], axis=0)
            s = _nt(sk_ref[0:n_keys, :], q4()) + pen_rows
            return _bias_tail(s, toe_ref[0, 1] if qi > 0 else None, toe_ref[0, 0])

        def score_win():
            s = _nt(wk_ref[first * TA:(qi + 1) * TA, :], q4())
            return _bias_tail(s, toe_ref[0, 2] if qi > 0 else None, toe_ref[0, 0])

        def finish_win(acc, l):
            st["o_win"] = acc * (1.0 / l)

        def finish_sel(acc, l):
            o_slc = acc * (1.0 / l)
            gl = misc_ref[rows, :].astype(F32)
            sig_t = (1.0 / (1.0 + jnp.exp(-gl))).T
            mixes = []
            for j in range(J):
                def gate(br, j=j):
                    r0 = 2 * MLA_ROPE + j * 3 + br
                    r1 = r0 + J * 3
                    return jnp.where(g_is0, sig_t[r0:r0 + 1, :], sig_t[r1:r1 + 1, :])

                cs = slice(j * TA, (j + 1) * TA)
                mixes.append(gate(0) * st["o_cmp"][:, cs] + gate(1) * o_slc[:, cs] + gate(2) * st["o_win"][:, cs])
            o_ref[rows, :] = jnp.concatenate(mixes, axis=0).T.astype(BF16)

        win = _SoftmaxJob(qi + 1 - first, score_win, lambda c: wvt_ref[:, (first + c) * TA:(first + c + 1) * TA],
                          swin_ref, finish_win)
        sel = _SoftmaxJob(qi + 1, score_sel, lambda c: svt_ref[:, c * TA:(c + 1) * TA], ssel_ref, finish_sel)
        return [win, sel]

    _run_interleaved([job for qi in range(n_tiles) for job in make_jobs(qi)])


def _nsa_attention(proj, kcmp, vcmp_t, bias_cmp_t, toe_g, erep, esel, ovl_t, B, S):
    J = NSA_HPG
    W = J * NSA_DH
    n_tiles = S // TA
    n_blk = S // SEL_LEN
    q_blk = NSA_Q_COL // W
    kv_blk = NSA_KV_COL // W
    misc_blk = MISC_COL // LANES
    return pl.pallas_call(
        functools.partial(_nsa_kernel, n_tiles=n_tiles),
        grid=(B, NSA_GROUPS),
        in_specs=[
            pl.BlockSpec((S, W), lambda b, g: (b, q_blk + g)),
            pl.BlockSpec((S, W), lambda b, g: (b, kv_blk + g)),
            pl.BlockSpec((S, LANES), lambda b, g: (b, misc_blk)),
            pl.BlockSpec((1, 1, N_CMP_PAD, W), lambda b, g: (g, b, 0, 0)),
            pl.BlockSpec((1, 1, NSA_DH, N_CMP_PAD), lambda b, g: (2 + g, b, 0, 0)),
            pl.BlockSpec((1, n_tiles, N_CMP_PAD, J * TA), lambda b, g: (g, 0, 0, 0)),
            pl.BlockSpec((1, 3, TA, J * TA), lambda b, g: (g, 0, 0, 0)),
            pl.BlockSpec((2, W, W), lambda b, g: (0, 0, 0)),
            pl.BlockSpec((2, NSA_DH, W), lambda b, g: (0, 0, 0)),
            pl.BlockSpec((n_blk, N_CMP_PAD), lambda b, g: (0, 0)),
        ],
        out_specs=pl.BlockSpec((S, W), lambda b, g: (b, g)),
        out_shape=jax.ShapeDtypeStruct((B * S, NSA_HEADS * NSA_DH), BF16),
        scratch_shapes=[pltpu.VMEM((S, W), BF16), pltpu.VMEM((S, W), BF16),
                        pltpu.VMEM((NSA_DH + ONES_ROWS, S), BF16), pltpu.VMEM((NSA_DH + ONES_ROWS, S), BF16),
                        pltpu.VMEM((S, J * TA), F32), pltpu.VMEM((2 * TA, J * TA), F32)],
        compiler_params=_cparams(("parallel", "parallel")),
        name="nsa_attention",
    )(proj, proj, proj, kcmp, vcmp_t, bias_cmp_t, toe_g, erep, esel, ovl_t)


def _diff_kernel(q_ref, k_ref, v_ref, toe_ref, lam_ref, sub_ref, id_ref, o_ref, vt_ref, s_ref, *, n_tiles, lam_init):
    hd = DIFF_DH
    lane = lax.broadcasted_iota(jnp.int32, (TA, 2 * hd), 1)
    lv = lam_ref[...]
    lam = (jnp.exp(jnp.sum(lv[0:1] * lv[1:2], axis=-1, keepdims=True))
           - jnp.exp(jnp.sum(lv[2:3] * lv[3:4], axis=-1, keepdims=True)) + lam_init)
    vt_ref[0:2 * hd, :] = _nt(id_ref[...], v_ref[...]).astype(BF16)
    vt_ref[2 * hd:, :] = jnp.ones((ONES_ROWS, vt_ref.shape[1]), BF16)

    def both(t):
        return jnp.concatenate([t, t], axis=1)

    def make_job(qi):
        rows = slice(qi * TA, (qi + 1) * TA)

        def score():
            q = q_ref[rows, :]
            zero = jnp.zeros_like(q)
            q2 = jnp.concatenate([jnp.where(lane < hd, q, zero), jnp.where(lane >= hd, q, zero)], axis=0)
            s = _nt(k_ref[0:(qi + 1) * TA, :], q2)
            near = both(toe_ref[0, TA:2 * TA, :]) if qi > 0 else None
            return _bias_tail(s, near, both(toe_ref[0, 0:TA, :]))

        def finish(acc, l):
            o2 = acc * (1.0 / l)
            o = (o2[:, 0:TA] - lam * o2[:, TA:2 * TA]).T
            o = o * lax.rsqrt(jnp.mean(o * o, axis=-1, keepdims=True) + RMS_EPS) * sub_ref[...]
            o_ref[rows, :] = (o * (1.0 - lam_init)).astype(BF16)

        return _SoftmaxJob(qi + 1, score, lambda c: vt_ref[:, c * TA:(c + 1) * TA], s_ref.at[qi % 2], finish)

    _run_interleaved([make_job(qi) for qi in range(n_tiles)])


def _diff_attention(proj, toe_t, lam_vec, subln, ident, layer_idx, B, S):
    H = DIFF_HEADS
    hw = 2 * DIFF_DH
    lam_init = 0.8 - 0.6 * math.exp(-0.3 * layer_idx)
    return pl.pallas_call(
        functools.partial(_diff_kernel, n_tiles=S // TA, lam_init=lam_init),
        grid=(B, H),
        in_specs=[
            pl.BlockSpec((S, hw), lambda b, h: (b, h)),
            pl.BlockSpec((S, hw), lambda b, h: (b, H + h)),
            pl.BlockSpec((S, hw), lambda b, h: (b, 2 * H + h)),
            pl.BlockSpec((1, 3 * TA, TA), lambda b, h: (h, 0, 0)),
            pl.BlockSpec((4, DIFF_DH), lambda b, h: (0, 0)),
            pl.BlockSpec((1, hw), lambda b, h: (0, 0)),
            pl.BlockSpec((hw, hw), lambda b, h: (0, 0)),
        ],
        out_specs=pl.BlockSpec((S, hw), lambda b, h: (b, h)),
        out_shape=jax.ShapeDtypeStruct((B * S, H * hw), BF16),
        scratch_shapes=[pltpu.VMEM((hw + ONES_ROWS, S), BF16), pltpu.VMEM((2, S, 2 * TA), F32)],
        compiler_params=_cparams(("parallel", "parallel")),
        name="diff_attention",
    )(proj, proj, proj, toe_t, lam_vec, subln.reshape(1, hw), ident)


def _tail_kernel(oa_ref, ob_ref, wo_ref, x_ref, g1_ref, lg1_ref, lb1_ref, sc_ref, sh_ref, g2_ref,
                 wu_ref, wg_ref, cw_ref, cb_ref, wd_ref, lg2_ref, lb2_ref, o_ref, hprev_ref, act_ref,
                 *, per_seq, tf):
    i = pl.program_id(0)
    half = oa_ref.shape[1]
    tm = x_ref.shape[0]
    y = _mm(oa_ref[...], wo_ref[:half, :]) + _mm(ob_ref[...], wo_ref[half:, :])
    x1 = _layer_norm_rows(ALPHA * x_ref[...] + (1.0 + g1_ref[0]) * y, lg1_ref[...], lb1_ref[...])
    o_ref[...] = x1
    h = (x1 * (1.0 + sc_ref[0]) + sh_ref[0]).astype(BF16)

    @pl.when(i % per_seq == 0)
    def _():
        hprev_ref[...] = jnp.zeros(hprev_ref.shape, BF16)

    h_ext = jnp.concatenate([hprev_ref[...], h], axis=0)
    hprev_ref[...] = h[tm - CONV_HALO:, :]
    for f in range(D_FF // tf):
        cs = slice(f * tf, (f + 1) * tf)
        u = _mm(h, wu_ref[:, cs])
        ge = _mm(h_ext, wg_ref[:, cs])
        cw = cw_ref[:, cs]
        a = (cw[2:3] * ge[CONV_HALO:CONV_HALO + tm] + cw[1:2] * ge[CONV_HALO - 1:CONV_HALO - 1 + tm]
             + cw[0:1] * ge[CONV_HALO - 2:CONV_HALO - 2 + tm] + cb_ref[:, cs])
        act_ref[:, cs] = (a / (1.0 + jnp.exp(-a)) * u).astype(BF16)
    y2 = _mm(act_ref[...], wd_ref[...])
    z = ALPHA * o_ref[...] + (1.0 + g2_ref[0]) * y2
    o_ref[...] = _layer_norm_rows(z, lg2_ref[...], lb2_ref[...])


def _layer_tail(o_a, o_b, blk_a, blk_b, w_o, x, g1, ln1_g, ln1_b, scale, shift, g2, w_up, w_gate, conv_w, conv_b,
                w_down, ln2_g, ln2_b, S):
    T, D = x.shape
    tm, tf = 1024, 256
    half = w_o.shape[0] // 2
    per_seq = S // tm
    once = pl.Buffered(1)

    def const(shape):
        return pl.BlockSpec(shape, lambda i: (0,) * len(shape), pipeline_mode=once)

    mod_spec = pl.BlockSpec((1, 1, D), lambda i: (i // per_seq, 0, 0))
    return pl.pallas_call(
        functools.partial(_tail_kernel, per_seq=per_seq, tf=tf),
        grid=(T // tm,),
        in_specs=[
            pl.BlockSpec((tm, half), lambda i: (i, blk_a)),
            pl.BlockSpec((tm, half), lambda i: (i, blk_b)),
            const((2 * half, D)),
            pl.BlockSpec((tm, D), lambda i: (i, 0)),
            mod_spec, const((1, D)), const((1, D)),
            mod_spec, mod_spec, mod_spec,
            const((D, D_FF)), const((D, D_FF)), const((3, D_FF)), const((1, D_FF)), const((D_FF, D)),
            const((1, D)), const((1, D)),
        ],
        out_specs=pl.BlockSpec((tm, D), lambda i: (i, 0)),
        out_shape=jax.ShapeDtypeStruct((T, D), F32),
        scratch_shapes=[pltpu.VMEM((CONV_HALO, D), BF16), pltpu.VMEM((tm, D_FF), BF16)],
        compiler_params=_cparams(("arbitrary",)),
        name="layer_tail",
    )(o_a, o_b, w_o, x, g1, ln1_g, ln1_b, scale, shift, g2, w_up, w_gate, conv_w, conv_b, w_down, ln2_g, ln2_b)


def _even_in_columns():
    src = np.zeros(EVEN_W, np.int32)
    mul = np.zeros(EVEN_W, np.float32)

    def put(dst, cols, scale=1.0):
        cols = np.asarray(cols)
        src[dst:dst + len(cols)] = cols
        mul[dst:dst + len(cols)] = scale

    put(0, np.arange(MLA_Q_RANK + MLA_KV_RANK))
    rope0 = MLA_Q_RANK + MLA_KV_RANK
    half = MLA_ROPE // 2
    put(MISC_COL, rope0 + np.arange(MLA_ROPE))
    put(MISC_COL + MLA_ROPE, rope0 + half + np.arange(half), -1.0)
    put(MISC_COL + MLA_ROPE + half, rope0 + np.arange(half), 1.0)
    nsa0 = MLA_COLS
    put(MISC_COL + 2 * MLA_ROPE, nsa0 + NSA_Q_COLS + NSA_KV_COLS + np.arange(NSA_GATE_COLS))
    put(NSA_Q_COL, nsa0 + np.arange(NSA_Q_COLS), NSA_DH ** -0.5 * LOG2E)
    kv0 = nsa0 + NSA_Q_COLS

    def chunk(branch, kv, g):
        return kv0 + ((branch * 2 + kv) * NSA_GROUPS + g) * NSA_DH + np.arange(NSA_DH)

    for g in range(NSA_GROUPS):
        base = NSA_KV_COL + g * 4 * NSA_DH
        for slot, (branch, kv) in enumerate([(1, 0), (2, 0), (1, 1), (2, 1)]):
            put(base + slot * NSA_DH, chunk(branch, kv, g))
    put(NSA_CMP_COL, kv0 + np.arange(2 * NSA_GROUPS * NSA_DH))
    return src, mul


def _mla_up_columns():
    hw = MLA_HEADS * LANES
    qd = MLA_NOPE + MLA_ROPE
    half = MLA_ROPE // 2
    src = np.zeros(2 * hw, np.int32)
    mul = np.zeros(2 * hw, np.float32)
    for h in range(MLA_HEADS):
        src[h * LANES:h * LANES + qd] = h * qd + np.arange(qd)
        mul[h * LANES:h * LANES + qd] = 1.0
        r = hw + h * LANES + MLA_NOPE
        src[r:r + half] = h * qd + MLA_NOPE + half + np.arange(half)
        mul[r:r + half] = -1.0
        src[r + half:r + MLA_ROPE] = h * qd + MLA_NOPE + np.arange(half)
        mul[r + half:r + MLA_ROPE] = 1.0
    ksrc = np.zeros(hw, np.int32)
    kmul = np.zeros(hw, np.float32)
    vsrc = np.zeros(MLA_HEADS * MLA_V, np.int32)
    for h in range(MLA_HEADS):
        ksrc[h * LANES:h * LANES + MLA_NOPE] = h * (MLA_NOPE + MLA_V) + np.arange(MLA_NOPE)
        kmul[h * LANES:h * LANES + MLA_NOPE] = 1.0
        vsrc[h * MLA_V:(h + 1) * MLA_V] = h * (MLA_NOPE + MLA_V) + MLA_NOPE + np.arange(MLA_V)
    return src, mul, ksrc, kmul, vsrc


def _routing_constants(S):
    hw = MLA_HEADS * LANES
    erope = np.zeros((LANES, hw), np.float32)
    for h in range(MLA_HEADS):
        for i in range(MLA_ROPE):
            erope[i, h * LANES + MLA_NOPE + i] = 1.0
            erope[MLA_ROPE + i, h * LANES + MLA_NOPE + i] = 1.0
    W = NSA_HPG * NSA_DH
    erep = np.zeros((2, W, W), np.float32)
    for slot in range(2):
        for d in range(NSA_DH):
            for j in range(NSA_HPG):
                erep[slot, slot * NSA_DH + d, j * NSA_DH + d] = 1.0
    esel = np.zeros((2, NSA_DH, W), np.float32)
    for slot in range(2):
        esel[slot, np.arange(NSA_DH), (2 + slot) * NSA_DH + np.arange(NSA_DH)] = 1.0
    n_slc = S // SEL_LEN
    starts = np.arange(N_CMP_PAD) * CMP_STRIDE
    jb = np.arange(n_slc)
    ovl = ((starts[:, None] < (jb[None, :] + 1) * SEL_LEN)
           & (starts[:, None] + CMP_LEN > jb[None, :] * SEL_LEN)).astype(np.float32)
    ovl[(S - CMP_LEN) // CMP_STRIDE + 1:, :] = 0.0
    cmask_t = np.where(np.arange(TA)[None, :] >= np.arange(TA)[:, None], 0.0, NEG_INF).astype(np.float32)
    return erope, erep, esel, np.ascontiguousarray(ovl.T), cmask_t


def _rope_tables(S):
    inv = 1.0 / (ROPE_BASE ** (jnp.arange(0, MLA_ROPE, 2, dtype=jnp.float32) / MLA_ROPE))
    ang = jnp.arange(S, dtype=jnp.float32)[:, None] * inv[None, :]
    cos, sin = jnp.cos(ang), jnp.sin(ang)
    cos2 = jnp.concatenate([cos, cos], axis=-1)
    sin2 = jnp.concatenate([sin, sin], axis=-1)
    scale = (MLA_NOPE + MLA_ROPE) ** -0.5 * LOG2E
    z32 = jnp.zeros((S, LANES - MLA_NOPE - MLA_ROPE), F32)
    ctab = jnp.concatenate([jnp.full((S, MLA_NOPE), scale, F32), cos2 * scale, z32], axis=-1)
    stab = jnp.concatenate([jnp.zeros((S, MLA_NOPE), F32), sin2 * scale, z32], axis=-1)
    mrope = jnp.concatenate([cos2, sin2, jnp.zeros((S, LANES - 2 * MLA_ROPE), F32)], axis=-1)
    return ctab, stab, mrope


def kernel(x, c, rel_bias, ev_w_in, mla_q_norm, mla_kv_norm, mla_w_uq, mla_w_ukv, nsa_cmp_pe, nsa_cmp_w1,
           nsa_cmp_w2, ev_w_o, od_w_in, diff_lambda, diff_subln, od_w_o, ada_w, ada_b, ln_g, ln_b, ffn_w_up,
           ffn_w_gate, ffn_conv_w, ffn_conv_b, ffn_w_down):
    B, S, D = x.shape
    assert D == D_MODEL and S % TA == 0 and S // CMP_STRIDE == N_CMP_PAD and S // SEL_LEN <= LANES
    far_np = np.arange(TA + 1, max(S, TA + 2))
    far_bucket = 16 + (np.log(far_np.astype(np.float32) / 16) / math.log(MAX_DISTANCE / 16) * 16).astype(np.int32)
    assert far_bucket.min() >= FAR_BUCKET
    T = B * S
    xf = x.reshape(T, D)

    mod = _ada_mod(c, ada_w, ada_b)
    toe_t_idx, cmp_t_idx = _bias_index_tiles(S)
    toe_t = _expand_bias(rel_bias, toe_t_idx, rel_far=True)
    bias_cmp_t = _expand_bias(rel_bias, cmp_t_idx)
    G, J, NQ = NSA_GROUPS, NSA_HPG, S // TA
    toe_g = toe_t.reshape(G, J, 3, TA, TA).transpose(0, 2, 3, 1, 4).reshape(G, 3, TA, J * TA)
    bias_cmp_t = bias_cmp_t.reshape(G, J, N_CMP_PAD, NQ, TA).transpose(0, 3, 2, 1, 4).reshape(G, NQ, N_CMP_PAD, J * TA)
    erope, erep, esel, ovl_t, cmask_t = _routing_constants(S)
    ident = jnp.eye(LANES, dtype=BF16)
    ctab, stab, mrope = _rope_tables(S)
    ev_src, ev_mul = _even_in_columns()
    q_src, q_mul, k_src, k_mul, v_src = _mla_up_columns()

    def bf(a):
        return jnp.asarray(a).astype(BF16)

    for l in range(DEPTH):
        sh1, sc1, g1, sh2, sc2, g2 = [mod[l, :, k * D:(k + 1) * D].reshape(B, 1, D) for k in range(6)]
        i = l // 2
        if l % 2 == 0:
            w_in = bf(ev_w_in[i][:, ev_src] * ev_mul)
            proj = _mod_matmul(xf, sc1, sh1, w_in, S)
            wq = bf(mla_w_uq[i][:, q_src] * q_mul)
            wk = bf(mla_w_ukv[i][:, k_src] * k_mul)
            wv = bf(mla_w_ukv[i][:, v_src])
            q_m, k_m, v_m = _mla_prep(proj, mla_q_norm[i].reshape(1, -1), mla_kv_norm[i].reshape(1, -1),
                                      wq, wk, wv, bf(erope), ctab, stab, mrope, S)
            o_1 = _mla_attention(q_m, k_m, v_m, jnp.asarray(cmask_t), ident, B, S)
            r = proj[:, NSA_CMP_COL:].reshape(B, S, 4, NSA_DH).transpose(2, 0, 1, 3)
            r = r.reshape(4, B, N_CMP_PAD, CMP_STRIDE * NSA_DH)
            w2_rep = jnp.tile(nsa_cmp_w2[i], (1, 1, NSA_HPG))
            kcmp, vcmp_t = _compress(r, nsa_cmp_pe[i].reshape(2, 1, CMP_LEN * NSA_DH), bf(nsa_cmp_w1[i]), bf(w2_rep),
                                     bf(nsa_cmp_w2[i].transpose(0, 2, 1)), B)
            o_2 = _nsa_attention(proj, kcmp, vcmp_t, bias_cmp_t, toe_g, bf(erep), bf(esel), bf(ovl_t), B, S)
            blk_2, w_o = 0, ev_w_o[i]
        else:
            w_in = od_w_in[i].at[:, :DIFF_HEADS * 2 * DIFF_DH].multiply(DIFF_DH ** -0.5 * LOG2E)
            proj = _mod_matmul(xf, sc1, sh1, bf(w_in), S)
            o_1 = _diff_attention(proj, toe_t, diff_lambda[i], diff_subln[i], ident, l, B, S)
            o_2, blk_2, w_o = o_1, 1, od_w_o[i]
        xf = _layer_tail(o_1, o_2, 0, blk_2, bf(w_o), xf, g1, ln_g[l, 0:1], ln_b[l, 0:1], sc2, sh2, g2,
                         bf(ffn_w_up[l]), bf(ffn_w_gate[l]), ffn_conv_w[l], ffn_conv_b[l].reshape(1, D_FF),
                         bf(ffn_w_down[l]), ln_g[l, 1:2], ln_b[l, 1:2], S)
    return xf.reshape(B, S, D)
```

```python
import functools
import math

import numpy as np
import jax
import jax.numpy as jnp
from jax import lax
from jax.experimental import pallas as pl
from jax.experimental.pallas import tpu as pltpu

F32 = jnp.float32
BF16 = jnp.bfloat16

D_MODEL = 1024
DEPTH = 4
N_BUCKETS = 32
MAX_DISTANCE = 128
NEG_INF = -1e30

MLA_HEADS = 8
MLA_Q_RANK = 384
MLA_KV_RANK = 256
MLA_NOPE = 64
MLA_ROPE = 32
MLA_V = 64
ROPE_BASE = 10000.0
MLA_COLS = MLA_Q_RANK + MLA_KV_RANK + MLA_ROPE

NSA_HEADS = 8
NSA_GROUPS = 2
NSA_HPG = NSA_HEADS // NSA_GROUPS
NSA_DH = 64
CMP_LEN = 32
CMP_STRIDE = 16
CMP_HIDDEN = 256
SEL_LEN = 64
SEL_TOP = 8
FORCE_SCORE = 1e4
WINDOW = 256
NSA_Q_COLS = NSA_HEADS * NSA_DH
NSA_KV_COLS = 3 * 2 * NSA_GROUPS * NSA_DH
NSA_GATE_COLS = 3 * NSA_HEADS

DIFF_HEADS = 8
DIFF_DH = 64

D_FF = 2816
ALPHA = (2.0 * DEPTH) ** 0.25
LN_EPS = 1e-5
RMS_EPS = 1e-6

LANES = 128
VMEM_LIMIT = 56 * 1024 * 1024

TA = 256
EVEN_W = 2048
MISC_COL = 640
NSA_Q_COL = 768
NSA_KV_COL = 1280
NSA_CMP_COL = 1792
N_CMP_PAD = 128
MASK_BUCKET = N_BUCKETS
FAR_BUCKET = N_BUCKETS - 1
LOG2E = math.log2(math.e)
ONES_ROWS = 16
CONV_HALO = 16


def _cparams(sem):
    return pltpu.CompilerParams(dimension_semantics=sem, vmem_limit_bytes=VMEM_LIMIT)


def _nt(a, b):
    return lax.dot_general(a, b, (((1,), (1,)), ((), ())), preferred_element_type=F32)


def _mm(a, b):
    return jnp.dot(a, b, preferred_element_type=F32)


def _split3(v):
    hi = v.astype(BF16)
    r1 = v - hi.astype(F32)
    mid = r1.astype(BF16)
    lo = (r1 - mid.astype(F32)).astype(BF16)
    return hi, mid, lo


def _layer_norm_rows(z, g, b):
    mu = jnp.mean(z, axis=-1, keepdims=True)
    zc = z - mu
    var = jnp.mean(zc * zc, axis=-1, keepdims=True)
    return zc * lax.rsqrt(var + LN_EPS) * g + b


def _bias_tail(s, near, diag):
    n = s.shape[0]
    parts = []
    if near is not None:
        if n > 2 * TA:
            parts.append(s[:n - 2 * TA])
        parts.append(s[n - 2 * TA:n - TA] + near)
    elif n > TA:
        parts.append(s[:n - TA])
    parts.append(s[n - TA:] + diag)
    return parts[0] if len(parts) == 1 else jnp.concatenate(parts, axis=0)


class _SoftmaxJob:
    def __init__(self, n_chunks, scores, v_t_chunk, s_buf, finish):
        self.n, self.scores, self.v_t_chunk, self.s_buf, self.finish = n_chunks, scores, v_t_chunk, s_buf, finish
        self.m = self.acc = None

    def score_pass(self):
        s = self.scores()
        self.s_buf[0:self.n * TA, :] = s
        m8 = jnp.max(s.reshape(self.n * TA // 8, 8, s.shape[1]), axis=0)
        self.m = jnp.max(m8, axis=0, keepdims=True)

    def value_step(self, i):
        p = jnp.exp2((self.s_buf[i * TA:(i + 1) * TA, :] - self.m).astype(BF16))
        pv = _mm(self.v_t_chunk(i), p)
        self.acc = pv if self.acc is None else self.acc + pv

    def done(self):
        dv = self.acc.shape[0] - ONES_ROWS
        self.finish(self.acc[:dv], self.acc[dv:dv + 1])


def _run_interleaved(jobs):
    jobs[0].score_pass()
    for k, job in enumerate(jobs):
        if k + 1 < len(jobs):
            jobs[k + 1].score_pass()
        for i in range(job.n):
            job.value_step(i)
        job.done()


def _ada_kernel(c_ref, w_ref, b_ref, o_ref):
    c = c_ref[...]
    ca = c / (1.0 + jnp.exp(-c))
    w = w_ref[0]
    c_hi = ca.astype(BF16)
    c_lo = (ca - c_hi.astype(F32)).astype(BF16)
    w_hi = w.astype(BF16)
    w_lo = (w - w_hi.astype(F32)).astype(BF16)
    o_ref[0] = _mm(c_hi, w_hi) + _mm(c_hi, w_lo) + _mm(c_lo, w_hi) + b_ref[0]


def _ada_mod(c, ada_w, ada_b):
    B, D = c.shape
    n_out = ada_w.shape[-1]
    tn = 1024
    return pl.pallas_call(
        _ada_kernel,
        grid=(DEPTH, n_out // tn),
        in_specs=[
            pl.BlockSpec((B, D), lambda l, j: (0, 0)),
            pl.BlockSpec((1, D, tn), lambda l, j: (l, 0, j)),
            pl.BlockSpec((1, 1, tn), lambda l, j: (l, 0, j)),
        ],
        out_specs=pl.BlockSpec((1, B, tn), lambda l, j: (l, 0, j)),
        out_shape=jax.ShapeDtypeStruct((DEPTH, B, n_out), F32),
        compiler_params=_cparams(("parallel", "parallel")),
        name="ada_mod",
    )(c, ada_w, ada_b.reshape(DEPTH, 1, n_out))


def _expand_kernel(tbl_ref, idx_ref, o_ref, *, rel_far):
    h = pl.program_id(0)
    idx = idx_ref[...]
    acc = jnp.full(idx.shape, NEG_INF, F32)
    base = tbl_ref[FAR_BUCKET, h] if rel_far else 0.0
    for b in range(N_BUCKETS):
        acc = jnp.where(idx == b, (tbl_ref[b, h] - base) * LOG2E, acc)
    o_ref[0] = acc


def _expand_bias(table, idx, rel_far=False):
    R, C = idx.shape
    rt = min(R, 256)
    return pl.pallas_call(
        functools.partial(_expand_kernel, rel_far=rel_far),
        grid=(table.shape[1], R // rt),
        in_specs=[
            pl.BlockSpec(memory_space=pltpu.SMEM),
            pl.BlockSpec((rt, C), lambda h, r: (r, 0)),
        ],
        out_specs=pl.BlockSpec((1, rt, C), lambda h, r: (h, r, 0)),
        out_shape=jax.ShapeDtypeStruct((table.shape[1], R, C), F32),
        compiler_params=_cparams(("parallel", "parallel")),
        name="expand_bias",
    )(table, idx)


def _t5_bucket(dist):
    n = jnp.maximum(dist, 0)
    max_exact = N_BUCKETS // 2
    nf = jnp.maximum(n, 1).astype(jnp.float32)
    large = max_exact + (jnp.log(nf / max_exact) / math.log(MAX_DISTANCE / max_exact)
                         * (N_BUCKETS - max_exact)).astype(jnp.int32)
    large = jnp.minimum(large, N_BUCKETS - 1)
    return jnp.where(n < max_exact, n, large)


def _bias_index_tiles(S):
    a = np.arange(TA)[None, :]
    b = np.arange(TA)[:, None]
    d0 = a - b
    d1 = TA + a - b
    t0 = jnp.where(jnp.asarray(d0 >= 0), _t5_bucket(jnp.asarray(d0)), MASK_BUCKET)
    t1 = _t5_bucket(jnp.asarray(d1))
    t1w = jnp.where(jnp.asarray(d1 < WINDOW), t1, MASK_BUCKET)
    toe_t = jnp.concatenate([t0, t1, t1w], axis=0).astype(jnp.int32)
    t_pos = np.arange(S)[None, :]
    n = np.arange(N_CMP_PAD)[:, None]
    d_cmp = t_pos - (n * CMP_STRIDE + CMP_LEN - 1)
    cmp_t = jnp.where(jnp.asarray(d_cmp >= 0), _t5_bucket(jnp.asarray(d_cmp)), MASK_BUCKET).astype(jnp.int32)
    return toe_t, cmp_t


def _modmm_kernel(x_ref, sc_ref, sh_ref, w_ref, o_ref, *, tn):
    h = (x_ref[...] * (1.0 + sc_ref[0]) + sh_ref[0]).astype(BF16)
    for j in range(w_ref.shape[1] // tn):
        cs = slice(j * tn, (j + 1) * tn)
        o_ref[:, cs] = _mm(h, w_ref[:, cs]).astype(BF16)


def _mod_matmul(x, scale, shift, w, S):
    T, D = x.shape
    N = w.shape[1]
    tm, tn = 512, 512
    per_seq = S // tm
    return pl.pallas_call(
        functools.partial(_modmm_kernel, tn=tn),
        grid=(T // tm,),
        in_specs=[
            pl.BlockSpec((tm, D), lambda i: (i, 0)),
            pl.BlockSpec((1, 1, D), lambda i: (i // per_seq, 0, 0)),
            pl.BlockSpec((1, 1, D), lambda i: (i // per_seq, 0, 0)),
            pl.BlockSpec((D, N), lambda i: (0, 0), pipeline_mode=pl.Buffered(1)),
        ],
        out_specs=pl.BlockSpec((tm, N), lambda i: (i, 0)),
        out_shape=jax.ShapeDtypeStruct((T, N), BF16),
        compiler_params=_cparams(("parallel",)),
        name="mod_matmul",
    )(x, scale, shift, w)


def _mla_prep_kernel(p_ref, qn_ref, kvn_ref, wq_ref, wk_ref, wv_ref, er_ref, ct_ref, st_ref, mr_ref,
                     q_ref, k_ref, v_ref):
    p = p_ref[...]
    ql = p[:, :MLA_Q_RANK].astype(F32)
    kl = p[:, MLA_Q_RANK:MLA_Q_RANK + MLA_KV_RANK].astype(F32)
    misc = p[:, MISC_COL:MISC_COL + LANES].astype(F32)
    c_q = (ql * lax.rsqrt(jnp.mean(ql * ql, axis=-1, keepdims=True) + RMS_EPS) * qn_ref[...]).astype(BF16)
    c_kv = (kl * lax.rsqrt(jnp.mean(kl * kl, axis=-1, keepdims=True) + RMS_EPS) * kvn_ref[...]).astype(BF16)
    ab = _mm(c_q, wq_ref[...])
    ct = ct_ref[...]
    st = st_ref[...]
    half = MLA_HEADS * LANES
    for h in range(MLA_HEADS):
        lo = h * LANES
        q_ref[:, lo:lo + LANES] = (ab[:, lo:lo + LANES] * ct + ab[:, half + lo:half + lo + LANES] * st).astype(BF16)
    k_rope = _mm((misc * mr_ref[...]).astype(BF16), er_ref[...])
    k_ref[...] = (_mm(c_kv, wk_ref[...]) + k_rope).astype(BF16)
    v_ref[...] = _mm(c_kv, wv_ref[...]).astype(BF16)


def _mla_prep(proj, qn, kvn, wq, wk, wv, erope, ctab, stab, mrope, S):
    T = proj.shape[0]
    tm = 512
    per_seq = S // tm
    hw = MLA_HEADS * LANES
    const = lambda i: (0, 0)
    seq = lambda i: (i % per_seq, 0)
    return pl.pallas_call(
        _mla_prep_kernel,
        grid=(T // tm,),
        in_specs=[
            pl.BlockSpec((tm, NSA_Q_COL), lambda i: (i, 0)),
            pl.BlockSpec((1, MLA_Q_RANK), const),
            pl.BlockSpec((1, MLA_KV_RANK), const),
            pl.BlockSpec((MLA_Q_RANK, 2 * hw), const),
            pl.BlockSpec((MLA_KV_RANK, hw), const),
            pl.BlockSpec((MLA_KV_RANK, MLA_HEADS * MLA_V), const),
            pl.BlockSpec((LANES, hw), const),
            pl.BlockSpec((tm, LANES), seq),
            pl.BlockSpec((tm, LANES), seq),
            pl.BlockSpec((tm, LANES), seq),
        ],
        out_specs=[
            pl.BlockSpec((tm, hw), lambda i: (i, 0)),
            pl.BlockSpec((tm, hw), lambda i: (i, 0)),
            pl.BlockSpec((tm, MLA_HEADS * MLA_V), lambda i: (i, 0)),
        ],
        out_shape=[
            jax.ShapeDtypeStruct((T, hw), BF16),
            jax.ShapeDtypeStruct((T, hw), BF16),
            jax.ShapeDtypeStruct((T, MLA_HEADS * MLA_V), BF16),
        ],
        compiler_params=_cparams(("parallel",)),
        name="mla_prep",
    )(proj, qn, kvn, wq, wk, wv, erope, ctab, stab, mrope)


def _mla_attn_kernel(q_ref, k_ref, v_ref, cm_ref, id_ref, o_ref, vt_ref, s_ref, *, n_tiles):
    hv = MLA_V + ONES_ROWS
    vt = _nt(id_ref[...], v_ref[...]).astype(BF16)
    ones = jnp.ones((ONES_ROWS, vt.shape[1]), BF16)
    for hh in range(2):
        vt_ref[hh * hv:hh * hv + MLA_V, :] = vt[hh * MLA_V:(hh + 1) * MLA_V]
        vt_ref[hh * hv + MLA_V:(hh + 1) * hv, :] = ones
    outs = {}

    def make_job(qi, hh):
        rows = slice(qi * TA, (qi + 1) * TA)
        cs = slice(hh * LANES, (hh + 1) * LANES)

        def score():
            s = _nt(k_ref[0:(qi + 1) * TA, cs], q_ref[rows, cs])
            return _bias_tail(s, None, cm_ref[...])

        def finish(acc, l):
            outs[hh] = acc * (1.0 / l)
            if hh == 1:
                o_ref[rows, :] = jnp.concatenate([outs[0], outs[1]], axis=0).T.astype(BF16)

        return _SoftmaxJob(qi + 1, score, lambda c: vt_ref[hh * hv:(hh + 1) * hv, c * TA:(c + 1) * TA],
                           s_ref.at[hh], finish)

    _run_interleaved([make_job(qi, hh) for qi in range(n_tiles) for hh in range(2)])


def _mla_attention(q, k, v, cmask_t, ident, B, S):
    pairs = MLA_HEADS // 2
    return pl.pallas_call(
        functools.partial(_mla_attn_kernel, n_tiles=S // TA),
        grid=(B, pairs),
        in_specs=[
            pl.BlockSpec((S, 2 * LANES), lambda b, hp: (b, hp)),
            pl.BlockSpec((S, 2 * LANES), lambda b, hp: (b, hp)),
            pl.BlockSpec((S, LANES), lambda b, hp: (b, hp)),
            pl.BlockSpec((TA, TA), lambda b, hp: (0, 0)),
            pl.BlockSpec((LANES, LANES), lambda b, hp: (0, 0)),
        ],
        out_specs=pl.BlockSpec((S, LANES), lambda b, hp: (b, hp)),
        out_shape=jax.ShapeDtypeStruct((B * S, MLA_HEADS * MLA_V), BF16),
        scratch_shapes=[pltpu.VMEM((2 * (MLA_V + ONES_ROWS), S), BF16), pltpu.VMEM((2, S, TA), F32)],
        compiler_params=_cparams(("parallel", "parallel")),
        name="mla_attention",
    )(q, k, v, cmask_t, ident)


def _compress_kernel(r_ref, pe_ref, w1_ref, w2_ref, w2t_ref, o_ref, ot_ref):
    half = CMP_STRIDE * NSA_DH
    r = r_ref[0, 0].astype(F32)
    pe = pe_ref[0]
    x_lo = (r + pe[:, :half]).astype(BF16)
    x_hi = (r + pe[:, half:]).astype(BF16)
    y_lo = _mm(x_lo, w1_ref[0, :half, :])
    y_hi = _mm(x_hi, w1_ref[0, half:, :])
    y = y_lo + pltpu.roll(y_hi, N_CMP_PAD - 1, 0)
    hid = jax.nn.gelu(y).astype(BF16)
    o_ref[0, 0] = _mm(hid, w2_ref[0]).astype(BF16)
    ot_ref[0, 0] = _nt(w2t_ref[0], hid).astype(BF16)


def _compress(r, pe, w1, w2_rep, w2_t, B):
    rep = NSA_HPG * NSA_DH
    return pl.pallas_call(
        _compress_kernel,
        grid=(4, B),
        in_specs=[
            pl.BlockSpec((1, 1, N_CMP_PAD, CMP_STRIDE * NSA_DH), lambda c, b: (c, b, 0, 0)),
            pl.BlockSpec((1, 1, CMP_LEN * NSA_DH), lambda c, b: (c // 2, 0, 0)),
            pl.BlockSpec((1, CMP_LEN * NSA_DH, CMP_HIDDEN), lambda c, b: (c // 2, 0, 0)),
            pl.BlockSpec((1, CMP_HIDDEN, rep), lambda c, b: (c // 2, 0, 0)),
            pl.BlockSpec((1, NSA_DH, CMP_HIDDEN), lambda c, b: (c // 2, 0, 0)),
        ],
        out_specs=[
            pl.BlockSpec((1, 1, N_CMP_PAD, rep), lambda c, b: (c, b, 0, 0)),
            pl.BlockSpec((1, 1, NSA_DH, N_CMP_PAD), lambda c, b: (c, b, 0, 0)),
        ],
        out_shape=[
            jax.ShapeDtypeStruct((4, B, N_CMP_PAD, rep), BF16),
            jax.ShapeDtypeStruct((4, B, NSA_DH, N_CMP_PAD), BF16),
        ],
        compiler_params=_cparams(("parallel", "parallel")),
        name="nsa_compress",
    )(r, pe, w1, w2_rep, w2_t)


def _nsa_kernel(q_ref, kv_ref, misc_ref, kc_ref, vct_ref, bc_ref, toe_ref, erep_ref, esel_ref, ovl_ref, o_ref,
                sk_ref, wk_ref, svt_ref, wvt_ref, ssel_ref, swin_ref, *, n_tiles):
    g = pl.program_id(1)
    J = NSA_HPG
    W = J * NSA_DH
    QW = J * TA
    n_blk = n_tiles * TA // SEL_LEN
    kv = kv_ref[...]
    sk_ref[...] = _mm(kv, erep_ref[0]).astype(BF16)
    wk_ref[...] = _mm(kv, erep_ref[1]).astype(BF16)
    ones = jnp.ones((ONES_ROWS, kv.shape[0]), BF16)
    for ref, slot in ((svt_ref, 0), (wvt_ref, 1)):
        ref[0:NSA_DH, :] = _nt(esel_ref[slot], kv).astype(BF16)
        ref[NSA_DH:, :] = ones

    head_of_lane = lax.broadcasted_iota(jnp.int32, (TA, W), 1) >> 6
    blk = lax.broadcasted_iota(jnp.int32, (n_blk, TA), 0)
    qpos = lax.broadcasted_iota(jnp.int32, (n_blk, TA), 1)
    g_is0 = g == 0

    def make_jobs(qi):
        rows = slice(qi * TA, (qi + 1) * TA)
        first = max(qi - 1, 0)
        st = {}

        def q4():
            if "q4" not in st:
                q = q_ref[rows, :]
                zero = jnp.zeros_like(q)
                st["q4"] = jnp.concatenate([jnp.where(head_of_lane == j, q, zero) for j in range(J)], axis=0)
            return st["q4"]

        def select():
            s = _nt(kc_ref[0, 0], q4()) + bc_ref[0, qi]
            e = jnp.exp2(s - jnp.max(s, axis=0, keepdims=True))
            p = e * (1.0 / jnp.sum(e, axis=0, keepdims=True))
            if qi * TA < CMP_LEN - 1:
                tq = (lax.broadcasted_iota(jnp.int32, p.shape, 1) & (TA - 1)) + qi * TA
                p = jnp.where(tq >= CMP_LEN - 1, p, 0.0)
            st["o_cmp"] = _mm(vct_ref[0, 0], p.astype(BF16))
            p_sum = p[:, 0:TA]
            for j in range(1, J):
                p_sum = p_sum + p[:, j * TA:(j + 1) * TA]
            hi, mid, lo = _split3(p_sum)
            imp = _mm(ovl_ref[...], hi) + _mm(ovl_ref[...], mid) + _mm(ovl_ref[...], lo)
            cur = (qpos + qi * TA) >> 6
            forced = (blk == 0) | (blk == cur) | (blk == cur - 1)
            score = jnp.where(forced, FORCE_SCORE, imp)
            score = jnp.where(blk <= cur, score, -1.0)
            rank = jnp.zeros((n_blk, TA), F32)
            for mp in range(n_blk):
                r = score[mp:mp + 1, :]
                beats = (r > score) | ((r == score) & (blk > mp))
                rank = rank + jnp.where(beats, 1.0, 0.0)
            pen = jnp.where(rank < SEL_TOP, 0.0, NEG_INF)
            return jnp.concatenate([pen] * J, axis=1)

        def score_sel():
            n_keys = (qi + 1) * TA
            pen4 = select()
            pen_rows = jnp.concatenate(
                [jnp.broadcast_to(pen4[m:m + 1, :], (SEL_LEN, QW)) for m in range(n_keys // SEL_LEN)], axis=0)
            s = _nt(sk_ref[0:n_keys, :], q4()) + pen_rows
            return _bias_tail(s, toe_ref[0, 1] if qi > 0 else None, toe_ref[0, 0])

        def score_win():
            s = _nt(wk_ref[first * TA:(qi + 1) * TA, :], q4())
            return _bias_tail(s, toe_ref[0, 2] if qi > 0 else None, toe_ref[0, 0])

        def finish_sel(acc, l):
            st["o_slc"] = acc * (1.0 / l)

        def finish_win(acc, l):
            o_win = acc * (1.0 / l)
            gl = misc_ref[rows, :].astype(F32)
            sig_t = (1.0 / (1.0 + jnp.exp(-gl))).T
            mixes = []
            for j in range(J):
                def gate(br, j=j):
                    r0 = 2 * MLA_ROPE + j * 3 + br
                    r1 = r0 + J * 3
                    return jnp.where(g_is0, sig_t[r0:r0 + 1, :], sig_t[r1:r1 + 1, :])

                cs = slice(j * TA, (j + 1) * TA)
                mixes.append(gate(0) * st["o_cmp"][:, cs] + gate(1) * st["o_slc"][:, cs] + gate(2) * o_win[:, cs])
            o_ref[rows, :] = jnp.concatenate(mixes, axis=0).T.astype(BF16)

        win = _SoftmaxJob(qi + 1 - first, score_win, lambda c: wvt_ref[:, (first + c) * TA:(first + c + 1) * TA],
                          swin_ref, finish_win)
        sel = _SoftmaxJob(qi + 1, score_sel, lambda c: svt_ref[:, c * TA:(c + 1) * TA], ssel_ref, finish_sel)
        return [sel, win]

    _run_interleaved([job for qi in range(n_tiles) for job in make_jobs(qi)])


def _nsa_attention(proj, kcmp, vcmp_t, bias_cmp_t, toe_g, erep, esel, ovl_t, B, S):
    J = NSA_HPG
    W = J * NSA_DH
    n_tiles = S // TA
    n_blk = S // SEL_LEN
    q_blk = NSA_Q_COL // W
    kv_blk = NSA_KV_COL // W
    misc_blk = MISC_COL // LANES
    return pl.pallas_call(
        functools.partial(_nsa_kernel, n_tiles=n_tiles),
        grid=(B, NSA_GROUPS),
        in_specs=[
            pl.BlockSpec((S, W), lambda b, g: (b, q_blk + g)),
            pl.BlockSpec((S, W), lambda b, g: (b, kv_blk + g)),
            pl.BlockSpec((S, LANES), lambda b, g: (b, misc_blk)),
            pl.BlockSpec((1, 1, N_CMP_PAD, W), lambda b, g: (g, b, 0, 0)),
            pl.BlockSpec((1, 1, NSA_DH, N_CMP_PAD), lambda b, g: (2 + g, b, 0, 0)),
            pl.BlockSpec((1, n_tiles, N_CMP_PAD, J * TA), lambda b, g: (g, 0, 0, 0)),
            pl.BlockSpec((1, 3, TA, J * TA), lambda b, g: (g, 0, 0, 0)),
            pl.BlockSpec((2, W, W), lambda b, g: (0, 0, 0)),
            pl.BlockSpec((2, NSA_DH, W), lambda b, g: (0, 0, 0)),
            pl.BlockSpec((n_blk, N_CMP_PAD), lambda b, g: (0, 0)),
        ],
        out_specs=pl.BlockSpec((S, W), lambda b, g: (b, g)),
        out_shape=jax.ShapeDtypeStruct((B * S, NSA_HEADS * NSA_DH), BF16),
        scratch_shapes=[pltpu.VMEM((S, W), BF16), pltpu.VMEM((S, W), BF16),
                        pltpu.VMEM((NSA_DH + ONES_ROWS, S), BF16), pltpu.VMEM((NSA_DH + ONES_ROWS, S), BF16),
                        pltpu.VMEM((S, J * TA), F32), pltpu.VMEM((2 * TA, J * TA), F32)],
        compiler_params=_cparams(("parallel", "parallel")),
        name="nsa_attention",
    )(proj, proj, proj, kcmp, vcmp_t, bias_cmp_t, toe_g, erep, esel, ovl_t)


def _diff_kernel(q_ref, k_ref, v_ref, toe_ref, lam_ref, sub_ref, id_ref, o_ref, vt_ref, s_ref, *, n_tiles, lam_init):
    hd = DIFF_DH
    lane = lax.broadcasted_iota(jnp.int32, (TA, 2 * hd), 1)
    lv = lam_ref[...]
    lam = (jnp.exp(jnp.sum(lv[0:1] * lv[1:2], axis=-1, keepdims=True))
           - jnp.exp(jnp.sum(lv[2:3] * lv[3:4], axis=-1, keepdims=True)) + lam_init)
    vt_ref[0:2 * hd, :] = _nt(id_ref[...], v_ref[...]).astype(BF16)
    vt_ref[2 * hd:, :] = jnp.ones((ONES_ROWS, vt_ref.shape[1]), BF16)

    def both(t):
        return jnp.concatenate([t, t], axis=1)

    def make_job(qi):
        rows = slice(qi * TA, (qi + 1) * TA)

        def score():
            q = q_ref[rows, :]
            zero = jnp.zeros_like(q)
            q2 = jnp.concatenate([jnp.where(lane < hd, q, zero), jnp.where(lane >= hd, q, zero)], axis=0)
            s = _nt(k_ref[0:(qi + 1) * TA, :], q2)
            near = both(toe_ref[0, TA:2 * TA, :]) if qi > 0 else None
            return _bias_tail(s, near, both(toe_ref[0, 0:TA, :]))

        def finish(acc, l):
            o2 = acc * (1.0 / l)
            o = (o2[:, 0:TA] - lam * o2[:, TA:2 * TA]).T
            o = o * lax.rsqrt(jnp.mean(o * o, axis=-1, keepdims=True) + RMS_EPS) * sub_ref[...]
            o_ref[rows, :] = (o * (1.0 - lam_init)).astype(BF16)

        return _SoftmaxJob(qi + 1, score, lambda c: vt_ref[:, c * TA:(c + 1) * TA], s_ref.at[qi % 2], finish)

    _run_interleaved([make_job(qi) for qi in range(n_tiles)])


def _diff_attention(proj, toe_t, lam_vec, subln, ident, layer_idx, B, S):
    H = DIFF_HEADS
    hw = 2 * DIFF_DH
    lam_init = 0.8 - 0.6 * math.exp(-0.3 * layer_idx)
    return pl.pallas_call(
        functools.partial(_diff_kernel, n_tiles=S // TA, lam_init=lam_init),
        grid=(B, H),
        in_specs=[
            pl.BlockSpec((S, hw), lambda b, h: (b, h)),
            pl.BlockSpec((S, hw), lambda b, h: (b, H + h)),
            pl.BlockSpec((S, hw), lambda b, h: (b, 2 * H + h)),
            pl.BlockSpec((1, 3 * TA, TA), lambda b, h: (h, 0, 0)),
            pl.BlockSpec((4, DIFF_DH), lambda b, h: (0, 0)),
            pl.BlockSpec((1, hw), lambda b, h: (0, 0)),
            pl.BlockSpec((hw, hw), lambda b, h: (0, 0)),
        ],
        out_specs=pl.BlockSpec((S, hw), lambda b, h: (b, h)),
        out_shape=jax.ShapeDtypeStruct((B * S, H * hw), BF16),
        scratch_shapes=[pltpu.VMEM((hw + ONES_ROWS, S), BF16), pltpu.VMEM((2, S, 2 * TA), F32)],
        compiler_params=_cparams(("parallel", "parallel")),
        name="diff_attention",
    )(proj, proj, proj, toe_t, lam_vec, subln.reshape(1, hw), ident)


def _tail_kernel(oa_ref, ob_ref, wo_ref, x_ref, g1_ref, lg1_ref, lb1_ref, sc_ref, sh_ref, g2_ref,
                 wu_ref, wg_ref, cw_ref, cb_ref, wd_ref, lg2_ref, lb2_ref, o_ref, hprev_ref, act_ref,
                 *, per_seq, tf):
    i = pl.program_id(0)
    half = oa_ref.shape[1]
    tm = x_ref.shape[0]
    y = _mm(oa_ref[...], wo_ref[:half, :]) + _mm(ob_ref[...], wo_ref[half:, :])
    x1 = _layer_norm_rows(ALPHA * x_ref[...] + (1.0 + g1_ref[0]) * y, lg1_ref[...], lb1_ref[...])
    o_ref[...] = x1
    h = (x1 * (1.0 + sc_ref[0]) + sh_ref[0]).astype(BF16)

    @pl.when(i % per_seq == 0)
    def _():
        hprev_ref[...] = jnp.zeros(hprev_ref.shape, BF16)

    h_ext = jnp.concatenate([hprev_ref[...], h], axis=0)
    hprev_ref[...] = h[tm - CONV_HALO:, :]
    for f in range(D_FF // tf):
        cs = slice(f * tf, (f + 1) * tf)
        u = _mm(h, wu_ref[:, cs])
        ge = _mm(h_ext, wg_ref[:, cs])
        cw = cw_ref[:, cs]
        a = (cw[2:3] * ge[CONV_HALO:CONV_HALO + tm] + cw[1:2] * ge[CONV_HALO - 1:CONV_HALO - 1 + tm]
             + cw[0:1] * ge[CONV_HALO - 2:CONV_HALO - 2 + tm] + cb_ref[:, cs])
        act_ref[:, cs] = (a / (1.0 + jnp.exp(-a)) * u).astype(BF16)
    y2 = _mm(act_ref[...], wd_ref[...])
    z = ALPHA * o_ref[...] + (1.0 + g2_ref[0]) * y2
    o_ref[...] = _layer_norm_rows(z, lg2_ref[...], lb2_ref[...])


def _layer_tail(o_a, o_b, blk_a, blk_b, w_o, x, g1, ln1_g, ln1_b, scale, shift, g2, w_up, w_gate, conv_w, conv_b,
                w_down, ln2_g, ln2_b, S):
    T, D = x.shape
    tm, tf = 1024, 256
    half = w_o.shape[0] // 2
    per_seq = S // tm
    once = pl.Buffered(1)

    def const(shape):
        return pl.BlockSpec(shape, lambda i: (0,) * len(shape), pipeline_mode=once)

    mod_spec = pl.BlockSpec((1, 1, D), lambda i: (i // per_seq, 0, 0))
    return pl.pallas_call(
        functools.partial(_tail_kernel, per_seq=per_seq, tf=tf),
        grid=(T // tm,),
        in_specs=[
            pl.BlockSpec((tm, half), lambda i: (i, blk_a)),
            pl.BlockSpec((tm, half), lambda i: (i, blk_b)),
            const((2 * half, D)),
            pl.BlockSpec((tm, D), lambda i: (i, 0)),
            mod_spec, const((1, D)), const((1, D)),
            mod_spec, mod_spec, mod_spec,
            const((D, D_FF)), const((D, D_FF)), const((3, D_FF)), const((1, D_FF)), const((D_FF, D)),
            const((1, D)), const((1, D)),
        ],
        out_specs=pl.BlockSpec((tm, D), lambda i: (i, 0)),
        out_shape=jax.ShapeDtypeStruct((T, D), F32),
        scratch_shapes=[pltpu.VMEM((CONV_HALO, D), BF16), pltpu.VMEM((tm, D_FF), BF16)],
        compiler_params=_cparams(("arbitrary",)),
        name="layer_tail",
    )(o_a, o_b, w_o, x, g1, ln1_g, ln1_b, scale, shift, g2, w_up, w_gate, conv_w, conv_b, w_down, ln2_g, ln2_b)


def _take_columns(w, src, mul):
    parts = []
    start = 0
    n = len(src)
    while start < n:
        end = start + 1
        while end < n and mul[end] == mul[start] and (mul[start] == 0.0 or src[end] == src[end - 1] + 1):
            end += 1
        if mul[start] == 0.0:
            parts.append(jnp.zeros((w.shape[0], end - start), w.dtype))
        else:
            piece = w[:, int(src[start]):int(src[start]) + end - start]
            parts.append(piece if mul[start] == 1.0 else piece * float(mul[start]))
        start = end
    return jnp.concatenate(parts, axis=1)


def _even_in_columns():
    src = np.zeros(EVEN_W, np.int32)
    mul = np.zeros(EVEN_W, np.float32)

    def put(dst, cols, scale=1.0):
        cols = np.asarray(cols)
        src[dst:dst + len(cols)] = cols
        mul[dst:dst + len(cols)] = scale

    put(0, np.arange(MLA_Q_RANK + MLA_KV_RANK))
    rope0 = MLA_Q_RANK + MLA_KV_RANK
    half = MLA_ROPE // 2
    put(MISC_COL, rope0 + np.arange(MLA_ROPE))
    put(MISC_COL + MLA_ROPE, rope0 + half + np.arange(half), -1.0)
    put(MISC_COL + MLA_ROPE + half, rope0 + np.arange(half), 1.0)
    nsa0 = MLA_COLS
    put(MISC_COL + 2 * MLA_ROPE, nsa0 + NSA_Q_COLS + NSA_KV_COLS + np.arange(NSA_GATE_COLS))
    put(NSA_Q_COL, nsa0 + np.arange(NSA_Q_COLS), NSA_DH ** -0.5 * LOG2E)
    kv0 = nsa0 + NSA_Q_COLS

    def chunk(branch, kv, g):
        return kv0 + ((branch * 2 + kv) * NSA_GROUPS + g) * NSA_DH + np.arange(NSA_DH)

    for g in range(NSA_GROUPS):
        base = NSA_KV_COL + g * 4 * NSA_DH
        for slot, (branch, kv) in enumerate([(1, 0), (2, 0), (1, 1), (2, 1)]):
            put(base + slot * NSA_DH, chunk(branch, kv, g))
    put(NSA_CMP_COL, kv0 + np.arange(2 * NSA_GROUPS * NSA_DH))
    return src, mul


def _mla_up_columns():
    hw = MLA_HEADS * LANES
    qd = MLA_NOPE + MLA_ROPE
    half = MLA_ROPE // 2
    src = np.zeros(2 * hw, np.int32)
    mul = np.zeros(2 * hw, np.float32)
    for h in range(MLA_HEADS):
        src[h * LANES:h * LANES + qd] = h * qd + np.arange(qd)
        mul[h * LANES:h * LANES + qd] = 1.0
        r = hw + h * LANES + MLA_NOPE
        src[r:r + half] = h * qd + MLA_NOPE + half + np.arange(half)
        mul[r:r + half] = -1.0
        src[r + half:r + MLA_ROPE] = h * qd + MLA_NOPE + np.arange(half)
        mul[r + half:r + MLA_ROPE] = 1.0
    ksrc = np.zeros(hw, np.int32)
    kmul = np.zeros(hw, np.float32)
    vsrc = np.zeros(MLA_HEADS * MLA_V, np.int32)
    for h in range(MLA_HEADS):
        ksrc[h * LANES:h * LANES + MLA_NOPE] = h * (MLA_NOPE + MLA_V) + np.arange(MLA_NOPE)
        kmul[h * LANES:h * LANES + MLA_NOPE] = 1.0
        vsrc[h * MLA_V:(h + 1) * MLA_V] = h * (MLA_NOPE + MLA_V) + MLA_NOPE + np.arange(MLA_V)
    return src, mul, ksrc, kmul, vsrc


def _routing_constants(S):
    hw = MLA_HEADS * LANES
    erope = np.zeros((LANES, hw), np.float32)
    for h in range(MLA_HEADS):
        for i in range(MLA_ROPE):
            erope[i, h * LANES + MLA_NOPE + i] = 1.0
            erope[MLA_ROPE + i, h * LANES + MLA_NOPE + i] = 1.0
    W = NSA_HPG * NSA_DH
    erep = np.zeros((2, W, W), np.float32)
    for slot in range(2):
        for d in range(NSA_DH):
            for j in range(NSA_HPG):
                erep[slot, slot * NSA_DH + d, j * NSA_DH + d] = 1.0
    esel = np.zeros((2, NSA_DH, W), np.float32)
    for slot in range(2):
        esel[slot, np.arange(NSA_DH), (2 + slot) * NSA_DH + np.arange(NSA_DH)] = 1.0
    n_slc = S // SEL_LEN
    starts = np.arange(N_CMP_PAD) * CMP_STRIDE
    jb = np.arange(n_slc)
    ovl = ((starts[:, None] < (jb[None, :] + 1) * SEL_LEN)
           & (starts[:, None] + CMP_LEN > jb[None, :] * SEL_LEN)).astype(np.float32)
    ovl[(S - CMP_LEN) // CMP_STRIDE + 1:, :] = 0.0
    cmask_t = np.where(np.arange(TA)[None, :] >= np.arange(TA)[:, None], 0.0, NEG_INF).astype(np.float32)
    return erope, erep, esel, np.ascontiguousarray(ovl.T), cmask_t


def _rope_tables(S):
    inv = 1.0 / (ROPE_BASE ** (jnp.arange(0, MLA_ROPE, 2, dtype=jnp.float32) / MLA_ROPE))
    ang = jnp.arange(S, dtype=jnp.float32)[:, None] * inv[None, :]
    cos, sin = jnp.cos(ang), jnp.sin(ang)
    cos2 = jnp.concatenate([cos, cos], axis=-1)
    sin2 = jnp.concatenate([sin, sin], axis=-1)
    scale = (MLA_NOPE + MLA_ROPE) ** -0.5 * LOG2E
    z32 = jnp.zeros((S, LANES - MLA_NOPE - MLA_ROPE), F32)
    ctab = jnp.concatenate([jnp.full((S, MLA_NOPE), scale, F32), cos2 * scale, z32], axis=-1)
    stab = jnp.concatenate([jnp.zeros((S, MLA_NOPE), F32), sin2 * scale, z32], axis=-1)
    mrope = jnp.concatenate([cos2, sin2, jnp.zeros((S, LANES - 2 * MLA_ROPE), F32)], axis=-1)
    return ctab, stab, mrope


def kernel(x, c, rel_bias, ev_w_in, mla_q_norm, mla_kv_norm, mla_w_uq, mla_w_ukv, nsa_cmp_pe, nsa_cmp_w1,
           nsa_cmp_w2, ev_w_o, od_w_in, diff_lambda, diff_subln, od_w_o, ada_w, ada_b, ln_g, ln_b, ffn_w_up,
           ffn_w_gate, ffn_conv_w, ffn_conv_b, ffn_w_down):
    B, S, D = x.shape
    assert D == D_MODEL and S % TA == 0 and S // CMP_STRIDE == N_CMP_PAD and S // SEL_LEN <= LANES
    far_np = np.arange(TA + 1, max(S, TA + 2))
    far_bucket = 16 + (np.log(far_np.astype(np.float32) / 16) / math.log(MAX_DISTANCE / 16) * 16).astype(np.int32)
    assert far_bucket.min() >= FAR_BUCKET
    T = B * S
    xf = x.reshape(T, D)

    mod = _ada_mod(c, ada_w, ada_b)
    toe_t_idx, cmp_t_idx = _bias_index_tiles(S)
    toe_t = _expand_bias(rel_bias, toe_t_idx, rel_far=True)
    bias_cmp_t = _expand_bias(rel_bias, cmp_t_idx)
    G, J, NQ = NSA_GROUPS, NSA_HPG, S // TA
    toe_g = toe_t.reshape(G, J, 3, TA, TA).transpose(0, 2, 3, 1, 4).reshape(G, 3, TA, J * TA)
    bias_cmp_t = bias_cmp_t.reshape(G, J, N_CMP_PAD, NQ, TA).transpose(0, 3, 2, 1, 4).reshape(G, NQ, N_CMP_PAD, J * TA)
    erope, erep, esel, ovl_t, cmask_t = _routing_constants(S)
    ident = jnp.eye(LANES, dtype=BF16)
    ctab, stab, mrope = _rope_tables(S)
    ev_src, ev_mul = _even_in_columns()
    q_src, q_mul, k_src, k_mul, v_src = _mla_up_columns()

    def bf(a):
        return jnp.asarray(a).astype(BF16)

    for l in range(DEPTH):
        sh1, sc1, g1, sh2, sc2, g2 = [mod[l, :, k * D:(k + 1) * D].reshape(B, 1, D) for k in range(6)]
        i = l // 2
        if l % 2 == 0:
            w_in = bf(_take_columns(ev_w_in[i], ev_src, ev_mul))
            proj = _mod_matmul(xf, sc1, sh1, w_in, S)
            wq = bf(_take_columns(mla_w_uq[i], q_src, q_mul))
            wk = bf(_take_columns(mla_w_ukv[i], k_src, k_mul))
            wv = bf(_take_columns(mla_w_ukv[i], v_src, np.ones(len(v_src), np.float32)))
            q_m, k_m, v_m = _mla_prep(proj, mla_q_norm[i].reshape(1, -1), mla_kv_norm[i].reshape(1, -1),
                                      wq, wk, wv, bf(erope), ctab, stab, mrope, S)
            o_1 = _mla_attention(q_m, k_m, v_m, jnp.asarray(cmask_t), ident, B, S)
            r = proj[:, NSA_CMP_COL:].reshape(B, S, 4, NSA_DH).transpose(2, 0, 1, 3)
            r = r.reshape(4, B, N_CMP_PAD, CMP_STRIDE * NSA_DH)
            w2_rep = jnp.tile(nsa_cmp_w2[i], (1, 1, NSA_HPG))
            kcmp, vcmp_t = _compress(r, nsa_cmp_pe[i].reshape(2, 1, CMP_LEN * NSA_DH), bf(nsa_cmp_w1[i]), bf(w2_rep),
                                     bf(nsa_cmp_w2[i].transpose(0, 2, 1)), B)
            o_2 = _nsa_attention(proj, kcmp, vcmp_t, bias_cmp_t, toe_g, bf(erep), bf(esel), bf(ovl_t), B, S)
            blk_2, w_o = 0, ev_w_o[i]
        else:
            w_in = od_w_in[i].at[:, :DIFF_HEADS * 2 * DIFF_DH].multiply(DIFF_DH ** -0.5 * LOG2E)
            proj = _mod_matmul(xf, sc1, sh1, bf(w_in), S)
            o_1 = _diff_attention(proj, toe_t, diff_lambda[i], diff_subln[i], ident, l, B, S)
            o_2, blk_2, w_o = o_1, 1, od_w_o[i]
        xf = _layer_tail(o_1, o_2, 0, blk_2, bf(w_o), xf, g1, ln_g[l, 0:1], ln_b[l, 0:1], sc2, sh2, g2,
                         bf(ffn_w_up[l]), bf(ffn_w_gate[l]), ffn_conv_w[l], ffn_conv_b[l].reshape(1, D_FF),
                         bf(ffn_w_down[l]), ln_g[l, 1:2], ln_b[l, 1:2], S)
    return xf.reshape(B, S, D)
```

```python
import functools
import math

import numpy as np
import jax
import jax.numpy as jnp
from jax import lax
from jax.experimental import pallas as pl
from jax.experimental.pallas import tpu as pltpu

F32 = jnp.float32
BF16 = jnp.bfloat16

D_MODEL = 1024
DEPTH = 4
N_BUCKETS = 32
MAX_DISTANCE = 128
NEG_INF = -1e30

MLA_HEADS = 8
MLA_Q_RANK = 384
MLA_KV_RANK = 256
MLA_NOPE = 64
MLA_ROPE = 32
MLA_V = 64
ROPE_BASE = 10000.0
MLA_COLS = MLA_Q_RANK + MLA_KV_RANK + MLA_ROPE

NSA_HEADS = 8
NSA_GROUPS = 2
NSA_HPG = NSA_HEADS // NSA_GROUPS
NSA_DH = 64
CMP_LEN = 32
CMP_STRIDE = 16
CMP_HIDDEN = 256
SEL_LEN = 64
SEL_TOP = 8
FORCE_SCORE = 1e4
WINDOW = 256
NSA_Q_COLS = NSA_HEADS * NSA_DH
NSA_KV_COLS = 3 * 2 * NSA_GROUPS * NSA_DH
NSA_GATE_COLS = 3 * NSA_HEADS

DIFF_HEADS = 8
DIFF_DH = 64

D_FF = 2816
ALPHA = (2.0 * DEPTH) ** 0.25
LN_EPS = 1e-5
RMS_EPS = 1e-6

LANES = 128
VMEM_LIMIT = 56 * 1024 * 1024

TA = 256
EVEN_W = 2048
MISC_COL = 640
NSA_Q_COL = 768
NSA_KV_COL = 1280
NSA_CMP_COL = 1792
N_CMP_PAD = 128
MASK_BUCKET = N_BUCKETS
FAR_BUCKET = N_BUCKETS - 1
LOG2E = math.log2(math.e)
ONES_ROWS = 16
CONV_HALO = 16


def _cparams(sem):
    return pltpu.CompilerParams(dimension_semantics=sem, vmem_limit_bytes=VMEM_LIMIT)


def _nt(a, b):
    return lax.dot_general(a, b, (((1,), (1,)), ((), ())), preferred_element_type=F32)


def _mm(a, b):
    return jnp.dot(a, b, preferred_element_type=F32)


def _split3(v):
    hi = v.astype(BF16)
    r1 = v - hi.astype(F32)
    mid = r1.astype(BF16)
    lo = (r1 - mid.astype(F32)).astype(BF16)
    return hi, mid, lo


def _layer_norm_rows(z, g, b):
    mu = jnp.mean(z, axis=-1, keepdims=True)
    zc = z - mu
    var = jnp.mean(zc * zc, axis=-1, keepdims=True)
    return zc * lax.rsqrt(var + LN_EPS) * g + b


def _bias_tail(s, near, diag):
    n = s.shape[0]
    parts = []
    if near is not None:
        if n > 2 * TA:
            parts.append(s[:n - 2 * TA])
        parts.append(s[n - 2 * TA:n - TA] + near)
    elif n > TA:
        parts.append(s[:n - TA])
    parts.append(s[n - TA:] + diag)
    return parts[0] if len(parts) == 1 else jnp.concatenate(parts, axis=0)


class _SoftmaxJob:
    def __init__(self, n_chunks, scores, v_t_chunk, s_buf, finish):
        self.n, self.scores, self.v_t_chunk, self.s_buf, self.finish = n_chunks, scores, v_t_chunk, s_buf, finish
        self.m = self.acc = None

    def score_pass(self):
        s = self.scores()
        self.s_buf[0:self.n * TA, :] = s
        m8 = jnp.max(s.reshape(self.n * TA // 8, 8, s.shape[1]), axis=0)
        self.m = jnp.max(m8, axis=0, keepdims=True)

    def value_step(self, i):
        p = jnp.exp2((self.s_buf[i * TA:(i + 1) * TA, :] - self.m).astype(BF16))
        pv = _mm(self.v_t_chunk(i), p)
        self.acc = pv if self.acc is None else self.acc + pv

    def done(self):
        dv = self.acc.shape[0] - ONES_ROWS
        self.finish(self.acc[:dv], self.acc[dv:dv + 1])


def _run_interleaved(jobs):
    jobs[0].score_pass()
    for k, job in enumerate(jobs):
        if k + 1 < len(jobs):
            jobs[k + 1].score_pass()
        for i in range(job.n):
            job.value_step(i)
        job.done()


def _ada_kernel(c_ref, w_ref, b_ref, o_ref):
    c = c_ref[...]
    ca = c / (1.0 + jnp.exp(-c))
    w = w_ref[0]
    c_hi = ca.astype(BF16)
    c_lo = (ca - c_hi.astype(F32)).astype(BF16)
    w_hi = w.astype(BF16)
    w_lo = (w - w_hi.astype(F32)).astype(BF16)
    o_ref[0] = _mm(c_hi, w_hi) + _mm(c_hi, w_lo) + _mm(c_lo, w_hi) + b_ref[0]


def _ada_mod(c, ada_w, ada_b):
    B, D = c.shape
    n_out = ada_w.shape[-1]
    tn = 1024
    return pl.pallas_call(
        _ada_kernel,
        grid=(DEPTH, n_out // tn),
        in_specs=[
            pl.BlockSpec((B, D), lambda l, j: (0, 0)),
            pl.BlockSpec((1, D, tn), lambda l, j: (l, 0, j)),
            pl.BlockSpec((1, 1, tn), lambda l, j: (l, 0, j)),
        ],
        out_specs=pl.BlockSpec((1, B, tn), lambda l, j: (l, 0, j)),
        out_shape=jax.ShapeDtypeStruct((DEPTH, B, n_out), F32),
        compiler_params=_cparams(("parallel", "parallel")),
        name="ada_mod",
    )(c, ada_w, ada_b.reshape(DEPTH, 1, n_out))


def _expand_kernel(tbl_ref, idx_ref, o_ref, *, rel_far):
    h = pl.program_id(0)
    idx = idx_ref[...]
    acc = jnp.full(idx.shape, NEG_INF, F32)
    base = tbl_ref[FAR_BUCKET, h] if rel_far else 0.0
    for b in range(N_BUCKETS):
        acc = jnp.where(idx == b, (tbl_ref[b, h] - base) * LOG2E, acc)
    o_ref[0] = acc


def _expand_bias(table, idx, rel_far=False):
    R, C = idx.shape
    rt = min(R, 256)
    return pl.pallas_call(
        functools.partial(_expand_kernel, rel_far=rel_far),
        grid=(table.shape[1], R // rt),
        in_specs=[
            pl.BlockSpec(memory_space=pltpu.SMEM),
            pl.BlockSpec((rt, C), lambda h, r: (r, 0)),
        ],
        out_specs=pl.BlockSpec((1, rt, C), lambda h, r: (h, r, 0)),
        out_shape=jax.ShapeDtypeStruct((table.shape[1], R, C), F32),
        compiler_params=_cparams(("parallel", "parallel")),
        name="expand_bias",
    )(table, idx)


def _t5_bucket(dist):
    n = jnp.maximum(dist, 0)
    max_exact = N_BUCKETS // 2
    nf = jnp.maximum(n, 1).astype(jnp.float32)
    large = max_exact + (jnp.log(nf / max_exact) / math.log(MAX_DISTANCE / max_exact)
                         * (N_BUCKETS - max_exact)).astype(jnp.int32)
    large = jnp.minimum(large, N_BUCKETS - 1)
    return jnp.where(n < max_exact, n, large)


def _bias_index_tiles(S):
    a = np.arange(TA)[None, :]
    b = np.arange(TA)[:, None]
    d0 = a - b
    d1 = TA + a - b
    t0 = jnp.where(jnp.asarray(d0 >= 0), _t5_bucket(jnp.asarray(d0)), MASK_BUCKET)
    t1 = _t5_bucket(jnp.asarray(d1))
    t1w = jnp.where(jnp.asarray(d1 < WINDOW), t1, MASK_BUCKET)
    toe_t = jnp.concatenate([t0, t1, t1w], axis=0).astype(jnp.int32)
    t_pos = np.arange(S)[None, :]
    n = np.arange(N_CMP_PAD)[:, None]
    d_cmp = t_pos - (n * CMP_STRIDE + CMP_LEN - 1)
    cmp_t = jnp.where(jnp.asarray(d_cmp >= 0), _t5_bucket(jnp.asarray(d_cmp)), MASK_BUCKET).astype(jnp.int32)
    return toe_t, cmp_t


def _modmm_kernel(x_ref, sc_ref, sh_ref, w_ref, o_ref, *rest, tn, planes_col):
    h = (x_ref[...] * (1.0 + sc_ref[0]) + sh_ref[0]).astype(BF16)
    for j in range(w_ref.shape[1] // tn):
        cs = slice(j * tn, (j + 1) * tn)
        res = _mm(h, w_ref[:, cs]).astype(BF16)
        o_ref[:, cs] = res
        if planes_col is not None and j * tn <= planes_col < (j + 1) * tn:
            planes_ref = rest[0]
            for c in range(planes_ref.shape[0]):
                lo = planes_col - j * tn + c * NSA_DH
                planes_ref[c] = res[:, lo:lo + NSA_DH]


def _mod_matmul(x, scale, shift, w, S, planes_col=None):
    T, D = x.shape
    N = w.shape[1]
    tm, tn = 512, 512
    per_seq = S // tm
    out_specs = [pl.BlockSpec((tm, N), lambda i: (i, 0))]
    out_shape = [jax.ShapeDtypeStruct((T, N), BF16)]
    if planes_col is not None:
        assert planes_col // tn == (planes_col + 4 * NSA_DH - 1) // tn
        out_specs.append(pl.BlockSpec((4, tm, NSA_DH), lambda i: (0, i, 0)))
        out_shape.append(jax.ShapeDtypeStruct((4, T, NSA_DH), BF16))
    return pl.pallas_call(
        functools.partial(_modmm_kernel, tn=tn, planes_col=planes_col),
        grid=(T // tm,),
        in_specs=[
            pl.BlockSpec((tm, D), lambda i: (i, 0)),
            pl.BlockSpec((1, 1, D), lambda i: (i // per_seq, 0, 0)),
            pl.BlockSpec((1, 1, D), lambda i: (i // per_seq, 0, 0)),
            pl.BlockSpec((D, N), lambda i: (0, 0), pipeline_mode=pl.Buffered(1)),
        ],
        out_specs=out_specs,
        out_shape=out_shape,
        compiler_params=_cparams(("parallel",)),
        name="mod_matmul",
    )(x, scale, shift, w)


def _mla_prep_kernel(p_ref, qn_ref, kvn_ref, wq_ref, wk_ref, wv_ref, er_ref, ct_ref, st_ref, mr_ref,
                     q_ref, k_ref, v_ref):
    p = p_ref[...]
    ql = p[:, :MLA_Q_RANK].astype(F32)
    kl = p[:, MLA_Q_RANK:MLA_Q_RANK + MLA_KV_RANK].astype(F32)
    misc = p[:, MISC_COL:MISC_COL + LANES].astype(F32)
    c_q = (ql * lax.rsqrt(jnp.mean(ql * ql, axis=-1, keepdims=True) + RMS_EPS) * qn_ref[...]).astype(BF16)
    c_kv = (kl * lax.rsqrt(jnp.mean(kl * kl, axis=-1, keepdims=True) + RMS_EPS) * kvn_ref[...]).astype(BF16)
    ab = _mm(c_q, wq_ref[...])
    ct = ct_ref[...]
    st = st_ref[...]
    half = MLA_HEADS * LANES
    for h in range(MLA_HEADS):
        lo = h * LANES
        q_ref[:, lo:lo + LANES] = (ab[:, lo:lo + LANES] * ct + ab[:, half + lo:half + lo + LANES] * st).astype(BF16)
    k_rope = _mm((misc * mr_ref[...]).astype(BF16), er_ref[...])
    k_ref[...] = (_mm(c_kv, wk_ref[...]) + k_rope).astype(BF16)
    v_ref[...] = _mm(c_kv, wv_ref[...]).astype(BF16)


def _mla_prep(proj, qn, kvn, wq, wk, wv, erope, ctab, stab, mrope, S):
    T = proj.shape[0]
    tm = 512
    per_seq = S // tm
    hw = MLA_HEADS * LANES
    const = lambda i: (0, 0)
    seq = lambda i: (i % per_seq, 0)
    return pl.pallas_call(
        _mla_prep_kernel,
        grid=(T // tm,),
        in_specs=[
            pl.BlockSpec((tm, NSA_Q_COL), lambda i: (i, 0)),
            pl.BlockSpec((1, MLA_Q_RANK), const),
            pl.BlockSpec((1, MLA_KV_RANK), const),
            pl.BlockSpec((MLA_Q_RANK, 2 * hw), const),
            pl.BlockSpec((MLA_KV_RANK, hw), const),
            pl.BlockSpec((MLA_KV_RANK, MLA_HEADS * MLA_V), const),
            pl.BlockSpec((LANES, hw), const),
            pl.BlockSpec((tm, LANES), seq),
            pl.BlockSpec((tm, LANES), seq),
            pl.BlockSpec((tm, LANES), seq),
        ],
        out_specs=[
            pl.BlockSpec((tm, hw), lambda i: (i, 0)),
            pl.BlockSpec((tm, hw), lambda i: (i, 0)),
            pl.BlockSpec((tm, MLA_HEADS * MLA_V), lambda i: (i, 0)),
        ],
        out_shape=[
            jax.ShapeDtypeStruct((T, hw), BF16),
            jax.ShapeDtypeStruct((T, hw), BF16),
            jax.ShapeDtypeStruct((T, MLA_HEADS * MLA_V), BF16),
        ],
        compiler_params=_cparams(("parallel",)),
        name="mla_prep",
    )(proj, qn, kvn, wq, wk, wv, erope, ctab, stab, mrope)


def _mla_attn_kernel(q_ref, k_ref, v_ref, cm_ref, id_ref, o_ref, vt_ref, s_ref, *, n_tiles):
    hv = MLA_V + ONES_ROWS
    vt = _nt(id_ref[...], v_ref[...]).astype(BF16)
    ones = jnp.ones((ONES_ROWS, vt.shape[1]), BF16)
    for hh in range(2):
        vt_ref[hh * hv:hh * hv + MLA_V, :] = vt[hh * MLA_V:(hh + 1) * MLA_V]
        vt_ref[hh * hv + MLA_V:(hh + 1) * hv, :] = ones
    outs = {}

    def make_job(qi, hh):
        rows = slice(qi * TA, (qi + 1) * TA)
        cs = slice(hh * LANES, (hh + 1) * LANES)

        def score():
            s = _nt(k_ref[0:(qi + 1) * TA, cs], q_ref[rows, cs])
            return _bias_tail(s, None, cm_ref[...])

        def finish(acc, l):
            outs[hh] = acc * (1.0 / l)
            if hh == 1:
                o_ref[rows, :] = jnp.concatenate([outs[0], outs[1]], axis=0).T.astype(BF16)

        return _SoftmaxJob(qi + 1, score, lambda c: vt_ref[hh * hv:(hh + 1) * hv, c * TA:(c + 1) * TA],
                           s_ref.at[hh], finish)

    _run_interleaved([make_job(qi, hh) for qi in range(n_tiles) for hh in range(2)])


def _mla_attention(q, k, v, cmask_t, ident, B, S):
    pairs = MLA_HEADS // 2
    return pl.pallas_call(
        functools.partial(_mla_attn_kernel, n_tiles=S // TA),
        grid=(B, pairs),
        in_specs=[
            pl.BlockSpec((S, 2 * LANES), lambda b, hp: (b, hp)),
            pl.BlockSpec((S, 2 * LANES), lambda b, hp: (b, hp)),
            pl.BlockSpec((S, LANES), lambda b, hp: (b, hp)),
            pl.BlockSpec((TA, TA), lambda b, hp: (0, 0)),
            pl.BlockSpec((LANES, LANES), lambda b, hp: (0, 0)),
        ],
        out_specs=pl.BlockSpec((S, LANES), lambda b, hp: (b, hp)),
        out_shape=jax.ShapeDtypeStruct((B * S, MLA_HEADS * MLA_V), BF16),
        scratch_shapes=[pltpu.VMEM((2 * (MLA_V + ONES_ROWS), S), BF16), pltpu.VMEM((2, S, TA), F32)],
        compiler_params=_cparams(("parallel", "parallel")),
        name="mla_attention",
    )(q, k, v, cmask_t, ident)


def _compress_kernel(r_ref, pe_ref, w1_ref, w2_ref, w2t_ref, o_ref, ot_ref):
    half = CMP_STRIDE * NSA_DH
    r = r_ref[0, 0].astype(F32)
    pe = pe_ref[0]
    x_lo = (r + pe[:, :half]).astype(BF16)
    x_hi = (r + pe[:, half:]).astype(BF16)
    y_lo = _mm(x_lo, w1_ref[0, :half, :])
    y_hi = _mm(x_hi, w1_ref[0, half:, :])
    y = y_lo + pltpu.roll(y_hi, N_CMP_PAD - 1, 0)
    hid = jax.nn.gelu(y).astype(BF16)
    o_ref[0, 0] = _mm(hid, w2_ref[0]).astype(BF16)
    ot_ref[0, 0] = _nt(w2t_ref[0], hid).astype(BF16)


def _compress(r, pe, w1, w2_rep, w2_t, B):
    rep = NSA_HPG * NSA_DH
    return pl.pallas_call(
        _compress_kernel,
        grid=(4, B),
        in_specs=[
            pl.BlockSpec((1, 1, N_CMP_PAD, CMP_STRIDE * NSA_DH), lambda c, b: (c, b, 0, 0)),
            pl.BlockSpec((1, 1, CMP_LEN * NSA_DH), lambda c, b: (c // 2, 0, 0)),
            pl.BlockSpec((1, CMP_LEN * NSA_DH, CMP_HIDDEN), lambda c, b: (c // 2, 0, 0)),
            pl.BlockSpec((1, CMP_HIDDEN, rep), lambda c, b: (c // 2, 0, 0)),
            pl.BlockSpec((1, NSA_DH, CMP_HIDDEN), lambda c, b: (c // 2, 0, 0)),
        ],
        out_specs=[
            pl.BlockSpec((1, 1, N_CMP_PAD, rep), lambda c, b: (c, b, 0, 0)),
            pl.BlockSpec((1, 1, NSA_DH, N_CMP_PAD), lambda c, b: (c, b, 0, 0)),
        ],
        out_shape=[
            jax.ShapeDtypeStruct((4, B, N_CMP_PAD, rep), BF16),
            jax.ShapeDtypeStruct((4, B, NSA_DH, N_CMP_PAD), BF16),
        ],
        compiler_params=_cparams(("parallel", "parallel")),
        name="nsa_compress",
    )(r, pe, w1, w2_rep, w2_t)


def _nsa_kernel(q_ref, kv_ref, misc_ref, kc_ref, vct_ref, bc_ref, toe_ref, erep_ref, esel_ref, ovl_ref, o_ref,
                sk_ref, wk_ref, svt_ref, wvt_ref, ssel_ref, swin_ref, *, n_tiles):
    g = pl.program_id(1)
    J = NSA_HPG
    W = J * NSA_DH
    QW = J * TA
    n_blk = n_tiles * TA // SEL_LEN
    kv = kv_ref[...]
    sk_ref[...] = _mm(kv, erep_ref[0]).astype(BF16)
    wk_ref[...] = _mm(kv, erep_ref[1]).astype(BF16)
    ones = jnp.ones((ONES_ROWS, kv.shape[0]), BF16)
    for ref, slot in ((svt_ref, 0), (wvt_ref, 1)):
        ref[0:NSA_DH, :] = _nt(esel_ref[slot], kv).astype(BF16)
        ref[NSA_DH:, :] = ones

    head_of_lane = lax.broadcasted_iota(jnp.int32, (TA, W), 1) >> 6
    blk = lax.broadcasted_iota(jnp.int32, (n_blk, TA), 0)
    qpos = lax.broadcasted_iota(jnp.int32, (n_blk, TA), 1)
    g_is0 = g == 0

    def make_jobs(qi):
        rows = slice(qi * TA, (qi + 1) * TA)
        first = max(qi - 1, 0)
        st = {}

        def q4():
            if "q4" not in st:
                q = q_ref[rows, :]
                zero = jnp.zeros_like(q)
                st["q4"] = jnp.concatenate([jnp.where(head_of_lane == j, q, zero) for j in range(J)], axis=0)
            return st["q4"]

        def select():
            s = _nt(kc_ref[0, 0], q4()) + bc_ref[0, qi]
            e = jnp.exp2(s - jnp.max(s, axis=0, keepdims=True))
            p = e * (1.0 / jnp.sum(e, axis=0, keepdims=True))
            if qi * TA < CMP_LEN - 1:
                tq = (lax.broadcasted_iota(jnp.int32, p.shape, 1) & (TA - 1)) + qi * TA
                p = jnp.where(tq >= CMP_LEN - 1, p, 0.0)
            st["o_cmp"] = _mm(vct_ref[0, 0], p.astype(BF16))
            p_sum = p[:, 0:TA]
            for j in range(1, J):
                p_sum = p_sum + p[:, j * TA:(j + 1) * TA]
            hi, mid, lo = _split3(p_sum)
            imp = _mm(ovl_ref[...], hi) + _mm(ovl_ref[...], mid) + _mm(ovl_ref[...], lo)
            cur = (qpos + qi * TA) >> 6
            forced = (blk == 0) | (blk == cur) | (blk == cur - 1)
            score = jnp.where(forced, FORCE_SCORE, imp)
            score = jnp.where(blk <= cur, score, -1.0)
            rank = jnp.zeros((n_blk, TA), F32)
            for mp in range(n_blk):
                r = score[mp:mp + 1, :]
                beats = (r > score) | ((r == score) & (blk > mp))
                rank = rank + jnp.where(beats, 1.0, 0.0)
            pen = jnp.where(rank < SEL_TOP, 0.0, NEG_INF)
            return jnp.concatenate([pen] * J, axis=1)

        def score_sel():
            n_keys = (qi + 1) * TA
            pen4 = select()
            pen_rows = jnp.concatenate(
                [jnp.broadcast_to(pen4[m:m + 1, :], (SEL_LEN, QW)) for m in range(n_keys // SEL_LEN)], axis=0)
            s = _nt(sk_ref[0:n_keys, :], q4()) + pen_rows
            return _bias_tail(s, toe_ref[0, 1] if qi > 0 else None, toe_ref[0, 0])

        def score_win():
            s = _nt(wk_ref[first * TA:(qi + 1) * TA, :], q4())
            return _bias_tail(s, toe_ref[0, 2] if qi > 0 else None, toe_ref[0, 0])

        def finish_sel(acc, l):
            st["o_slc"] = acc * (1.0 / l)

        def finish_win(acc, l):
            o_win = acc * (1.0 / l)
            gl = misc_ref[rows, :].astype(F32)
            sig_t = (1.0 / (1.0 + jnp.exp(-gl))).T
            mixes = []
            for j in range(J):
                def gate(br, j=j):
                    r0 = 2 * MLA_ROPE + j * 3 + br
                    r1 = r0 + J * 3
                    return jnp.where(g_is0, sig_t[r0:r0 + 1, :], sig_t[r1:r1 + 1, :])

                cs = slice(j * TA, (j + 1) * TA)
                mixes.append(gate(0) * st["o_cmp"][:, cs] + gate(1) * st["o_slc"][:, cs] + gate(2) * o_win[:, cs])
            o_ref[rows, :] = jnp.concatenate(mixes, axis=0).T.astype(BF16)

        win = _SoftmaxJob(qi + 1 - first, score_win, lambda c: wvt_ref[:, (first + c) * TA:(first + c + 1) * TA],
                          swin_ref, finish_win)
        sel = _SoftmaxJob(qi + 1, score_sel, lambda c: svt_ref[:, c * TA:(c + 1) * TA], ssel_ref, finish_sel)
        return [sel, win]

    _run_interleaved([job for qi in range(n_tiles) for job in make_jobs(qi)])


def _nsa_attention(proj, kcmp, vcmp_t, bias_cmp_t, toe_g, erep, esel, ovl_t, B, S):
    J = NSA_HPG
    W = J * NSA_DH
    n_tiles = S // TA
    n_blk = S // SEL_LEN
    q_blk = NSA_Q_COL // W
    kv_blk = NSA_KV_COL // W
    misc_blk = MISC_COL // LANES
    return pl.pallas_call(
        functools.partial(_nsa_kernel, n_tiles=n_tiles),
        grid=(B, NSA_GROUPS),
        in_specs=[
            pl.BlockSpec((S, W), lambda b, g: (b, q_blk + g)),
            pl.BlockSpec((S, W), lambda b, g: (b, kv_blk + g)),
            pl.BlockSpec((S, LANES), lambda b, g: (b, misc_blk)),
            pl.BlockSpec((1, 1, N_CMP_PAD, W), lambda b, g: (g, b, 0, 0)),
            pl.BlockSpec((1, 1, NSA_DH, N_CMP_PAD), lambda b, g: (2 + g, b, 0, 0)),
            pl.BlockSpec((1, n_tiles, N_CMP_PAD, J * TA), lambda b, g: (g, 0, 0, 0)),
            pl.BlockSpec((1, 3, TA, J * TA), lambda b, g: (g, 0, 0, 0)),
            pl.BlockSpec((2, W, W), lambda b, g: (0, 0, 0)),
            pl.BlockSpec((2, NSA_DH, W), lambda b, g: (0, 0, 0)),
            pl.BlockSpec((n_blk, N_CMP_PAD), lambda b, g: (0, 0)),
        ],
        out_specs=pl.BlockSpec((S, W), lambda b, g: (b, g)),
        out_shape=jax.ShapeDtypeStruct((B * S, NSA_HEADS * NSA_DH), BF16),
        scratch_shapes=[pltpu.VMEM((S, W), BF16), pltpu.VMEM((S, W), BF16),
                        pltpu.VMEM((NSA_DH + ONES_ROWS, S), BF16), pltpu.VMEM((NSA_DH + ONES_ROWS, S), BF16),
                        pltpu.VMEM((S, J * TA), F32), pltpu.VMEM((2 * TA, J * TA), F32)],
        compiler_params=_cparams(("parallel", "parallel")),
        name="nsa_attention",
    )(proj, proj, proj, kcmp, vcmp_t, bias_cmp_t, toe_g, erep, esel, ovl_t)


def _diff_kernel(q_ref, k_ref, v_ref, toe_ref, lam_ref, sub_ref, id_ref, o_ref, vt_ref, s_ref, *, n_tiles, lam_init):
    hd = DIFF_DH
    lane = lax.broadcasted_iota(jnp.int32, (TA, 2 * hd), 1)
    lv = lam_ref[...]
    lam = (jnp.exp(jnp.sum(lv[0:1] * lv[1:2], axis=-1, keepdims=True))
           - jnp.exp(jnp.sum(lv[2:3] * lv[3:4], axis=-1, keepdims=True)) + lam_init)
    vt_ref[0:2 * hd, :] = _nt(id_ref[...], v_ref[...]).astype(BF16)
    vt_ref[2 * hd:, :] = jnp.ones((ONES_ROWS, vt_ref.shape[1]), BF16)

    def both(t):
        return jnp.concatenate([t, t], axis=1)

    def make_job(qi):
        rows = slice(qi * TA, (qi + 1) * TA)

        def score():
            q = q_ref[rows, :]
            zero = jnp.zeros_like(q)
            q2 = jnp.concatenate([jnp.where(lane < hd, q, zero), jnp.where(lane >= hd, q, zero)], axis=0)
            s = _nt(k_ref[0:(qi + 1) * TA, :], q2)
            near = both(toe_ref[0, TA:2 * TA, :]) if qi > 0 else None
            return _bias_tail(s, near, both(toe_ref[0, 0:TA, :]))

        def finish(acc, l):
            o2 = acc * (1.0 / l)
            o = (o2[:, 0:TA] - lam * o2[:, TA:2 * TA]).T
            o = o * lax.rsqrt(jnp.mean(o * o, axis=-1, keepdims=True) + RMS_EPS) * sub_ref[...]
            o_ref[rows, :] = (o * (1.0 - lam_init)).astype(BF16)

        return _SoftmaxJob(qi + 1, score, lambda c: vt_ref[:, c * TA:(c + 1) * TA], s_ref.at[qi % 2], finish)

    _run_interleaved([make_job(qi) for qi in range(n_tiles)])


def _diff_attention(proj, toe_t, lam_vec, subln, ident, layer_idx, B, S):
    H = DIFF_HEADS
    hw = 2 * DIFF_DH
    lam_init = 0.8 - 0.6 * math.exp(-0.3 * layer_idx)
    return pl.pallas_call(
        functools.partial(_diff_kernel, n_tiles=S // TA, lam_init=lam_init),
        grid=(B, H),
        in_specs=[
            pl.BlockSpec((S, hw), lambda b, h: (b, h)),
            pl.BlockSpec((S, hw), lambda b, h: (b, H + h)),
            pl.BlockSpec((S, hw), lambda b, h: (b, 2 * H + h)),
            pl.BlockSpec((1, 3 * TA, TA), lambda b, h: (h, 0, 0)),
            pl.BlockSpec((4, DIFF_DH), lambda b, h: (0, 0)),
            pl.BlockSpec((1, hw), lambda b, h: (0, 0)),
            pl.BlockSpec((hw, hw), lambda b, h: (0, 0)),
        ],
        out_specs=pl.BlockSpec((S, hw), lambda b, h: (b, h)),
        out_shape=jax.ShapeDtypeStruct((B * S, H * hw), BF16),
        scratch_shapes=[pltpu.VMEM((hw + ONES_ROWS, S), BF16), pltpu.VMEM((2, S, 2 * TA), F32)],
        compiler_params=_cparams(("parallel", "parallel")),
        name="diff_attention",
    )(proj, proj, proj, toe_t, lam_vec, subln.reshape(1, hw), ident)


def _tail_kernel(oa_ref, ob_ref, wo_ref, x_ref, g1_ref, lg1_ref, lb1_ref, sc_ref, sh_ref, g2_ref,
                 wu_ref, wg_ref, cw_ref, cb_ref, wd_ref, lg2_ref, lb2_ref, o_ref, hprev_ref, act_ref,
                 *, per_seq, tf):
    i = pl.program_id(0)
    half = oa_ref.shape[1]
    tm = x_ref.shape[0]
    y = _mm(oa_ref[...], wo_ref[:half, :]) + _mm(ob_ref[...], wo_ref[half:, :])
    x1 = _layer_norm_rows(ALPHA * x_ref[...] + (1.0 + g1_ref[0]) * y, lg1_ref[...], lb1_ref[...])
    o_ref[...] = x1
    h = (x1 * (1.0 + sc_ref[0]) + sh_ref[0]).astype(BF16)

    @pl.when(i % per_seq == 0)
    def _():
        hprev_ref[...] = jnp.zeros(hprev_ref.shape, BF16)

    h_ext = jnp.concatenate([hprev_ref[...], h], axis=0)
    hprev_ref[...] = h[tm - CONV_HALO:, :]
    for f in range(D_FF // tf):
        cs = slice(f * tf, (f + 1) * tf)
        u = _mm(h, wu_ref[:, cs])
        ge = _mm(h_ext, wg_ref[:, cs])
        cw = cw_ref[:, cs]
        a = (cw[2:3] * ge[CONV_HALO:CONV_HALO + tm] + cw[1:2] * ge[CONV_HALO - 1:CONV_HALO - 1 + tm]
             + cw[0:1] * ge[CONV_HALO - 2:CONV_HALO - 2 + tm] + cb_ref[:, cs])
        act_ref[:, cs] = (a / (1.0 + jnp.exp(-a)) * u).astype(BF16)
    y2 = _mm(act_ref[...], wd_ref[...])
    z = ALPHA * o_ref[...] + (1.0 + g2_ref[0]) * y2
    o_ref[...] = _layer_norm_rows(z, lg2_ref[...], lb2_ref[...])


def _layer_tail(o_a, o_b, blk_a, blk_b, w_o, x, g1, ln1_g, ln1_b, scale, shift, g2, w_up, w_gate, conv_w, conv_b,
                w_down, ln2_g, ln2_b, S):
    T, D = x.shape
    tm, tf = 1024, 256
    half = w_o.shape[0] // 2
    per_seq = S // tm
    once = pl.Buffered(1)

    def const(shape):
        return pl.BlockSpec(shape, lambda i: (0,) * len(shape), pipeline_mode=once)

    mod_spec = pl.BlockSpec((1, 1, D), lambda i: (i // per_seq, 0, 0))
    return pl.pallas_call(
        functools.partial(_tail_kernel, per_seq=per_seq, tf=tf),
        grid=(T // tm,),
        in_specs=[
            pl.BlockSpec((tm, half), lambda i: (i, blk_a)),
            pl.BlockSpec((tm, half), lambda i: (i, blk_b)),
            const((2 * half, D)),
            pl.BlockSpec((tm, D), lambda i: (i, 0)),
            mod_spec, const((1, D)), const((1, D)),
            mod_spec, mod_spec, mod_spec,
            const((D, D_FF)), const((D, D_FF)), const((3, D_FF)), const((1, D_FF)), const((D_FF, D)),
            const((1, D)), const((1, D)),
        ],
        out_specs=pl.BlockSpec((tm, D), lambda i: (i, 0)),
        out_shape=jax.ShapeDtypeStruct((T, D), F32),
        scratch_shapes=[pltpu.VMEM((CONV_HALO, D), BF16), pltpu.VMEM((tm, D_FF), BF16)],
        compiler_params=_cparams(("arbitrary",)),
        name="layer_tail",
    )(o_a, o_b, w_o, x, g1, ln1_g, ln1_b, scale, shift, g2, w_up, w_gate, conv_w, conv_b, w_down, ln2_g, ln2_b)


def _take_columns(w, src, mul):
    parts = []
    start = 0
    n = len(src)
    while start < n:
        end = start + 1
        while end < n and mul[end] == mul[start] and (mul[start] == 0.0 or src[end] == src[end - 1] + 1):
            end += 1
        if mul[start] == 0.0:
            parts.append(jnp.zeros((w.shape[0], end - start), w.dtype))
        else:
            piece = w[:, int(src[start]):int(src[start]) + end - start]
            parts.append(piece if mul[start] == 1.0 else piece * float(mul[start]))
        start = end
    return jnp.concatenate(parts, axis=1)


def _even_in_columns():
    src = np.zeros(EVEN_W, np.int32)
    mul = np.zeros(EVEN_W, np.float32)

    def put(dst, cols, scale=1.0):
        cols = np.asarray(cols)
        src[dst:dst + len(cols)] = cols
        mul[dst:dst + len(cols)] = scale

    put(0, np.arange(MLA_Q_RANK + MLA_KV_RANK))
    rope0 = MLA_Q_RANK + MLA_KV_RANK
    half = MLA_ROPE // 2
    put(MISC_COL, rope0 + np.arange(MLA_ROPE))
    put(MISC_COL + MLA_ROPE, rope0 + half + np.arange(half), -1.0)
    put(MISC_COL + MLA_ROPE + half, rope0 + np.arange(half), 1.0)
    nsa0 = MLA_COLS
    put(MISC_COL + 2 * MLA_ROPE, nsa0 + NSA_Q_COLS + NSA_KV_COLS + np.arange(NSA_GATE_COLS))
    put(NSA_Q_COL, nsa0 + np.arange(NSA_Q_COLS), NSA_DH ** -0.5 * LOG2E)
    kv0 = nsa0 + NSA_Q_COLS

    def chunk(branch, kv, g):
        return kv0 + ((branch * 2 + kv) * NSA_GROUPS + g) * NSA_DH + np.arange(NSA_DH)

    for g in range(NSA_GROUPS):
        base = NSA_KV_COL + g * 4 * NSA_DH
        for slot, (branch, kv) in enumerate([(1, 0), (2, 0), (1, 1), (2, 1)]):
            put(base + slot * NSA_DH, chunk(branch, kv, g))
    put(NSA_CMP_COL, kv0 + np.arange(2 * NSA_GROUPS * NSA_DH))
    return src, mul


def _mla_up_columns():
    hw = MLA_HEADS * LANES
    qd = MLA_NOPE + MLA_ROPE
    half = MLA_ROPE // 2
    src = np.zeros(2 * hw, np.int32)
    mul = np.zeros(2 * hw, np.float32)
    for h in range(MLA_HEADS):
        src[h * LANES:h * LANES + qd] = h * qd + np.arange(qd)
        mul[h * LANES:h * LANES + qd] = 1.0
        r = hw + h * LANES + MLA_NOPE
        src[r:r + half] = h * qd + MLA_NOPE + half + np.arange(half)
        mul[r:r + half] = -1.0
        src[r + half:r + MLA_ROPE] = h * qd + MLA_NOPE + np.arange(half)
        mul[r + half:r + MLA_ROPE] = 1.0
    ksrc = np.zeros(hw, np.int32)
    kmul = np.zeros(hw, np.float32)
    vsrc = np.zeros(MLA_HEADS * MLA_V, np.int32)
    for h in range(MLA_HEADS):
        ksrc[h * LANES:h * LANES + MLA_NOPE] = h * (MLA_NOPE + MLA_V) + np.arange(MLA_NOPE)
        kmul[h * LANES:h * LANES + MLA_NOPE] = 1.0
        vsrc[h * MLA_V:(h + 1) * MLA_V] = h * (MLA_NOPE + MLA_V) + MLA_NOPE + np.arange(MLA_V)
    return src, mul, ksrc, kmul, vsrc


def _routing_constants(S):
    hw = MLA_HEADS * LANES
    erope = np.zeros((LANES, hw), np.float32)
    for h in range(MLA_HEADS):
        for i in range(MLA_ROPE):
            erope[i, h * LANES + MLA_NOPE + i] = 1.0
            erope[MLA_ROPE + i, h * LANES + MLA_NOPE + i] = 1.0
    W = NSA_HPG * NSA_DH
    erep = np.zeros((2, W, W), np.float32)
    for slot in range(2):
        for d in range(NSA_DH):
            for j in range(NSA_HPG):
                erep[slot, slot * NSA_DH + d, j * NSA_DH + d] = 1.0
    esel = np.zeros((2, NSA_DH, W), np.float32)
    for slot in range(2):
        esel[slot, np.arange(NSA_DH), (2 + slot) * NSA_DH + np.arange(NSA_DH)] = 1.0
    n_slc = S // SEL_LEN
    starts = np.arange(N_CMP_PAD) * CMP_STRIDE
    jb = np.arange(n_slc)
    ovl = ((starts[:, None] < (jb[None, :] + 1) * SEL_LEN)
           & (starts[:, None] + CMP_LEN > jb[None, :] * SEL_LEN)).astype(np.float32)
    ovl[(S - CMP_LEN) // CMP_STRIDE + 1:, :] = 0.0
    cmask_t = np.where(np.arange(TA)[None, :] >= np.arange(TA)[:, None], 0.0, NEG_INF).astype(np.float32)
    return erope, erep, esel, np.ascontiguousarray(ovl.T), cmask_t


def _rope_tables(S):
    inv = 1.0 / (ROPE_BASE ** (jnp.arange(0, MLA_ROPE, 2, dtype=jnp.float32) / MLA_ROPE))
    ang = jnp.arange(S, dtype=jnp.float32)[:, None] * inv[None, :]
    cos, sin = jnp.cos(ang), jnp.sin(ang)
    cos2 = jnp.concatenate([cos, cos], axis=-1)
    sin2 = jnp.concatenate([sin, sin], axis=-1)
    scale = (MLA_NOPE + MLA_ROPE) ** -0.5 * LOG2E
    z32 = jnp.zeros((S, LANES - MLA_NOPE - MLA_ROPE), F32)
    ctab = jnp.concatenate([jnp.full((S, MLA_NOPE), scale, F32), cos2 * scale, z32], axis=-1)
    stab = jnp.concatenate([jnp.zeros((S, MLA_NOPE), F32), sin2 * scale, z32], axis=-1)
    mrope = jnp.concatenate([cos2, sin2, jnp.zeros((S, LANES - 2 * MLA_ROPE), F32)], axis=-1)
    return ctab, stab, mrope


def kernel(x, c, rel_bias, ev_w_in, mla_q_norm, mla_kv_norm, mla_w_uq, mla_w_ukv, nsa_cmp_pe, nsa_cmp_w1,
           nsa_cmp_w2, ev_w_o, od_w_in, diff_lambda, diff_subln, od_w_o, ada_w, ada_b, ln_g, ln_b, ffn_w_up,
           ffn_w_gate, ffn_conv_w, ffn_conv_b, ffn_w_down):
    B, S, D = x.shape
    assert D == D_MODEL and S % TA == 0 and S // CMP_STRIDE == N_CMP_PAD and S // SEL_LEN <= LANES
    far_np = np.arange(TA + 1, max(S, TA + 2))
    far_bucket = 16 + (np.log(far_np.astype(np.float32) / 16) / math.log(MAX_DISTANCE / 16) * 16).astype(np.int32)
    assert far_bucket.min() >= FAR_BUCKET
    T = B * S
    xf = x.reshape(T, D)

    mod = _ada_mod(c, ada_w, ada_b)
    toe_t_idx, cmp_t_idx = _bias_index_tiles(S)
    toe_t = _expand_bias(rel_bias, toe_t_idx, rel_far=True)
    bias_cmp_t = _expand_bias(rel_bias, cmp_t_idx)
    G, J, NQ = NSA_GROUPS, NSA_HPG, S // TA
    toe_g = toe_t.reshape(G, J, 3, TA, TA).transpose(0, 2, 3, 1, 4).reshape(G, 3, TA, J * TA)
    bias_cmp_t = bias_cmp_t.reshape(G, J, N_CMP_PAD, NQ, TA).transpose(0, 3, 2, 1, 4).reshape(G, NQ, N_CMP_PAD, J * TA)
    erope, erep, esel, ovl_t, cmask_t = _routing_constants(S)
    ident = jnp.eye(LANES, dtype=BF16)
    ctab, stab, mrope = _rope_tables(S)
    ev_src, ev_mul = _even_in_columns()
    q_src, q_mul, k_src, k_mul, v_src = _mla_up_columns()

    def bf(a):
        return jnp.asarray(a).astype(BF16)

    for l in range(DEPTH):
        sh1, sc1, g1, sh2, sc2, g2 = [mod[l, :, k * D:(k + 1) * D].reshape(B, 1, D) for k in range(6)]
        i = l // 2
        if l % 2 == 0:
            w_in = bf(_take_columns(ev_w_in[i], ev_src, ev_mul))
            proj, cmp_planes = _mod_matmul(xf, sc1, sh1, w_in, S, planes_col=NSA_CMP_COL)
            wq = bf(_take_columns(mla_w_uq[i], q_src, q_mul))
            wk = bf(_take_columns(mla_w_ukv[i], k_src, k_mul))
            wv = bf(_take_columns(mla_w_ukv[i], v_src, np.ones(len(v_src), np.float32)))
            q_m, k_m, v_m = _mla_prep(proj, mla_q_norm[i].reshape(1, -1), mla_kv_norm[i].reshape(1, -1),
                                      wq, wk, wv, bf(erope), ctab, stab, mrope, S)
            o_1 = _mla_attention(q_m, k_m, v_m, jnp.asarray(cmask_t), ident, B, S)
            r = cmp_planes.reshape(4, B, N_CMP_PAD, CMP_STRIDE * NSA_DH)
            w2_rep = jnp.tile(nsa_cmp_w2[i], (1, 1, NSA_HPG))
            kcmp, vcmp_t = _compress(r, nsa_cmp_pe[i].reshape(2, 1, CMP_LEN * NSA_DH), bf(nsa_cmp_w1[i]), bf(w2_rep),
                                     bf(nsa_cmp_w2[i].transpose(0, 2, 1)), B)
            o_2 = _nsa_attention(proj, kcmp, vcmp_t, bias_cmp_t, toe_g, bf(erep), bf(esel), bf(ovl_t), B, S)
            blk_2, w_o = 0, ev_w_o[i]
        else:
            w_in = od_w_in[i].at[:, :DIFF_HEADS * 2 * DIFF_DH].multiply(DIFF_DH ** -0.5 * LOG2E)
            proj, = _mod_matmul(xf, sc1, sh1, bf(w_in), S)
            o_1 = _diff_attention(proj, toe_t, diff_lambda[i], diff_subln[i], ident, l, B, S)
            o_2, blk_2, w_o = o_1, 1, od_w_o[i]
        xf = _layer_tail(o_1, o_2, 0, blk_2, bf(w_o), xf, g1, ln_g[l, 0:1], ln_b[l, 0:1], sc2, sh2, g2,
                         bf(ffn_w_up[l]), bf(ffn_w_gate[l]), ffn_conv_w[l], ffn_conv_b[l].reshape(1, D_FF),
                         bf(ffn_w_down[l]), ln_g[l, 1:2], ln_b[l, 1:2], S)
    return xf.reshape(B, S, D)
```

```python
import functools
import math

import numpy as np
import jax
import jax.numpy as jnp
from jax import lax
from jax.experimental import pallas as pl
from jax.experimental.pallas import tpu as pltpu

F32 = jnp.float32
BF16 = jnp.bfloat16

D_MODEL = 1024
DEPTH = 4
N_BUCKETS = 32
MAX_DISTANCE = 128
NEG_INF = -1e30

MLA_HEADS = 8
MLA_Q_RANK = 384
MLA_KV_RANK = 256
MLA_NOPE = 64
MLA_ROPE = 32
MLA_V = 64
ROPE_BASE = 10000.0
MLA_COLS = MLA_Q_RANK + MLA_KV_RANK + MLA_ROPE

NSA_HEADS = 8
NSA_GROUPS = 2
NSA_HPG = NSA_HEADS // NSA_GROUPS
NSA_DH = 64
CMP_LEN = 32
CMP_STRIDE = 16
CMP_HIDDEN = 256
SEL_LEN = 64
SEL_TOP = 8
FORCE_SCORE = 1e4
WINDOW = 256
NSA_Q_COLS = NSA_HEADS * NSA_DH
NSA_KV_COLS = 3 * 2 * NSA_GROUPS * NSA_DH
NSA_GATE_COLS = 3 * NSA_HEADS

DIFF_HEADS = 8
DIFF_DH = 64

D_FF = 2816
ALPHA = (2.0 * DEPTH) ** 0.25
LN_EPS = 1e-5
RMS_EPS = 1e-6

LANES = 128
VMEM_LIMIT = 56 * 1024 * 1024

TA = 256
EVEN_W = 2048
MISC_COL = 640
NSA_Q_COL = 768
NSA_KV_COL = 1280
NSA_CMP_COL = 1792
N_CMP_PAD = 128
MASK_BUCKET = N_BUCKETS
FAR_BUCKET = N_BUCKETS - 1
LOG2E = math.log2(math.e)
ONES_ROWS = 16
CONV_HALO = 16


def _cparams(sem):
    return pltpu.CompilerParams(dimension_semantics=sem, vmem_limit_bytes=VMEM_LIMIT)


def _nt(a, b):
    return lax.dot_general(a, b, (((1,), (1,)), ((), ())), preferred_element_type=F32)


def _mm(a, b):
    return jnp.dot(a, b, preferred_element_type=F32)


def _split3(v):
    hi = v.astype(BF16)
    r1 = v - hi.astype(F32)
    mid = r1.astype(BF16)
    lo = (r1 - mid.astype(F32)).astype(BF16)
    return hi, mid, lo


def _layer_norm_rows(z, g, b):
    mu = jnp.mean(z, axis=-1, keepdims=True)
    zc = z - mu
    var = jnp.mean(zc * zc, axis=-1, keepdims=True)
    return zc * lax.rsqrt(var + LN_EPS) * g + b


def _bias_tail(s, near, diag):
    n = s.shape[0]
    parts = []
    if near is not None:
        if n > 2 * TA:
            parts.append(s[:n - 2 * TA])
        parts.append(s[n - 2 * TA:n - TA] + near)
    elif n > TA:
        parts.append(s[:n - TA])
    parts.append(s[n - TA:] + diag)
    return parts[0] if len(parts) == 1 else jnp.concatenate(parts, axis=0)


class _SoftmaxJob:
    def __init__(self, n_chunks, scores, v_t_chunk, s_buf, finish):
        self.n, self.scores, self.v_t_chunk, self.s_buf, self.finish = n_chunks, scores, v_t_chunk, s_buf, finish
        self.m = self.acc = None

    def score_pass(self):
        s = self.scores()
        self.s_buf[0:self.n * TA, :] = s
        m8 = jnp.max(s.reshape(self.n * TA // 8, 8, s.shape[1]), axis=0)
        self.m = jnp.max(m8, axis=0, keepdims=True)

    def value_step(self, i):
        p = jnp.exp2((self.s_buf[i * TA:(i + 1) * TA, :] - self.m).astype(BF16))
        pv = _mm(self.v_t_chunk(i), p)
        self.acc = pv if self.acc is None else self.acc + pv

    def done(self):
        dv = self.acc.shape[0] - ONES_ROWS
        self.finish(self.acc[:dv], self.acc[dv:dv + 1])


def _run_interleaved(jobs):
    jobs[0].score_pass()
    for k, job in enumerate(jobs):
        if k + 1 < len(jobs):
            jobs[k + 1].score_pass()
        for i in range(job.n):
            job.value_step(i)
        job.done()


def _ada_kernel(c_ref, w_ref, b_ref, o_ref):
    c = c_ref[...]
    ca = c / (1.0 + jnp.exp(-c))
    w = w_ref[0]
    c_hi = ca.astype(BF16)
    c_lo = (ca - c_hi.astype(F32)).astype(BF16)
    w_hi = w.astype(BF16)
    w_lo = (w - w_hi.astype(F32)).astype(BF16)
    o_ref[0] = _mm(c_hi, w_hi) + _mm(c_hi, w_lo) + _mm(c_lo, w_hi) + b_ref[0]


def _ada_mod(c, ada_w, ada_b):
    B, D = c.shape
    n_out = ada_w.shape[-1]
    tn = 1024
    return pl.pallas_call(
        _ada_kernel,
        grid=(DEPTH, n_out // tn),
        in_specs=[
            pl.BlockSpec((B, D), lambda l, j: (0, 0)),
            pl.BlockSpec((1, D, tn), lambda l, j: (l, 0, j)),
            pl.BlockSpec((1, 1, tn), lambda l, j: (l, 0, j)),
        ],
        out_specs=pl.BlockSpec((1, B, tn), lambda l, j: (l, 0, j)),
        out_shape=jax.ShapeDtypeStruct((DEPTH, B, n_out), F32),
        compiler_params=_cparams(("parallel", "parallel")),
        name="ada_mod",
    )(c, ada_w, ada_b.reshape(DEPTH, 1, n_out))


def _expand_kernel(tbl_ref, idx_ref, o_ref, *, rel_far):
    h = pl.program_id(0)
    idx = idx_ref[...]
    acc = jnp.full(idx.shape, NEG_INF, F32)
    base = tbl_ref[FAR_BUCKET, h] if rel_far else 0.0
    for b in range(N_BUCKETS):
        acc = jnp.where(idx == b, (tbl_ref[b, h] - base) * LOG2E, acc)
    o_ref[0] = acc


def _expand_bias(table, idx, rel_far=False):
    R, C = idx.shape
    rt = min(R, 256)
    return pl.pallas_call(
        functools.partial(_expand_kernel, rel_far=rel_far),
        grid=(table.shape[1], R // rt),
        in_specs=[
            pl.BlockSpec(memory_space=pltpu.SMEM),
            pl.BlockSpec((rt, C), lambda h, r: (r, 0)),
        ],
        out_specs=pl.BlockSpec((1, rt, C), lambda h, r: (h, r, 0)),
        out_shape=jax.ShapeDtypeStruct((table.shape[1], R, C), F32),
        compiler_params=_cparams(("parallel", "parallel")),
        name="expand_bias",
    )(table, idx)


def _t5_bucket(dist):
    n = jnp.maximum(dist, 0)
    max_exact = N_BUCKETS // 2
    nf = jnp.maximum(n, 1).astype(jnp.float32)
    large = max_exact + (jnp.log(nf / max_exact) / math.log(MAX_DISTANCE / max_exact)
                         * (N_BUCKETS - max_exact)).astype(jnp.int32)
    large = jnp.minimum(large, N_BUCKETS - 1)
    return jnp.where(n < max_exact, n, large)


def _bias_index_tiles(S):
    a = np.arange(TA)[None, :]
    b = np.arange(TA)[:, None]
    d0 = a - b
    d1 = TA + a - b
    t0 = jnp.where(jnp.asarray(d0 >= 0), _t5_bucket(jnp.asarray(d0)), MASK_BUCKET)
    t1 = _t5_bucket(jnp.asarray(d1))
    t1w = jnp.where(jnp.asarray(d1 < WINDOW), t1, MASK_BUCKET)
    toe_t = jnp.concatenate([t0, t1, t1w], axis=0).astype(jnp.int32)
    t_pos = np.arange(S)[None, :]
    n = np.arange(N_CMP_PAD)[:, None]
    d_cmp = t_pos - (n * CMP_STRIDE + CMP_LEN - 1)
    cmp_t = jnp.where(jnp.asarray(d_cmp >= 0), _t5_bucket(jnp.asarray(d_cmp)), MASK_BUCKET).astype(jnp.int32)
    return toe_t, cmp_t


def _modmm_kernel(x_ref, sc_ref, sh_ref, w_ref, o_ref, *rest, tn, planes_col):
    h = (x_ref[...] * (1.0 + sc_ref[0]) + sh_ref[0]).astype(BF16)
    for j in range(w_ref.shape[1] // tn):
        cs = slice(j * tn, (j + 1) * tn)
        res = _mm(h, w_ref[:, cs]).astype(BF16)
        o_ref[:, cs] = res
        if planes_col is not None and j * tn <= planes_col < (j + 1) * tn:
            planes_ref = rest[0]
            for c in range(planes_ref.shape[0]):
                lo = planes_col - j * tn + c * NSA_DH
                planes_ref[c] = res[:, lo:lo + NSA_DH]


def _mod_matmul(x, scale, shift, w, S, planes_col=None):
    T, D = x.shape
    N = w.shape[1]
    tm, tn = 1024, 512
    per_seq = S // tm
    out_specs = [pl.BlockSpec((tm, N), lambda i: (i, 0))]
    out_shape = [jax.ShapeDtypeStruct((T, N), BF16)]
    if planes_col is not None:
        assert planes_col // tn == (planes_col + 4 * NSA_DH - 1) // tn
        out_specs.append(pl.BlockSpec((4, tm, NSA_DH), lambda i: (0, i, 0)))
        out_shape.append(jax.ShapeDtypeStruct((4, T, NSA_DH), BF16))
    return pl.pallas_call(
        functools.partial(_modmm_kernel, tn=tn, planes_col=planes_col),
        grid=(T // tm,),
        in_specs=[
            pl.BlockSpec((tm, D), lambda i: (i, 0)),
            pl.BlockSpec((1, 1, D), lambda i: (i // per_seq, 0, 0)),
            pl.BlockSpec((1, 1, D), lambda i: (i // per_seq, 0, 0)),
            pl.BlockSpec((D, N), lambda i: (0, 0), pipeline_mode=pl.Buffered(1)),
        ],
        out_specs=out_specs,
        out_shape=out_shape,
        compiler_params=_cparams(("parallel",)),
        name="mod_matmul",
    )(x, scale, shift, w)


def _mla_prep_kernel(p_ref, qn_ref, kvn_ref, wq_ref, wk_ref, wv_ref, er_ref, ct_ref, st_ref, mr_ref,
                     q_ref, k_ref, v_ref):
    p = p_ref[...]
    ql = p[:, :MLA_Q_RANK].astype(F32)
    kl = p[:, MLA_Q_RANK:MLA_Q_RANK + MLA_KV_RANK].astype(F32)
    misc = p[:, MISC_COL:MISC_COL + LANES].astype(F32)
    c_q = (ql * lax.rsqrt(jnp.mean(ql * ql, axis=-1, keepdims=True) + RMS_EPS) * qn_ref[...]).astype(BF16)
    c_kv = (kl * lax.rsqrt(jnp.mean(kl * kl, axis=-1, keepdims=True) + RMS_EPS) * kvn_ref[...]).astype(BF16)
    ab = _mm(c_q, wq_ref[...])
    ct = ct_ref[...]
    st = st_ref[...]
    half = MLA_HEADS * LANES
    for h in range(MLA_HEADS):
        lo = h * LANES
        q_ref[:, lo:lo + LANES] = (ab[:, lo:lo + LANES] * ct + ab[:, half + lo:half + lo + LANES] * st).astype(BF16)
    k_rope = _mm((misc * mr_ref[...]).astype(BF16), er_ref[...])
    k_ref[...] = (_mm(c_kv, wk_ref[...]) + k_rope).astype(BF16)
    v_ref[...] = _mm(c_kv, wv_ref[...]).astype(BF16)


def _mla_prep(proj, qn, kvn, wq, wk, wv, erope, ctab, stab, mrope, S):
    T = proj.shape[0]
    tm = 512
    per_seq = S // tm
    hw = MLA_HEADS * LANES
    const = lambda i: (0, 0)
    seq = lambda i: (i % per_seq, 0)
    return pl.pallas_call(
        _mla_prep_kernel,
        grid=(T // tm,),
        in_specs=[
            pl.BlockSpec((tm, NSA_Q_COL), lambda i: (i, 0)),
            pl.BlockSpec((1, MLA_Q_RANK), const),
            pl.BlockSpec((1, MLA_KV_RANK), const),
            pl.BlockSpec((MLA_Q_RANK, 2 * hw), const),
            pl.BlockSpec((MLA_KV_RANK, hw), const),
            pl.BlockSpec((MLA_KV_RANK, MLA_HEADS * MLA_V), const),
            pl.BlockSpec((LANES, hw), const),
            pl.BlockSpec((tm, LANES), seq),
            pl.BlockSpec((tm, LANES), seq),
            pl.BlockSpec((tm, LANES), seq),
        ],
        out_specs=[
            pl.BlockSpec((tm, hw), lambda i: (i, 0)),
            pl.BlockSpec((tm, hw), lambda i: (i, 0)),
            pl.BlockSpec((tm, MLA_HEADS * MLA_V), lambda i: (i, 0)),
        ],
        out_shape=[
            jax.ShapeDtypeStruct((T, hw), BF16),
            jax.ShapeDtypeStruct((T, hw), BF16),
            jax.ShapeDtypeStruct((T, MLA_HEADS * MLA_V), BF16),
        ],
        compiler_params=_cparams(("parallel",)),
        name="mla_prep",
    )(proj, qn, kvn, wq, wk, wv, erope, ctab, stab, mrope)


def _mla_attn_kernel(q_ref, k_ref, v_ref, cm_ref, id_ref, o_ref, vt_ref, s_ref, *, n_tiles):
    hv = MLA_V + ONES_ROWS
    vt = _nt(id_ref[...], v_ref[...]).astype(BF16)
    ones = jnp.ones((ONES_ROWS, vt.shape[1]), BF16)
    for hh in range(2):
        vt_ref[hh * hv:hh * hv + MLA_V, :] = vt[hh * MLA_V:(hh + 1) * MLA_V]
        vt_ref[hh * hv + MLA_V:(hh + 1) * hv, :] = ones
    class _PairJob(_SoftmaxJob):
        def value_step(self, i):
            p = jnp.exp2((self.s_buf[i * TA:(i + 1) * TA, :] - self.m).astype(BF16))
            vts = self.v_t_chunk(i)
            pv = jnp.concatenate([_mm(vts[0], p[:, :TA]), _mm(vts[1], p[:, TA:])], axis=1)
            self.acc = pv if self.acc is None else self.acc + pv

    def make_job(qi):
        rows = slice(qi * TA, (qi + 1) * TA)

        def score():
            parts = []
            for hh in range(2):
                cs = slice(hh * LANES, (hh + 1) * LANES)
                s = _nt(k_ref[0:(qi + 1) * TA, cs], q_ref[rows, cs])
                parts.append(_bias_tail(s, None, cm_ref[...]))
            return jnp.concatenate(parts, axis=1)

        def finish(acc, l):
            o = acc * (1.0 / l)
            o_ref[rows, :] = jnp.concatenate([o[:, :TA], o[:, TA:]], axis=0).T.astype(BF16)

        def v_t(c):
            return [vt_ref[hh * hv:(hh + 1) * hv, c * TA:(c + 1) * TA] for hh in range(2)]

        return _PairJob(qi + 1, score, v_t, s_ref.at[qi % 2], finish)

    _run_interleaved([make_job(qi) for qi in range(n_tiles)])


def _mla_attention(q, k, v, cmask_t, ident, B, S):
    pairs = MLA_HEADS // 2
    return pl.pallas_call(
        functools.partial(_mla_attn_kernel, n_tiles=S // TA),
        grid=(B, pairs),
        in_specs=[
            pl.BlockSpec((S, 2 * LANES), lambda b, hp: (b, hp)),
            pl.BlockSpec((S, 2 * LANES), lambda b, hp: (b, hp)),
            pl.BlockSpec((S, LANES), lambda b, hp: (b, hp)),
            pl.BlockSpec((TA, TA), lambda b, hp: (0, 0)),
            pl.BlockSpec((LANES, LANES), lambda b, hp: (0, 0)),
        ],
        out_specs=pl.BlockSpec((S, LANES), lambda b, hp: (b, hp)),
        out_shape=jax.ShapeDtypeStruct((B * S, MLA_HEADS * MLA_V), BF16),
        scratch_shapes=[pltpu.VMEM((2 * (MLA_V + ONES_ROWS), S), BF16), pltpu.VMEM((2, S, 2 * TA), F32)],
        compiler_params=_cparams(("parallel", "parallel")),
        name="mla_attention",
    )(q, k, v, cmask_t, ident)


def _compress_kernel(r_ref, pe_ref, w1_ref, w2_ref, w2t_ref, o_ref, ot_ref):
    half = CMP_STRIDE * NSA_DH
    r = r_ref[0, 0].astype(F32)
    pe = pe_ref[0]
    x_lo = (r + pe[:, :half]).astype(BF16)
    x_hi = (r + pe[:, half:]).astype(BF16)
    y_lo = _mm(x_lo, w1_ref[0, :half, :])
    y_hi = _mm(x_hi, w1_ref[0, half:, :])
    y = y_lo + pltpu.roll(y_hi, N_CMP_PAD - 1, 0)
    hid = jax.nn.gelu(y).astype(BF16)
    o_ref[0, 0] = _mm(hid, w2_ref[0]).astype(BF16)
    ot_ref[0, 0] = _nt(w2t_ref[0], hid).astype(BF16)


def _compress(r, pe, w1, w2_rep, w2_t, B):
    rep = NSA_HPG * NSA_DH
    return pl.pallas_call(
        _compress_kernel,
        grid=(4, B),
        in_specs=[
            pl.BlockSpec((1, 1, N_CMP_PAD, CMP_STRIDE * NSA_DH), lambda c, b: (c, b, 0, 0)),
            pl.BlockSpec((1, 1, CMP_LEN * NSA_DH), lambda c, b: (c // 2, 0, 0)),
            pl.BlockSpec((1, CMP_LEN * NSA_DH, CMP_HIDDEN), lambda c, b: (c // 2, 0, 0)),
            pl.BlockSpec((1, CMP_HIDDEN, rep), lambda c, b: (c // 2, 0, 0)),
            pl.BlockSpec((1, NSA_DH, CMP_HIDDEN), lambda c, b: (c // 2, 0, 0)),
        ],
        out_specs=[
            pl.BlockSpec((1, 1, N_CMP_PAD, rep), lambda c, b: (c, b, 0, 0)),
            pl.BlockSpec((1, 1, NSA_DH, N_CMP_PAD), lambda c, b: (c, b, 0, 0)),
        ],
        out_shape=[
            jax.ShapeDtypeStruct((4, B, N_CMP_PAD, rep), BF16),
            jax.ShapeDtypeStruct((4, B, NSA_DH, N_CMP_PAD), BF16),
        ],
        compiler_params=_cparams(("parallel", "parallel")),
        name="nsa_compress",
    )(r, pe, w1, w2_rep, w2_t)


def _nsa_kernel(q_ref, kv_ref, misc_ref, kc_ref, vct_ref, bc_ref, toe_ref, erep_ref, esel_ref, ovl_ref, o_ref,
                sk_ref, wk_ref, svt_ref, wvt_ref, ssel_ref, swin_ref, *, n_tiles):
    g = pl.program_id(1)
    J = NSA_HPG
    W = J * NSA_DH
    QW = J * TA
    n_blk = n_tiles * TA // SEL_LEN
    kv = kv_ref[...]
    sk_ref[...] = _mm(kv, erep_ref[0]).astype(BF16)
    wk_ref[...] = _mm(kv, erep_ref[1]).astype(BF16)
    ones = jnp.ones((ONES_ROWS, kv.shape[0]), BF16)
    for ref, slot in ((svt_ref, 0), (wvt_ref, 1)):
        ref[0:NSA_DH, :] = _nt(esel_ref[slot], kv).astype(BF16)
        ref[NSA_DH:, :] = ones

    head_of_lane = lax.broadcasted_iota(jnp.int32, (TA, W), 1) >> 6
    blk = lax.broadcasted_iota(jnp.int32, (n_blk, TA), 0)
    qpos = lax.broadcasted_iota(jnp.int32, (n_blk, TA), 1)
    g_is0 = g == 0

    def make_jobs(qi):
        rows = slice(qi * TA, (qi + 1) * TA)
        first = max(qi - 1, 0)
        st = {}

        def q4():
            if "q4" not in st:
                q = q_ref[rows, :]
                zero = jnp.zeros_like(q)
                st["q4"] = jnp.concatenate([jnp.where(head_of_lane == j, q, zero) for j in range(J)], axis=0)
            return st["q4"]

        def select():
            s = _nt(kc_ref[0, 0], q4()) + bc_ref[0, qi]
            e = jnp.exp2(s - jnp.max(s, axis=0, keepdims=True))
            p = e * (1.0 / jnp.sum(e, axis=0, keepdims=True))
            if qi * TA < CMP_LEN - 1:
                tq = (lax.broadcasted_iota(jnp.int32, p.shape, 1) & (TA - 1)) + qi * TA
                p = jnp.where(tq >= CMP_LEN - 1, p, 0.0)
            st["o_cmp"] = _mm(vct_ref[0, 0], p.astype(BF16))
            p_sum = p[:, 0:TA]
            for j in range(1, J):
                p_sum = p_sum + p[:, j * TA:(j + 1) * TA]
            hi, mid, lo = _split3(p_sum)
            imp = _mm(ovl_ref[...], hi) + _mm(ovl_ref[...], mid) + _mm(ovl_ref[...], lo)
            cur = (qpos + qi * TA) >> 6
            forced = (blk == 0) | (blk == cur) | (blk == cur - 1)
            score = jnp.where(forced, FORCE_SCORE, imp)
            score = jnp.where(blk <= cur, score, -1.0)
            rank = jnp.zeros((n_blk, TA), F32)
            for mp in range(n_blk):
                r = score[mp:mp + 1, :]
                beats = (r > score) | ((r == score) & (blk > mp))
                rank = rank + jnp.where(beats, 1.0, 0.0)
            pen = jnp.where(rank < SEL_TOP, 0.0, NEG_INF)
            return jnp.concatenate([pen] * J, axis=1)

        def score_sel():
            n_keys = (qi + 1) * TA
            pen4 = select()
            pen_rows = jnp.concatenate(
                [jnp.broadcast_to(pen4[m:m + 1, :], (SEL_LEN, QW)) for m in range(n_keys // SEL_LEN)], axis=0)
            s = _nt(sk_ref[0:n_keys, :], q4()) + pen_rows
            return _bias_tail(s, toe_ref[0, 1] if qi > 0 else None, toe_ref[0, 0])

        def score_win():
            s = _nt(wk_ref[first * TA:(qi + 1) * TA, :], q4())
            return _bias_tail(s, toe_ref[0, 2] if qi > 0 else None, toe_ref[0, 0])

        def finish_sel(acc, l):
            st["o_slc"] = acc * (1.0 / l)

        def finish_win(acc, l):
            o_win = acc * (1.0 / l)
            gl = misc_ref[rows, :].astype(F32)
            sig_t = (1.0 / (1.0 + jnp.exp(-gl))).T
            mixes = []
            for j in range(J):
                def gate(br, j=j):
                    r0 = 2 * MLA_ROPE + j * 3 + br
                    r1 = r0 + J * 3
                    return jnp.where(g_is0, sig_t[r0:r0 + 1, :], sig_t[r1:r1 + 1, :])

                cs = slice(j * TA, (j + 1) * TA)
                mixes.append(gate(0) * st["o_cmp"][:, cs] + gate(1) * st["o_slc"][:, cs] + gate(2) * o_win[:, cs])
            o_ref[rows, :] = jnp.concatenate(mixes, axis=0).T.astype(BF16)

        win = _SoftmaxJob(qi + 1 - first, score_win, lambda c: wvt_ref[:, (first + c) * TA:(first + c + 1) * TA],
                          swin_ref, finish_win)
        sel = _SoftmaxJob(qi + 1, score_sel, lambda c: svt_ref[:, c * TA:(c + 1) * TA], ssel_ref, finish_sel)
        return [sel, win]

    _run_interleaved([job for qi in range(n_tiles) for job in make_jobs(qi)])


def _nsa_attention(proj, kcmp, vcmp_t, bias_cmp_t, toe_g, erep, esel, ovl_t, B, S):
    J = NSA_HPG
    W = J * NSA_DH
    n_tiles = S // TA
    n_blk = S // SEL_LEN
    q_blk = NSA_Q_COL // W
    kv_blk = NSA_KV_COL // W
    misc_blk = MISC_COL // LANES
    return pl.pallas_call(
        functools.partial(_nsa_kernel, n_tiles=n_tiles),
        grid=(B, NSA_GROUPS),
        in_specs=[
            pl.BlockSpec((S, W), lambda b, g: (b, q_blk + g)),
            pl.BlockSpec((S, W), lambda b, g: (b, kv_blk + g)),
            pl.BlockSpec((S, LANES), lambda b, g: (b, misc_blk)),
            pl.BlockSpec((1, 1, N_CMP_PAD, W), lambda b, g: (g, b, 0, 0)),
            pl.BlockSpec((1, 1, NSA_DH, N_CMP_PAD), lambda b, g: (2 + g, b, 0, 0)),
            pl.BlockSpec((1, n_tiles, N_CMP_PAD, J * TA), lambda b, g: (g, 0, 0, 0)),
            pl.BlockSpec((1, 3, TA, J * TA), lambda b, g: (g, 0, 0, 0)),
            pl.BlockSpec((2, W, W), lambda b, g: (0, 0, 0)),
            pl.BlockSpec((2, NSA_DH, W), lambda b, g: (0, 0, 0)),
            pl.BlockSpec((n_blk, N_CMP_PAD), lambda b, g: (0, 0)),
        ],
        out_specs=pl.BlockSpec((S, W), lambda b, g: (b, g)),
        out_shape=jax.ShapeDtypeStruct((B * S, NSA_HEADS * NSA_DH), BF16),
        scratch_shapes=[pltpu.VMEM((S, W), BF16), pltpu.VMEM((S, W), BF16),
                        pltpu.VMEM((NSA_DH + ONES_ROWS, S), BF16), pltpu.VMEM((NSA_DH + ONES_ROWS, S), BF16),
                        pltpu.VMEM((S, J * TA), F32), pltpu.VMEM((2 * TA, J * TA), F32)],
        compiler_params=_cparams(("parallel", "parallel")),
        name="nsa_attention",
    )(proj, proj, proj, kcmp, vcmp_t, bias_cmp_t, toe_g, erep, esel, ovl_t)


def _diff_kernel(q_ref, k_ref, v_ref, toe_ref, lam_ref, sub_ref, id_ref, o_ref, vt_ref, s_ref, *, n_tiles, lam_init):
    hd = DIFF_DH
    lane = lax.broadcasted_iota(jnp.int32, (TA, 2 * hd), 1)
    lv = lam_ref[...]
    lam = (jnp.exp(jnp.sum(lv[0:1] * lv[1:2], axis=-1, keepdims=True))
           - jnp.exp(jnp.sum(lv[2:3] * lv[3:4], axis=-1, keepdims=True)) + lam_init)
    vt_ref[0:2 * hd, :] = _nt(id_ref[...], v_ref[...]).astype(BF16)
    vt_ref[2 * hd:, :] = jnp.ones((ONES_ROWS, vt_ref.shape[1]), BF16)

    def both(t):
        return jnp.concatenate([t, t], axis=1)

    def make_job(qi):
        rows = slice(qi * TA, (qi + 1) * TA)

        def score():
            q = q_ref[rows, :]
            zero = jnp.zeros_like(q)
            q2 = jnp.concatenate([jnp.where(lane < hd, q, zero), jnp.where(lane >= hd, q, zero)], axis=0)
            s = _nt(k_ref[0:(qi + 1) * TA, :], q2)
            near = both(toe_ref[0, TA:2 * TA, :]) if qi > 0 else None
            return _bias_tail(s, near, both(toe_ref[0, 0:TA, :]))

        def finish(acc, l):
            o2 = acc * (1.0 / l)
            o = (o2[:, 0:TA] - lam * o2[:, TA:2 * TA]).T
            o = o * lax.rsqrt(jnp.mean(o * o, axis=-1, keepdims=True) + RMS_EPS) * sub_ref[...]
            o_ref[rows, :] = (o * (1.0 - lam_init)).astype(BF16)

        return _SoftmaxJob(qi + 1, score, lambda c: vt_ref[:, c * TA:(c + 1) * TA], s_ref.at[qi % 2], finish)

    _run_interleaved([make_job(qi) for qi in range(n_tiles)])


def _diff_attention(proj, toe_t, lam_vec, subln, ident, layer_idx, B, S):
    H = DIFF_HEADS
    hw = 2 * DIFF_DH
    lam_init = 0.8 - 0.6 * math.exp(-0.3 * layer_idx)
    return pl.pallas_call(
        functools.partial(_diff_kernel, n_tiles=S // TA, lam_init=lam_init),
        grid=(B, H),
        in_specs=[
            pl.BlockSpec((S, hw), lambda b, h: (b, h)),
            pl.BlockSpec((S, hw), lambda b, h: (b, H + h)),
            pl.BlockSpec((S, hw), lambda b, h: (b, 2 * H + h)),
            pl.BlockSpec((1, 3 * TA, TA), lambda b, h: (h, 0, 0)),
            pl.BlockSpec((4, DIFF_DH), lambda b, h: (0, 0)),
            pl.BlockSpec((1, hw), lambda b, h: (0, 0)),
            pl.BlockSpec((hw, hw), lambda b, h: (0, 0)),
        ],
        out_specs=pl.BlockSpec((S, hw), lambda b, h: (b, h)),
        out_shape=jax.ShapeDtypeStruct((B * S, H * hw), BF16),
        scratch_shapes=[pltpu.VMEM((hw + ONES_ROWS, S), BF16), pltpu.VMEM((2, S, 2 * TA), F32)],
        compiler_params=_cparams(("parallel", "parallel")),
        name="diff_attention",
    )(proj, proj, proj, toe_t, lam_vec, subln.reshape(1, hw), ident)


def _tail_kernel(oa_ref, ob_ref, wo_ref, x_ref, g1_ref, lg1_ref, lb1_ref, sc_ref, sh_ref, g2_ref,
                 wu_ref, wg_ref, cw_ref, cb_ref, wd_ref, lg2_ref, lb2_ref, o_ref, hprev_ref, act_ref,
                 *, per_seq, tf):
    i = pl.program_id(0)
    half = oa_ref.shape[1]
    tm = x_ref.shape[0]
    y = _mm(oa_ref[...], wo_ref[:half, :]) + _mm(ob_ref[...], wo_ref[half:, :])
    x1 = _layer_norm_rows(ALPHA * x_ref[...] + (1.0 + g1_ref[0]) * y, lg1_ref[...], lb1_ref[...])
    o_ref[...] = x1
    h = (x1 * (1.0 + sc_ref[0]) + sh_ref[0]).astype(BF16)

    @pl.when(i % per_seq == 0)
    def _():
        hprev_ref[...] = jnp.zeros(hprev_ref.shape, BF16)

    h_ext = jnp.concatenate([hprev_ref[...], h], axis=0)
    hprev_ref[...] = h[tm - CONV_HALO:, :]
    for f in range(D_FF // tf):
        cs = slice(f * tf, (f + 1) * tf)
        u = _mm(h, wu_ref[:, cs])
        ge = _mm(h_ext, wg_ref[:, cs])
        cw = cw_ref[:, cs]
        a = (cw[2:3] * ge[CONV_HALO:CONV_HALO + tm] + cw[1:2] * ge[CONV_HALO - 1:CONV_HALO - 1 + tm]
             + cw[0:1] * ge[CONV_HALO - 2:CONV_HALO - 2 + tm] + cb_ref[:, cs])
        act_ref[:, cs] = (a / (1.0 + jnp.exp(-a)) * u).astype(BF16)
    y2 = _mm(act_ref[...], wd_ref[...])
    z = ALPHA * o_ref[...] + (1.0 + g2_ref[0]) * y2
    o_ref[...] = _layer_norm_rows(z, lg2_ref[...], lb2_ref[...])


def _layer_tail(o_a, o_b, blk_a, blk_b, w_o, x, g1, ln1_g, ln1_b, scale, shift, g2, w_up, w_gate, conv_w, conv_b,
                w_down, ln2_g, ln2_b, S):
    T, D = x.shape
    tm, tf = 1024, 256
    half = w_o.shape[0] // 2
    per_seq = S // tm
    once = pl.Buffered(1)

    def const(shape):
        return pl.BlockSpec(shape, lambda i: (0,) * len(shape), pipeline_mode=once)

    mod_spec = pl.BlockSpec((1, 1, D), lambda i: (i // per_seq, 0, 0))
    return pl.pallas_call(
        functools.partial(_tail_kernel, per_seq=per_seq, tf=tf),
        grid=(T // tm,),
        in_specs=[
            pl.BlockSpec((tm, half), lambda i: (i, blk_a)),
            pl.BlockSpec((tm, half), lambda i: (i, blk_b)),
            const((2 * half, D)),
            pl.BlockSpec((tm, D), lambda i: (i, 0)),
            mod_spec, const((1, D)), const((1, D)),
            mod_spec, mod_spec, mod_spec,
            const((D, D_FF)), const((D, D_FF)), const((3, D_FF)), const((1, D_FF)), const((D_FF, D)),
            const((1, D)), const((1, D)),
        ],
        out_specs=pl.BlockSpec((tm, D), lambda i: (i, 0)),
        out_shape=jax.ShapeDtypeStruct((T, D), F32),
        scratch_shapes=[pltpu.VMEM((CONV_HALO, D), BF16), pltpu.VMEM((tm, D_FF), BF16)],
        compiler_params=_cparams(("arbitrary",)),
        name="layer_tail",
    )(o_a, o_b, w_o, x, g1, ln1_g, ln1_b, scale, shift, g2, w_up, w_gate, conv_w, conv_b, w_down, ln2_g, ln2_b)


def _take_columns(w, src, mul):
    parts = []
    start = 0
    n = len(src)
    while start < n:
        end = start + 1
        while end < n and mul[end] == mul[start] and (mul[start] == 0.0 or src[end] == src[end - 1] + 1):
            end += 1
        if mul[start] == 0.0:
            parts.append(jnp.zeros((w.shape[0], end - start), w.dtype))
        else:
            piece = w[:, int(src[start]):int(src[start]) + end - start]
            parts.append(piece if mul[start] == 1.0 else piece * float(mul[start]))
        start = end
    return jnp.concatenate(parts, axis=1)


def _even_in_columns():
    src = np.zeros(EVEN_W, np.int32)
    mul = np.zeros(EVEN_W, np.float32)

    def put(dst, cols, scale=1.0):
        cols = np.asarray(cols)
        src[dst:dst + len(cols)] = cols
        mul[dst:dst + len(cols)] = scale

    put(0, np.arange(MLA_Q_RANK + MLA_KV_RANK))
    rope0 = MLA_Q_RANK + MLA_KV_RANK
    half = MLA_ROPE // 2
    put(MISC_COL, rope0 + np.arange(MLA_ROPE))
    put(MISC_COL + MLA_ROPE, rope0 + half + np.arange(half), -1.0)
    put(MISC_COL + MLA_ROPE + half, rope0 + np.arange(half), 1.0)
    nsa0 = MLA_COLS
    put(MISC_COL + 2 * MLA_ROPE, nsa0 + NSA_Q_COLS + NSA_KV_COLS + np.arange(NSA_GATE_COLS))
    put(NSA_Q_COL, nsa0 + np.arange(NSA_Q_COLS), NSA_DH ** -0.5 * LOG2E)
    kv0 = nsa0 + NSA_Q_COLS

    def chunk(branch, kv, g):
        return kv0 + ((branch * 2 + kv) * NSA_GROUPS + g) * NSA_DH + np.arange(NSA_DH)

    for g in range(NSA_GROUPS):
        base = NSA_KV_COL + g * 4 * NSA_DH
        for slot, (branch, kv) in enumerate([(1, 0), (2, 0), (1, 1), (2, 1)]):
            put(base + slot * NSA_DH, chunk(branch, kv, g))
    put(NSA_CMP_COL, kv0 + np.arange(2 * NSA_GROUPS * NSA_DH))
    return src, mul


def _mla_up_columns():
    hw = MLA_HEADS * LANES
    qd = MLA_NOPE + MLA_ROPE
    half = MLA_ROPE // 2
    src = np.zeros(2 * hw, np.int32)
    mul = np.zeros(2 * hw, np.float32)
    for h in range(MLA_HEADS):
        src[h * LANES:h * LANES + qd] = h * qd + np.arange(qd)
        mul[h * LANES:h * LANES + qd] = 1.0
        r = hw + h * LANES + MLA_NOPE
        src[r:r + half] = h * qd + MLA_NOPE + half + np.arange(half)
        mul[r:r + half] = -1.0
        src[r + half:r + MLA_ROPE] = h * qd + MLA_NOPE + np.arange(half)
        mul[r + half:r + MLA_ROPE] = 1.0
    ksrc = np.zeros(hw, np.int32)
    kmul = np.zeros(hw, np.float32)
    vsrc = np.zeros(MLA_HEADS * MLA_V, np.int32)
    for h in range(MLA_HEADS):
        ksrc[h * LANES:h * LANES + MLA_NOPE] = h * (MLA_NOPE + MLA_V) + np.arange(MLA_NOPE)
        kmul[h * LANES:h * LANES + MLA_NOPE] = 1.0
        vsrc[h * MLA_V:(h + 1) * MLA_V] = h * (MLA_NOPE + MLA_V) + MLA_NOPE + np.arange(MLA_V)
    return src, mul, ksrc, kmul, vsrc


def _routing_constants(S):
    hw = MLA_HEADS * LANES
    erope = np.zeros((LANES, hw), np.float32)
    for h in range(MLA_HEADS):
        for i in range(MLA_ROPE):
            erope[i, h * LANES + MLA_NOPE + i] = 1.0
            erope[MLA_ROPE + i, h * LANES + MLA_NOPE + i] = 1.0
    W = NSA_HPG * NSA_DH
    erep = np.zeros((2, W, W), np.float32)
    for slot in range(2):
        for d in range(NSA_DH):
            for j in range(NSA_HPG):
                erep[slot, slot * NSA_DH + d, j * NSA_DH + d] = 1.0
    esel = np.zeros((2, NSA_DH, W), np.float32)
    for slot in range(2):
        esel[slot, np.arange(NSA_DH), (2 + slot) * NSA_DH + np.arange(NSA_DH)] = 1.0
    n_slc = S // SEL_LEN
    starts = np.arange(N_CMP_PAD) * CMP_STRIDE
    jb = np.arange(n_slc)
    ovl = ((starts[:, None] < (jb[None, :] + 1) * SEL_LEN)
           & (starts[:, None] + CMP_LEN > jb[None, :] * SEL_LEN)).astype(np.float32)
    ovl[(S - CMP_LEN) // CMP_STRIDE + 1:, :] = 0.0
    cmask_t = np.where(np.arange(TA)[None, :] >= np.arange(TA)[:, None], 0.0, NEG_INF).astype(np.float32)
    return erope, erep, esel, np.ascontiguousarray(ovl.T), cmask_t


def _rope_tables(S):
    inv = 1.0 / (ROPE_BASE ** (jnp.arange(0, MLA_ROPE, 2, dtype=jnp.float32) / MLA_ROPE))
    ang = jnp.arange(S, dtype=jnp.float32)[:, None] * inv[None, :]
    cos, sin = jnp.cos(ang), jnp.sin(ang)
    cos2 = jnp.concatenate([cos, cos], axis=-1)
    sin2 = jnp.concatenate([sin, sin], axis=-1)
    scale = (MLA_NOPE + MLA_ROPE) ** -0.5 * LOG2E
    z32 = jnp.zeros((S, LANES - MLA_NOPE - MLA_ROPE), F32)
    ctab = jnp.concatenate([jnp.full((S, MLA_NOPE), scale, F32), cos2 * scale, z32], axis=-1)
    stab = jnp.concatenate([jnp.zeros((S, MLA_NOPE), F32), sin2 * scale, z32], axis=-1)
    mrope = jnp.concatenate([cos2, sin2, jnp.zeros((S, LANES - 2 * MLA_ROPE), F32)], axis=-1)
    return ctab, stab, mrope


def kernel(x, c, rel_bias, ev_w_in, mla_q_norm, mla_kv_norm, mla_w_uq, mla_w_ukv, nsa_cmp_pe, nsa_cmp_w1,
           nsa_cmp_w2, ev_w_o, od_w_in, diff_lambda, diff_subln, od_w_o, ada_w, ada_b, ln_g, ln_b, ffn_w_up,
           ffn_w_gate, ffn_conv_w, ffn_conv_b, ffn_w_down):
    B, S, D = x.shape
    assert D == D_MODEL and S % TA == 0 and S // CMP_STRIDE == N_CMP_PAD and S // SEL_LEN <= LANES
    far_np = np.arange(TA + 1, max(S, TA + 2))
    far_bucket = 16 + (np.log(far_np.astype(np.float32) / 16) / math.log(MAX_DISTANCE / 16) * 16).astype(np.int32)
    assert far_bucket.min() >= FAR_BUCKET
    T = B * S
    xf = x.reshape(T, D)

    mod = _ada_mod(c, ada_w, ada_b)
    toe_t_idx, cmp_t_idx = _bias_index_tiles(S)
    toe_t = _expand_bias(rel_bias, toe_t_idx, rel_far=True)
    bias_cmp_t = _expand_bias(rel_bias, cmp_t_idx)
    G, J, NQ = NSA_GROUPS, NSA_HPG, S // TA
    toe_g = toe_t.reshape(G, J, 3, TA, TA).transpose(0, 2, 3, 1, 4).reshape(G, 3, TA, J * TA)
    bias_cmp_t = bias_cmp_t.reshape(G, J, N_CMP_PAD, NQ, TA).transpose(0, 3, 2, 1, 4).reshape(G, NQ, N_CMP_PAD, J * TA)
    erope, erep, esel, ovl_t, cmask_t = _routing_constants(S)
    ident = jnp.eye(LANES, dtype=BF16)
    ctab, stab, mrope = _rope_tables(S)
    ev_src, ev_mul = _even_in_columns()
    q_src, q_mul, k_src, k_mul, v_src = _mla_up_columns()

    def bf(a):
        return jnp.asarray(a).astype(BF16)

    for l in range(DEPTH):
        sh1, sc1, g1, sh2, sc2, g2 = [mod[l, :, k * D:(k + 1) * D].reshape(B, 1, D) for k in range(6)]
        i = l // 2
        if l % 2 == 0:
            w_in = bf(_take_columns(ev_w_in[i], ev_src, ev_mul))
            proj, cmp_planes = _mod_matmul(xf, sc1, sh1, w_in, S, planes_col=NSA_CMP_COL)
            wq = bf(_take_columns(mla_w_uq[i], q_src, q_mul))
            wk = bf(_take_columns(mla_w_ukv[i], k_src, k_mul))
            wv = bf(_take_columns(mla_w_ukv[i], v_src, np.ones(len(v_src), np.float32)))
            q_m, k_m, v_m = _mla_prep(proj, mla_q_norm[i].reshape(1, -1), mla_kv_norm[i].reshape(1, -1),
                                      wq, wk, wv, bf(erope), ctab, stab, mrope, S)
            o_1 = _mla_attention(q_m, k_m, v_m, jnp.asarray(cmask_t), ident, B, S)
            r = cmp_planes.reshape(4, B, N_CMP_PAD, CMP_STRIDE * NSA_DH)
            w2_rep = jnp.tile(nsa_cmp_w2[i], (1, 1, NSA_HPG))
            kcmp, vcmp_t = _compress(r, nsa_cmp_pe[i].reshape(2, 1, CMP_LEN * NSA_DH), bf(nsa_cmp_w1[i]), bf(w2_rep),
                                     bf(nsa_cmp_w2[i].transpose(0, 2, 1)), B)
            o_2 = _nsa_attention(proj, kcmp, vcmp_t, bias_cmp_t, toe_g, bf(erep), bf(esel), bf(ovl_t), B, S)
            blk_2, w_o = 0, ev_w_o[i]
        else:
            w_in = od_w_in[i].at[:, :DIFF_HEADS * 2 * DIFF_DH].multiply(DIFF_DH ** -0.5 * LOG2E)
            proj, = _mod_matmul(xf, sc1, sh1, bf(w_in), S)
            o_1 = _diff_attention(proj, toe_t, diff_lambda[i], diff_subln[i], ident, l, B, S)
            o_2, blk_2, w_o = o_1, 1, od_w_o[i]
        xf = _layer_tail(o_1, o_2, 0, blk_2, bf(w_o), xf, g1, ln_g[l, 0:1], ln_b[l, 0:1], sc2, sh2, g2,
                         bf(ffn_w_up[l]), bf(ffn_w_gate[l]), ffn_conv_w[l], ffn_conv_b[l].reshape(1, D_FF),
                         bf(ffn_w_down[l]), ln_g[l, 1:2], ln_b[l, 1:2], S)
    return xf.reshape(B, S, D)
```
